```python
import math
import functools
import jax
import jax.numpy as jnp
from jax import lax
import numpy as np

D_MODEL = 1024
BATCH = 4
SEQ = 4096
DEPTH = 4

GRID_W = 64
CTX_LEN = 256
CHUNK = 128
NORM_EPS = 1e-6
ROPE_BASE = 10000.0
NEG_INF = -1e30

RET_HEADS = 4
RET_DK = 128
RET_DV = 256
RET_QK_W = RET_HEADS * RET_DK
RET_V_W = RET_HEADS * RET_DV

SSM_HEADS = 16
SSM_HEAD_DIM = 64
SSM_GROUPS = 2
SSM_STATE = 128
SSM_CONV = 3
SSM_INNER = SSM_HEADS * SSM_HEAD_DIM
SSM_BC_W = SSM_GROUPS * SSM_STATE
SSM_XBC_W = SSM_INNER + 2 * SSM_BC_W

ATT_HEADS = 8
ATT_KV_HEADS = 2
ATT_HEAD_DIM = 128
ATT_WINDOW = 128
ATT_BLOCK = 128
ATT_Q_W = ATT_HEADS * ATT_HEAD_DIM
ATT_KV_W = ATT_KV_HEADS * ATT_HEAD_DIM

N_BRANCH = 3
BRANCH_W = RET_V_W

D_FF = 2816
N_EXPERTS = 8
TOP_K = 2
MOE_BLOCK = 128
N_DENSE = (DEPTH + 1) // 2
N_MOE = DEPTH // 2

IN_SPLITS = (RET_QK_W, RET_QK_W, RET_V_W, RET_V_W, SSM_INNER, SSM_XBC_W, SSM_HEADS, ATT_Q_W, ATT_KV_W, ATT_KV_W, N_BRANCH * D_MODEL)
IN_W = sum(IN_SPLITS)

kernel_name = 'hybrid_ret_ssd_swa_moe_dit'


def split_cols(t, sizes):
    return jnp.split(t, np.cumsum(sizes)[:-1].tolist(), axis=-1)


def rms_norm(t, w):
    tf = t.astype(jnp.float32)
    y = tf * lax.rsqrt(jnp.mean(tf * tf, axis=-1, keepdims=True) + NORM_EPS)
    return (y * w.astype(jnp.float32)).astype(t.dtype)


def head_rms(t):
    return t * lax.rsqrt(jnp.mean(t * t, axis=-1, keepdims=True) + NORM_EPS)


def adaln(cond, w, b):
    m = jnp.dot(jax.nn.silu(cond), w) + b
    return m.reshape(*cond.shape[:-1], 6, D_MODEL)


def rope_angles(pos, dim):
    inv = ROPE_BASE ** (-jnp.arange(0, dim, 2, dtype=jnp.float32) / dim)
    return pos.astype(jnp.float32)[:, None] * inv[None, :]


def apply_rope(t, ang):
    half = t.shape[-1] // 2
    cos = jnp.cos(ang)[None, :, None, :]
    sin = jnp.sin(ang)[None, :, None, :]
    t1 = t[..., :half].astype(jnp.float32)
    t2 = t[..., half:].astype(jnp.float32)
    return jnp.concatenate([t1 * cos - t2 * sin, t1 * sin + t2 * cos], axis=-1).astype(t.dtype)


def axial_rope(t):
    n_tok, dim = t.shape[1], t.shape[-1]
    rows = n_tok // GRID_W
    row = jnp.repeat(jnp.arange(rows), GRID_W)
    col = jnp.tile(jnp.arange(GRID_W), rows)
    half = dim // 2
    return jnp.concatenate([apply_rope(t[..., :half], rope_angles(row, half)),
                            apply_rope(t[..., half:], rope_angles(col, half))], axis=-1)


def keep_seq(t):
    return t


def flip_seq(t):
    return jnp.flip(t, axis=1)


def retention_chunked(q, k, v, log_g, s0, with_out):
    bsz, T, H, dk = q.shape
    dv = v.shape[-1]
    n = T // CHUNK
    qc = q.reshape(bsz, n, CHUNK, H, dk)
    kc = k.reshape(bsz, n, CHUNK, H, dk)
    vc = v.reshape(bsz, n, CHUNK, H, dv)
    i = jnp.arange(CHUNK, dtype=jnp.float32)
    w_state = jnp.exp((CHUNK - 1 - i)[:, None] * log_g[None, :])
    u = jnp.einsum('bnjhd,bnjhe,jh->bnhde', kc, vc, w_state)
    chunk_decay = jnp.exp(CHUNK * log_g)[None, :, None, None]

    def step(s, u_n):
        return chunk_decay * s + u_n, s

    s_last, s_prev = lax.scan(step, s0, jnp.moveaxis(u, 1, 0))
    if not with_out:
        return None, s_last
    s_prev = jnp.moveaxis(s_prev, 0, 1)
    diff = i[:, None] - i[None, :]
    dmat = jnp.where(diff[None] >= 0, jnp.exp(jnp.maximum(diff, 0.0)[None] * log_g[:, None, None]), 0.0)
    scores = jnp.einsum('bnihd,bnjhd->bnhij', qc, kc) * dmat
    o = jnp.einsum('bnhij,bnjhe->bnihe', scores, vc)
    w_q = jnp.exp((i + 1)[:, None] * log_g[None, :])
    o = o + jnp.einsum('bnihd,bnhde,ih->bnihe', qc, s_prev, w_q)
    return o.reshape(bsz, T, H, dv), s_last


def retention_branch(q_c, k_c, v_c, g_c, q_x, k_x, v_x, g_x, ret_decay, need_ctx):
    bsz, S = q_x.shape[:2]
    L = q_c.shape[1]

    def heads(t, d):
        return t.astype(jnp.float32).reshape(bsz, t.shape[1], RET_HEADS, d)

    scale = RET_DK ** -0.5
    ang = rope_angles(jnp.arange(S), RET_DK)
    qc, kc, vc = heads(q_c, RET_DK), heads(k_c, RET_DK) * scale, heads(v_c, RET_DV)
    qx = apply_rope(heads(q_x, RET_DK), ang)
    kx = apply_rope(heads(k_x, RET_DK), ang) * scale
    vx = heads(v_x, RET_DV)
    log_g = jax.nn.log_sigmoid(ret_decay.astype(jnp.float32))
    s0 = jnp.zeros((bsz, RET_HEADS, RET_DK, RET_DV), jnp.float32)
    oc = 0.0
    ox = 0.0
    for d, o in enumerate((keep_seq, flip_seq)):
        oc_d, s_ctx = retention_chunked(o(qc), o(kc), o(vc), log_g[d], s0, need_ctx)
        ox_d, _ = retention_chunked(o(qx), o(kx), o(vx), log_g[d], s_ctx, True)
        ox = ox + o(ox_d)
        if need_ctx:
            oc = oc + o(oc_d)
    out_x = (jax.nn.silu(g_x.astype(jnp.float32)) * head_rms(ox).reshape(bsz, S, RET_V_W)).astype(g_x.dtype)
    out_c = None
    if need_ctx:
        out_c = (jax.nn.silu(g_c.astype(jnp.float32)) * head_rms(oc).reshape(bsz, L, RET_V_W)).astype(g_c.dtype)
    return out_c, out_x


def short_conv_silu(t, w, b):
    pad = SSM_CONV // 2
    y = lax.conv_general_dilated(t, w[:, None, :].astype(t.dtype), window_strides=(1,), padding=[(pad, pad)],
                                 dimension_numbers=('NWC', 'WIO', 'NWC'), feature_group_count=t.shape[-1])
    return jax.nn.silu(y + b)


def ssd_chunked(xh, dt, a, bmat, cmat, d_skip, s0, with_out):
    bsz, T, G, Hg, P = xh.shape
    N = bmat.shape[-1]
    n = T // CHUNK
    xc = xh.reshape(bsz, n, CHUNK, G, Hg, P)
    dtc = dt.reshape(bsz, n, CHUNK, G, Hg)
    bc = bmat.reshape(bsz, n, CHUNK, G, N)
    cc = cmat.reshape(bsz, n, CHUNK, G, N)
    cum = jnp.cumsum(dtc * a, axis=2)
    xdt = xc * dtc[..., None]
    w_state = jnp.exp(cum[:, :, -1:] - cum)
    u = jnp.einsum('bcjgs,bcjgh,bcjghp->bcghps', bc, w_state, xdt)
    chunk_decay = jnp.exp(cum[:, :, -1])

    def step(s, inp):
        u_c, dec = inp
        return dec[..., None, None] * s + u_c, s

    s_last, s_prev = lax.scan(step, s0, (jnp.moveaxis(u, 1, 0), jnp.moveaxis(chunk_decay, 1, 0)))
    if not with_out:
        return None, s_last
    s_prev = jnp.moveaxis(s_prev, 0, 1)
    cum_t = jnp.moveaxis(cum, 2, -1)
    seg = cum_t[..., :, None] - cum_t[..., None, :]
    lower_tri = jnp.tril(jnp.ones((CHUNK, CHUNK), bool))
    lmat = jnp.where(lower_tri, jnp.exp(jnp.where(lower_tri, seg, 0.0)), 0.0)
    cb = jnp.einsum('bcigs,bcjgs->bcgij', cc, bc)
    y = jnp.einsum('bcgij,bcghij,bcjghp->bcighp', cb, lmat, xdt)
    y = y + jnp.einsum('bcigs,bcghps,bcigh->bcighp', cc, s_prev, jnp.exp(cum))
    y = y + xc * d_skip[..., None]
    return y.reshape(bsz, T, G, Hg, P), s_last


def gated_group_rms(y, z, w):
    bsz, T = y.shape[:2]
    gw = SSM_INNER // SSM_GROUPS
    g = y.reshape(bsz, T, SSM_GROUPS, gw) * jax.nn.silu(z.astype(jnp.float32)).reshape(bsz, T, SSM_GROUPS, gw)
    g = head_rms(g).reshape(bsz, T, SSM_INNER)
    return (g * w.astype(jnp.float32)).astype(z.dtype)


def ssd_branch(z_c, xbc_c, dt_c, z_x, xbc_x, dt_x, conv_w, conv_b, dt_bias, a_log, d_skip, norm_w, need_ctx):
    hg = SSM_HEADS // SSM_GROUPS

    def prep(xbc, dt_raw):
        bsz, T = xbc.shape[:2]
        u = short_conv_silu(xbc, conv_w, conv_b).astype(jnp.float32)
        xs, bm, cm = split_cols(u, (SSM_INNER, SSM_BC_W, SSM_BC_W))
        return (xs.reshape(bsz, T, SSM_GROUPS, hg, SSM_HEAD_DIM),
                bm.reshape(bsz, T, SSM_GROUPS, SSM_STATE),
                cm.reshape(bsz, T, SSM_GROUPS, SSM_STATE),
                dt_raw.astype(jnp.float32).reshape(bsz, T, SSM_GROUPS, hg))

    xc, bc, cc, dtc = prep(xbc_c, dt_c)
    xx, bx, cx, dtx = prep(xbc_x, dt_x)
    s0 = jnp.zeros((xx.shape[0], SSM_GROUPS, hg, SSM_HEAD_DIM, SSM_STATE), jnp.float32)
    yc = 0.0
    yx = 0.0
    for d, o in enumerate((keep_seq, flip_seq)):
        a = -jnp.exp(a_log[d].astype(jnp.float32)).reshape(SSM_GROUPS, hg)
        bias = dt_bias[d].astype(jnp.float32).reshape(SSM_GROUPS, hg)
        skip = d_skip[d].astype(jnp.float32).reshape(SSM_GROUPS, hg)
        yc_d, s_ctx = ssd_chunked(o(xc), jax.nn.softplus(o(dtc) + bias), a, o(bc), o(cc), skip, s0, need_ctx)
        yx_d, _ = ssd_chunked(o(xx), jax.nn.softplus(o(dtx) + bias), a, o(bx), o(cx), skip, s_ctx, True)
        yx = yx + o(yx_d)
        if need_ctx:
            yc = yc + o(yc_d)
    out_x = gated_group_rms(yx, z_x, norm_w)
    out_c = gated_group_rms(yc, z_c, norm_w) if need_ctx else None
    return out_c, out_x


def window_attention_branch(q_c, k_c, v_c, q_x, k_x, v_x, sink, need_ctx):
    bsz, S = q_x.shape[:2]
    L = k_c.shape[1]
    G = ATT_HEADS // ATT_KV_HEADS
    nb = S // ATT_BLOCK
    scale = ATT_HEAD_DIM ** -0.5
    kc = k_c.reshape(bsz, L, ATT_KV_HEADS, ATT_HEAD_DIM)
    vc = v_c.reshape(bsz, L, ATT_KV_HEADS, ATT_HEAD_DIM)
    qx = axial_rope(q_x.reshape(bsz, S, ATT_HEADS, ATT_HEAD_DIM)) * scale
    kx = axial_rope(k_x.reshape(bsz, S, ATT_KV_HEADS, ATT_HEAD_DIM))
    vx = v_x.reshape(bsz, S, ATT_KV_HEADS, ATT_HEAD_DIM)
    sink_kg = sink.astype(jnp.float32).reshape(ATT_KV_HEADS, G)

    def banded(t):
        tp = jnp.pad(t, ((0, 0), (ATT_BLOCK, ATT_BLOCK), (0, 0), (0, 0)))
        tp = tp.reshape(bsz, nb + 2, ATT_BLOCK, ATT_KV_HEADS, ATT_HEAD_DIM)
        return jnp.concatenate([tp[:, :-2], tp[:, 1:-1], tp[:, 2:]], axis=2)

    qb = qx.reshape(bsz, nb, ATT_BLOCK, ATT_KV_HEADS, G, ATT_HEAD_DIM)
    kb, vb = banded(kx), banded(vx)
    s_win = jnp.einsum('bnikgd,bnjkd->bnkgij', qb, kb).astype(jnp.float32)
    qi = jnp.arange(ATT_BLOCK)[:, None]
    kj = jnp.arange(3 * ATT_BLOCK)[None, :]
    kpos = (jnp.arange(nb) * ATT_BLOCK - ATT_BLOCK)[:, None, None] + kj[None]
    valid = (jnp.abs(qi + ATT_BLOCK - kj) <= ATT_WINDOW)[None] & (kpos >= 0) & (kpos < S)
    s_win = jnp.where(valid[None, :, None, None], s_win, NEG_INF)
    s_ctx = jnp.einsum('bnikgd,bjkd->bnkgij', qb, kc).astype(jnp.float32)
    s_sink = jnp.broadcast_to(sink_kg[None, None, :, :, None, None], s_ctx.shape[:-1] + (1,))
    p = jax.nn.softmax(jnp.concatenate([s_sink, s_ctx, s_win], axis=-1), axis=-1)
    o = (jnp.einsum('bnkgij,bjkd->bnikgd', p[..., 1:1 + L], vc.astype(jnp.float32))
         + jnp.einsum('bnkgij,bnjkd->bnikgd', p[..., 1 + L:], vb.astype(jnp.float32)))
    out_x = o.reshape(bsz, S, ATT_Q_W).astype(q_x.dtype)
    out_c = None
    if need_ctx:
        qc = q_c.reshape(bsz, L, ATT_KV_HEADS, G, ATT_HEAD_DIM) * scale
        s_cc = jnp.einsum('bikgd,bjkd->bkgij', qc, kc).astype(jnp.float32)
        s_sink_c = jnp.broadcast_to(sink_kg[None, :, :, None, None], s_cc.shape[:-1] + (1,))
        pc = jax.nn.softmax(jnp.concatenate([s_sink_c, s_cc], axis=-1), axis=-1)
        out_c = jnp.einsum('bkgij,bjkd->bikgd', pc[..., 1:], vc.astype(jnp.float32)).reshape(bsz, L, ATT_Q_W).astype(q_c.dtype)
    return out_c, out_x


def merge_branches(outs, gate_logits, w_branch, w_out):
    br = jnp.einsum('btnw,nwd->btnd', jnp.stack(outs, axis=2), w_branch)
    g = jax.nn.sigmoid(gate_logits.astype(jnp.float32)).reshape(*gate_logits.shape[:-1], N_BRANCH, D_MODEL)
    merged = jnp.sum(g * br, axis=2).astype(outs[0].dtype)
    return merged @ w_out


def swiglu(h, w1, w3, w2):
    return (jax.nn.silu(h @ w1) * (h @ w3)) @ w2


def moe_swiglu(h, w_router, w1, w3, w2):
    bsz, T, D = h.shape
    n_tok = bsz * T
    u = h.reshape(n_tok, D)
    logits = jnp.dot(u, w_router).astype(jnp.float32)
    top_logit, top_e = lax.top_k(logits, TOP_K)
    top_w = jax.nn.softmax(top_logit, axis=-1)
    n_asg = n_tok * TOP_K
    flat_e = top_e.reshape(n_asg)
    flat_tok = jnp.repeat(jnp.arange(n_tok, dtype=jnp.int32), TOP_K)
    flat_w = top_w.reshape(n_asg)
    order = jnp.argsort(flat_e)
    e_sorted = flat_e[order]
    counts = jnp.bincount(flat_e, length=N_EXPERTS)
    padded = (counts + MOE_BLOCK - 1) // MOE_BLOCK * MOE_BLOCK
    start = jnp.cumsum(counts) - counts
    pend = jnp.cumsum(padded)
    pstart = pend - padded
    slot = pstart[e_sorted] + (jnp.arange(n_asg) - start[e_sorted])
    n_slots = (-(-n_asg // MOE_BLOCK) + N_EXPERTS) * MOE_BLOCK
    n_blocks = n_slots // MOE_BLOCK
    slot_tok = jnp.zeros((n_slots,), jnp.int32).at[slot].set(flat_tok[order])
    slot_w = jnp.zeros((n_slots,), jnp.float32).at[slot].set(flat_w[order])
    blk_e = jnp.minimum(jnp.searchsorted(pend, jnp.arange(n_blocks) * MOE_BLOCK, side='right'), N_EXPERTS - 1)
    xb = u[slot_tok].reshape(n_blocks, MOE_BLOCK, D)

    def expert_block(args):
        xe, e = args
        return (jax.nn.silu(xe @ w1[e]) * (xe @ w3[e])) @ w2[e]

    yb = lax.map(expert_block, (xb, blk_e)).reshape(n_slots, D)
    y = jnp.zeros((n_tok, D), jnp.float32).at[slot_tok].add(yb.astype(jnp.float32) * slot_w[:, None])
    return y.reshape(bsz, T, D).astype(h.dtype)


def mixer_sublayer(xc, x, mods_c, mods_x, n_pre, n_post, w_in, ret_decay, conv_w, conv_b, dt_bias, a_log,
                   d_skip, ssm_norm_w, sink, w_branch, w_out, need_ctx):
    L = xc.shape[1]
    hc = rms_norm(xc, n_pre) * (1 + mods_c[1]) + mods_c[0]
    hx = rms_norm(x, n_pre) * (1 + mods_x[1]) + mods_x[0]
    proj = jnp.concatenate([hc, hx], axis=1) @ w_in
    rq, rk, rv, rg, sz, sxbc, sdt, aq, ak, av, bgate = split_cols(proj, IN_SPLITS)

    def pc(t):
        return t[:, :L]

    def px(t):
        return t[:, L:]

    ret_c, ret_x = retention_branch(pc(rq), pc(rk), pc(rv), pc(rg), px(rq), px(rk), px(rv), px(rg), ret_decay, need_ctx)
    ssd_c, ssd_x = ssd_branch(pc(sz), pc(sxbc), pc(sdt), px(sz), px(sxbc), px(sdt), conv_w, conv_b, dt_bias,
                              a_log, d_skip, ssm_norm_w, need_ctx)
    att_c, att_x = window_attention_branch(pc(aq), pc(ak), pc(av), px(aq), px(ak), px(av), sink, need_ctx)
    if need_ctx:
        outs = [jnp.concatenate([oc, ox], axis=1) for oc, ox in ((ret_c, ret_x), (ssd_c, ssd_x), (att_c, att_x))]
        y = merge_branches(outs, bgate, w_branch, w_out)
        xc_new = xc + mods_c[2] * rms_norm(y[:, :L], n_post)
        y = y[:, L:]
    else:
        y = merge_branches([ret_x, ssd_x, att_x], px(bgate), w_branch, w_out)
        xc_new = None
    return xc_new, x + mods_x[2] * rms_norm(y, n_post)


def ffn_sublayer(xc, x, mods_c, mods_x, n_pre, n_post, ffn):
    hx = rms_norm(x, n_pre) * (1 + mods_x[4]) + mods_x[3]
    if xc is None:
        return None, x + mods_x[5] * rms_norm(ffn(hx), n_post)
    L = xc.shape[1]
    hc = rms_norm(xc, n_pre) * (1 + mods_c[4]) + mods_c[3]
    y = rms_norm(ffn(jnp.concatenate([hc, hx], axis=1)), n_post)
    return xc + mods_c[5] * y[:, :L], x + mods_x[5] * y[:, L:]


def setup_inputs(seed: int = 0) -> dict:
    key = jax.random.key(seed)
    ks = iter(jax.random.split(key, 40))
    D = D_MODEL

    def nrm(shape, scale):
        return jax.random.normal(next(ks), shape, jnp.float32) * scale

    def unif(shape, lo, hi):
        return jax.random.uniform(next(ks), shape, jnp.float32, lo, hi)

    p = 2.0 ** (-5.0 - jnp.arange(RET_HEADS, dtype=jnp.float32))
    ret_logit = jnp.log((1.0 - p) / p)
    ret_decay = jnp.stack([ret_logit, ret_logit[::-1]])[None] + nrm((DEPTH, 2, RET_HEADS), 0.05)
    dt0 = jnp.exp(unif((DEPTH, 2, SSM_HEADS), math.log(1e-3), math.log(1e-1)))
    dt_bias = dt0 + jnp.log(-jnp.expm1(-dt0))
    a_log = jnp.log(unif((DEPTH, 2, SSM_HEADS), 1.0, 16.0))
    return {
        'x': nrm((BATCH, SEQ, D), 1.0),
        'c': nrm((BATCH, D), 1.0),
        'ctx': nrm((BATCH, CTX_LEN, D), 1.0),
        'c_ctx': nrm((D,), 1.0),
        'ada_w': nrm((DEPTH, D, 6 * D), 0.5 * D ** -0.5),
        'ada_b': nrm((DEPTH, 6 * D), 0.02),
        'mix_norm_pre': 1.0 + nrm((DEPTH, D), 0.02),
        'mix_norm_post': 1.0 + nrm((DEPTH, D), 0.02),
        'ffn_norm_pre': 1.0 + nrm((DEPTH, D), 0.02),
        'ffn_norm_post': 1.0 + nrm((DEPTH, D), 0.02),
        'w_in': nrm((DEPTH, D, IN_W), D ** -0.5),
        'ret_decay': ret_decay,
        'ssm_conv_w': nrm((DEPTH, SSM_CONV, SSM_XBC_W), SSM_CONV ** -0.5),
        'ssm_conv_b': nrm((DEPTH, SSM_XBC_W), 0.02),
        'ssm_dt_bias': dt_bias,
        'ssm_a_log': a_log,
        'ssm_d': 1.0 + nrm((DEPTH, 2, SSM_HEADS), 0.1),
        'ssm_norm_w': 1.0 + nrm((DEPTH, SSM_INNER), 0.02),
        'att_sink': nrm((DEPTH, ATT_HEADS), 0.5),
        'w_branch': nrm((DEPTH, N_BRANCH, BRANCH_W, D), BRANCH_W ** -0.5),
        'w_out': nrm((DEPTH, D, D), D ** -0.5),
        'ffn_w1': nrm((N_DENSE, D, D_FF), D ** -0.5),
        'ffn_w3': nrm((N_DENSE, D, D_FF), D ** -0.5),
        'ffn_w2': nrm((N_DENSE, D_FF, D), D_FF ** -0.5),
        'moe_router': nrm((N_MOE, D, N_EXPERTS), D ** -0.5),
        'moe_w1': nrm((N_MOE, N_EXPERTS, D, D_FF), D ** -0.5),
        'moe_w3': nrm((N_MOE, N_EXPERTS, D, D_FF), D ** -0.5),
        'moe_w2': nrm((N_MOE, N_EXPERTS, D_FF, D), D_FF ** -0.5),
    }


def reference(x, c, ctx, c_ctx, ada_w, ada_b, mix_norm_pre, mix_norm_post, ffn_norm_pre, ffn_norm_post, w_in,
              ret_decay, ssm_conv_w, ssm_conv_b, ssm_dt_bias, ssm_a_log, ssm_d, ssm_norm_w, att_sink, w_branch,
              w_out, ffn_w1, ffn_w3, ffn_w2, moe_router, moe_w1, moe_w3, moe_w2):
    xc = ctx
    for i in range(DEPTH):
        need_ctx = i < DEPTH - 1
        m_x = adaln(c, ada_w[i], ada_b[i])
        m_c = adaln(c_ctx, ada_w[i], ada_b[i])
        mods_x = [m_x[:, None, k] for k in range(6)]
        mods_c = [m_c[k] for k in range(6)]
        xc, x = mixer_sublayer(xc, x, mods_c, mods_x, mix_norm_pre[i], mix_norm_post[i], w_in[i], ret_decay[i],
                               ssm_conv_w[i], ssm_conv_b[i], ssm_dt_bias[i], ssm_a_log[i], ssm_d[i], ssm_norm_w[i],
                               att_sink[i], w_branch[i], w_out[i], need_ctx)
        j = i // 2
        if i % 2 == 0:
            ffn = functools.partial(swiglu, w1=ffn_w1[j], w3=ffn_w3[j], w2=ffn_w2[j])
        else:
            ffn = functools.partial(moe_swiglu, w_router=moe_router[j], w1=moe_w1[j], w3=moe_w3[j], w2=moe_w2[j])
        xc, x = ffn_sublayer(xc, x, mods_c, mods_x, ffn_norm_pre[i], ffn_norm_post[i], ffn)
    return x
```

```python
import functools
import math

import jax
import jax.numpy as jnp
import numpy as np
from jax import lax
from jax.experimental import pallas as pl
from jax.experimental.pallas import tpu as pltpu

f32 = jnp.float32
bf16 = jnp.bfloat16

D_MODEL = 1024
GRID_W = 64
CHUNK = 128
NORM_EPS = 1e-6
ROPE_BASE = 10000.0
NEG_INF = -1e30
RET_HEADS, RET_DK, RET_DV = 4, 128, 256
SSM_HEADS, SSM_HEAD_DIM, SSM_GROUPS, SSM_STATE = 16, 64, 2, 128
SSM_HG = SSM_HEADS // SSM_GROUPS
SSM_GW = SSM_HG * SSM_HEAD_DIM
ATT_HEADS, ATT_KV_HEADS, ATT_HEAD_DIM = 8, 2, 128
ATT_G = ATT_HEADS // ATT_KV_HEADS
D_FF = 2816
N_EXPERTS = 8
IN_SPLITS = (512, 512, 1024, 1024, 1024, 1536, 16, 1024, 256, 256, 3072)

LANES = 128
SUBLANES = 8
SEQ_TILE = 256
MOE_BLOCK = 256
MOE_DISPATCH_ROWS = 128
MOE_COMBINE_ROWS = 256
VMEM_LIMIT = 56 * 2 ** 20


def _cp(sem, vmem=None):
    return pltpu.CompilerParams(dimension_semantics=sem, vmem_limit_bytes=vmem)


def _silu(v):
    return v / (1.0 + jnp.exp(-v))


def _sigmoid(v):
    return 1.0 / (1.0 + jnp.exp(-v))


def _rms(v, w):
    return v * lax.rsqrt(jnp.mean(v * v, axis=-1, keepdims=True) + NORM_EPS) * w


def _norm_mod(x, nw, m, shift_row, scale_row):
    return _rms(x, nw) * (1.0 + m[scale_row:scale_row + 1]) + m[shift_row:shift_row + 1]


def _dot(a, b):
    return jnp.dot(a, b, preferred_element_type=f32)


def _dot_nt(a, b):
    return lax.dot_general(a, b, (((1,), (1,)), ((), ())), preferred_element_type=f32)


def _split3(a):
    hi = a.astype(bf16)
    r = a - hi.astype(f32)
    mid = r.astype(bf16)
    lo = (r - mid.astype(f32)).astype(bf16)
    return hi, mid, lo


def _adaln_body(c_ref, w_ref, b_ref, o_ref):
    s = _silu(c_ref[...])
    o_ref[0] = _dot(s.astype(bf16), w_ref[0].astype(bf16)) + b_ref[0]


def adaln_all(cond, ada_w, ada_b):
    depth, d, n = ada_w.shape
    rows = cond.shape[0]
    tn = 1024
    return pl.pallas_call(
        _adaln_body,
        grid=(depth, n // tn),
        in_specs=[pl.BlockSpec((rows, d), lambda l, j: (0, 0)),
                  pl.BlockSpec((1, d, tn), lambda l, j: (l, 0, j)),
                  pl.BlockSpec((1, 1, tn), lambda l, j: (l, 0, j))],
        out_specs=pl.BlockSpec((1, rows, tn), lambda l, j: (l, 0, j)),
        out_shape=jax.ShapeDtypeStruct((depth, rows, n), f32),
        compiler_params=_cp(("arbitrary", "arbitrary")),
        name="adaln",
    )(cond, ada_w, ada_b.reshape(depth, 1, n))


def _proj_body(*refs, rope_shifts, tn):
    x_ref, nw_ref, mod_ref, w_ref = refs[:4]
    n_rope = len(rope_shifts)
    if n_rope:
        cs_ref, cos_ref = refs[4:6]
        sin_refs = refs[6:6 + n_rope]
    o_ref, h_scr = refs[-2:]

    @pl.when(pl.program_id(1) == 0)
    def _():
        h_scr[...] = _norm_mod(x_ref[...], nw_ref[...], mod_ref[0], 0, 1).astype(bf16)

    acc = _dot(h_scr[...], w_ref[...])
    if not n_rope:
        o_ref[...] = acc.astype(o_ref.dtype)
        return
    cos = cos_ref[...]
    sins = [r[...] for r in sin_refs]
    for c in range(tn // LANES):
        sl = slice(c * LANES, (c + 1) * LANES)
        t = acc[:, sl]
        o = t * cos
        for sh, s in zip(rope_shifts, sins):
            o = o + pltpu.roll(t, sh, 1) * s
        o_ref[:, sl] = (o * cs_ref[:, sl]).astype(o_ref.dtype)


def in_proj(x, nw, mods, w, geom, out_dtype, tn, rope=None):
    ntok, d = x.shape
    n = w.shape[1]
    tm = geom["tm"]
    nxt = geom["nx"] // tm
    per_b = geom["S"] // tm
    nb = geom["B"]

    def mod_idx(i, j):
        return (jnp.where(i < nxt, i // per_b, nb), 0, 0)

    def pos_idx(i, j):
        return (jnp.where(i < nxt, i % per_b, per_b), 0)

    in_specs = [pl.BlockSpec((tm, d), lambda i, j: (i, 0)),
                pl.BlockSpec((1, d), lambda i, j: (0, 0)),
                pl.BlockSpec((1, 6, d), mod_idx),
                pl.BlockSpec((d, tn), lambda i, j: (0, j))]
    args = [x, nw, mods, w]
    shifts = ()
    if rope is not None:
        shifts, colscale, cos, sins = rope
        in_specs.append(pl.BlockSpec((1, tn), lambda i, j: (0, j)))
        args.append(colscale)
        for t in (cos,) + tuple(sins):
            in_specs.append(pl.BlockSpec((tm, LANES), pos_idx))
            args.append(t)
    return pl.pallas_call(
        functools.partial(_proj_body, rope_shifts=tuple(shifts), tn=tn),
        grid=(ntok // tm, n // tn),
        in_specs=in_specs,
        out_specs=pl.BlockSpec((tm, tn), lambda i, j: (i, j)),
        out_shape=jax.ShapeDtypeStruct((ntok, n), out_dtype),
        scratch_shapes=[pltpu.VMEM((tm, d), bf16)],
        compiler_params=_cp(("arbitrary", "arbitrary"), VMEM_LIMIT),
        name="in_proj",
    )(*args)


def _conv_body(x_ref, prev_ref, next_ref, w_ref, b_ref, o_ref, *, tiles_per_seq, nx_tiles):
    i = pl.program_id(0)
    x = x_ref[...]
    rows = x.shape[0]
    t_in = i % tiles_per_seq
    is_x = i < nx_tiles
    use_prev = jnp.logical_and(is_x, t_in > 0)
    use_next = jnp.logical_and(is_x, t_in < tiles_per_seq - 1)
    prow = jnp.where(use_prev, prev_ref[SUBLANES - 1:SUBLANES, :], 0.0)
    nrow = jnp.where(use_next, next_ref[0:1, :], 0.0)
    rid = lax.broadcasted_iota(jnp.int32, x.shape, 0)
    xm = jnp.where(rid == 0, prow, pltpu.roll(x, 1, 0))
    xp = jnp.where(rid == rows - 1, nrow, pltpu.roll(x, rows - 1, 0))
    w = w_ref[...]
    o_ref[...] = _silu(xm * w[0:1] + x * w[1:2] + xp * w[2:3] + b_ref[...])


def ssd_conv(d1, conv_w, conv_b, geom, col0):
    ntok = d1.shape[0]
    width = conv_w.shape[1]
    tc = 512
    cb0 = col0 // tc
    tr = SEQ_TILE
    per8 = tr // SUBLANES
    last8 = ntok // SUBLANES - 1
    return pl.pallas_call(
        functools.partial(_conv_body, tiles_per_seq=geom["S"] // tr, nx_tiles=geom["nx"] // tr),
        grid=(ntok // tr, width // tc),
        in_specs=[pl.BlockSpec((tr, tc), lambda i, j: (i, cb0 + j)),
                  pl.BlockSpec((SUBLANES, tc), lambda i, j: (jnp.maximum(i * per8 - 1, 0), cb0 + j)),
                  pl.BlockSpec((SUBLANES, tc), lambda i, j: (jnp.minimum((i + 1) * per8, last8), cb0 + j)),
                  pl.BlockSpec((3, tc), lambda i, j: (0, j)),
                  pl.BlockSpec((1, tc), lambda i, j: (0, j))],
        out_specs=pl.BlockSpec((tr, tc), lambda i, j: (i, j)),
        out_shape=jax.ShapeDtypeStruct((ntok, width), f32),
        compiler_params=_cp(("arbitrary", "arbitrary")),
        name="ssd_conv",
    )(d1, d1, d1, conv_w, conv_b.reshape(1, width))


def _seq_tile_index(geom, bwd):
    per_b = geom["S"] // SEQ_TILE
    ctx0 = geom["nx"] // SEQ_TILE

    def idx(b, s):
        xt = (per_b - s) if bwd else (s - 1)
        return jnp.where(s == 0, ctx0 + b, b * per_b + xt)

    return idx, per_b + 1


def _expand_heads(a, lane):
    rows = a.shape[0]
    lane = lane[:rows]
    cols = []
    for v in range(SSM_HG // 2):
        left = jnp.broadcast_to(a[:, 2 * v:2 * v + 1], (rows, LANES))
        right = jnp.broadcast_to(a[:, 2 * v + 1:2 * v + 2], (rows, LANES))
        cols.append(jnp.where(lane < SSM_HEAD_DIM, left, right))
    return jnp.concatenate(cols, axis=1)


def _ssd_body(*refs, bwd):
    if bwd:
        xs_ref, b_ref, c_ref, dt_ref, par_ref, z_ref, skip_ref, nw_ref, prev_ref, o_ref, s_scr, y_scr = refs
    else:
        xs_ref, b_ref, c_ref, dt_ref, par_ref, o_ref, s_scr, y_scr = refs

    @pl.when(pl.program_id(2) == 0)
    def _():
        s_scr[...] = jnp.zeros_like(s_scr)

    par = par_ref[0]
    ii = lax.broadcasted_iota(jnp.int32, (CHUNK, CHUNK), 0)
    jj = lax.broadcasted_iota(jnp.int32, (CHUNK, CHUNK), 1)
    tri_mask = (ii <= jj) if bwd else (ii >= jj)
    tri = jnp.where(tri_mask, 1.0, 0.0).astype(bf16)
    lane = jj

    chunks = range(SEQ_TILE // CHUNK)
    for ci in (reversed(chunks) if bwd else chunks):
        sl = slice(ci * CHUNK, (ci + 1) * CHUNK)
        pre = dt_ref[sl, :] + par[0:1]
        dt = jnp.maximum(pre, 0.0) + jnp.log(1.0 + jnp.exp(-jnp.abs(pre)))
        dta = dt * par[1:2]
        hi, mid, lo = _split3(dta)
        cum = _dot(tri, hi) + _dot(tri, mid) + _dot(tri, lo)
        tot = cum[0:1] if bwd else cum[CHUNK - 1:CHUNK]
        cum_t = cum.T
        dt_x = _expand_heads(dt, lane)
        ec_x = _expand_heads(jnp.exp(cum), lane)
        ws_x = _expand_heads(jnp.exp(tot - cum), lane)
        dec_x = _expand_heads(jnp.broadcast_to(jnp.exp(tot), (SUBLANES, LANES)), lane)[0:1]

        xs = xs_ref[sl, :]
        bm = b_ref[sl, :]
        cm = c_ref[sl, :].astype(bf16)
        xdt = xs * dt_x
        xdt_b = xdt.astype(bf16)
        cb = _dot_nt(cm, bm.astype(bf16))
        state = s_scr[...]
        for h in range(SSM_HG):
            seg = cum[:, h:h + 1] - cum_t[h:h + 1, :]
            lmat = jnp.exp(jnp.where(tri_mask, seg, NEG_INF))
            hs = slice(h * SSM_HEAD_DIM, (h + 1) * SSM_HEAD_DIM)
            y_scr[:, hs] = _dot((cb * lmat).astype(bf16), xdt_b[:, hs])
        y = y_scr[...] + _dot(cm, state.astype(bf16)) * ec_x
        s_scr[...] = dec_x * state + _dot(bm.T.astype(bf16), (xdt * ws_x).astype(bf16))
        if bwd:
            y = y + prev_ref[sl, :] + xs * skip_ref[...]
            gz = y * _silu(z_ref[sl, :])
            o_ref[sl, :] = _rms(gz, nw_ref[...]).astype(o_ref.dtype)
        else:
            o_ref[sl, :] = y


def ssd_scan(u, d1, par, geom, dt_cb0, bwd, z_cb0=None, skip=None, norm_w=None, prev=None):
    ntok = u.shape[0]
    idx, steps = _seq_tile_index(geom, bwd)
    gw = SSM_GW
    nb_blk = SSM_HEADS * SSM_HEAD_DIM // SSM_STATE
    tile = lambda w, cb: pl.BlockSpec((SEQ_TILE, w), lambda b, g, s, cb=cb: (idx(b, s), cb(g)))
    in_specs = [tile(gw, lambda g: g),
                tile(SSM_STATE, lambda g: nb_blk + g),
                tile(SSM_STATE, lambda g: nb_blk + SSM_GROUPS + g),
                tile(LANES, lambda g: dt_cb0 + g),
                pl.BlockSpec((1, 2, LANES), lambda b, g, s: (g, 0, 0))]
    args = [u, u, u, d1, par]
    if bwd:
        in_specs += [tile(gw, lambda g: z_cb0 + g),
                     pl.BlockSpec((1, gw), lambda b, g, s: (0, g)),
                     pl.BlockSpec((1, gw), lambda b, g, s: (0, g)),
                     tile(gw, lambda g: g)]
        args += [d1, skip, norm_w, prev]
    return pl.pallas_call(
        functools.partial(_ssd_body, bwd=bwd),
        grid=(geom["B"], SSM_GROUPS, steps),
        in_specs=in_specs,
        out_specs=tile(gw, lambda g: g),
        out_shape=jax.ShapeDtypeStruct((ntok, SSM_GROUPS * gw), bf16 if bwd else f32),
        scratch_shapes=[pltpu.VMEM((SSM_STATE, gw), f32), pltpu.VMEM((CHUNK, gw), f32)],
        compiler_params=_cp(("arbitrary", "arbitrary", "arbitrary")),
        name="ssd_bwd" if bwd else "ssd_fwd",
    )(*args)


def _ret_body(*refs, bwd):
    if bwd:
        lg_ref, q_ref, k_ref, v_ref, g_ref, prev_ref, o_ref, s_scr = refs
    else:
        lg_ref, q_ref, k_ref, v_ref, o_ref, s_scr = refs

    @pl.when(pl.program_id(2) == 0)
    def _():
        s_scr[...] = jnp.zeros_like(s_scr)

    lg = lg_ref[1 if bwd else 0, pl.program_id(1)]
    ii = lax.broadcasted_iota(jnp.int32, (CHUNK, CHUNK), 0).astype(f32)
    jj = lax.broadcasted_iota(jnp.int32, (CHUNK, CHUNK), 1).astype(f32)
    diff = (jj - ii) if bwd else (ii - jj)
    dmat = jnp.where(diff >= 0, jnp.exp(jnp.maximum(diff, 0.0) * lg), 0.0)
    icol = ii[:, 0:1]
    jrow = jj[0:1, :]
    if bwd:
        wq = jnp.exp((CHUNK - icol) * lg)
        ws = jnp.exp(jrow * lg)
    else:
        wq = jnp.exp((icol + 1.0) * lg)
        ws = jnp.exp((CHUNK - 1.0 - jrow) * lg)
    decay = jnp.exp(jnp.full((1, 1), float(CHUNK), f32) * lg)

    chunks = range(SEQ_TILE // CHUNK)
    for ci in (reversed(chunks) if bwd else chunks):
        sl = slice(ci * CHUNK, (ci + 1) * CHUNK)
        q = q_ref[sl, :]
        k = k_ref[sl, :]
        v = v_ref[sl, :]
        state = s_scr[...]
        sc = _dot_nt(q, k) * dmat
        o = _dot(sc.astype(bf16), v) + _dot(q, state.astype(bf16)) * wq
        kt = (k.astype(f32).T * ws).astype(bf16)
        s_scr[...] = decay * state + _dot(kt, v)
        if bwd:
            o = o + prev_ref[sl, :]
            o = o * lax.rsqrt(jnp.mean(o * o, axis=-1, keepdims=True) + NORM_EPS)
            o_ref[sl, :] = (_silu(g_ref[sl, :]) * o).astype(o_ref.dtype)
        else:
            o_ref[sl, :] = o


def ret_scan(lg, qk, vb, geom, bwd, d1=None, prev=None):
    ntok = qk.shape[0]
    idx, steps = _seq_tile_index(geom, bwd)
    tile = lambda w, cb: pl.BlockSpec((SEQ_TILE, w), lambda b, h, s, cb=cb: (idx(b, s), cb(h)))
    in_specs = [pl.BlockSpec(memory_space=pltpu.SMEM),
                tile(RET_DK, lambda h: h),
                tile(RET_DK, lambda h: RET_HEADS + h),
                tile(RET_DV, lambda h: h)]
    args = [lg, qk, qk, vb]
    if bwd:
        in_specs += [tile(RET_DV, lambda h: h), tile(RET_DV, lambda h: h)]
        args += [d1, prev]
    return pl.pallas_call(
        functools.partial(_ret_body, bwd=bwd),
        grid=(geom["B"], RET_HEADS, steps),
        in_specs=in_specs,
        out_specs=tile(RET_DV, lambda h: h),
        out_shape=jax.ShapeDtypeStruct((ntok, RET_HEADS * RET_DV), bf16 if bwd else f32),
        scratch_shapes=[pltpu.VMEM((RET_DK, RET_DV), f32)],
        compiler_params=_cp(("arbitrary", "arbitrary", "arbitrary")),
        name="ret_bwd" if bwd else "ret_fwd",
    )(*args)


def _att_body(sink_ref, q_ref, kc_ref, vc_ref, kp_ref, ko_ref, kn_ref, vp_ref, vo_ref, vn_ref, o_ref, *, nblk, ctx_blk):
    n = pl.program_id(1) - ctx_blk
    own_ok = n >= 0
    prev_ok = n >= 1
    next_ok = jnp.logical_and(n >= 0, n <= nblk - 2)
    qi = lax.broadcasted_iota(jnp.int32, (CHUNK, CHUNK), 0)
    kj = lax.broadcasted_iota(jnp.int32, (CHUNK, CHUNK), 1)
    m_prev = jnp.logical_and(kj >= qi, prev_ok)
    m_next = jnp.logical_and(kj <= qi, next_ok)
    hd = ATT_HEAD_DIM
    for kv in range(ATT_KV_HEADS):
        ks = slice(kv * hd, (kv + 1) * hd)
        kc, vc = kc_ref[:, ks], vc_ref[:, ks]
        kp, ko, kn = kp_ref[:, ks], ko_ref[:, ks], kn_ref[:, ks]
        vp, vo, vn = vp_ref[:, ks], vo_ref[:, ks], vn_ref[:, ks]
        for g in range(ATT_G):
            hq = kv * ATT_G + g
            qs = slice(hq * hd, (hq + 1) * hd)
            q = q_ref[:, qs]
            s_c = _dot_nt(q, kc)
            s_p = jnp.where(m_prev, _dot_nt(q, kp), NEG_INF)
            s_o = jnp.where(own_ok, _dot_nt(q, ko), NEG_INF)
            s_n = jnp.where(m_next, _dot_nt(q, kn), NEG_INF)
            sk = sink_ref[hq]
            mx = jnp.maximum(jnp.maximum(jnp.max(s_c, axis=-1, keepdims=True), jnp.max(s_p, axis=-1, keepdims=True)),
                             jnp.maximum(jnp.max(s_o, axis=-1, keepdims=True), jnp.max(s_n, axis=-1, keepdims=True)))
            mx = jnp.maximum(mx, sk)
            p_c, p_p, p_o, p_n = (jnp.exp(t - mx) for t in (s_c, s_p, s_o, s_n))
            den = (jnp.exp(sk - mx) + jnp.sum(p_c, axis=-1, keepdims=True) + jnp.sum(p_p, axis=-1, keepdims=True)
                   + jnp.sum(p_o, axis=-1, keepdims=True) + jnp.sum(p_n, axis=-1, keepdims=True))
            o = (_dot(p_c.astype(bf16), vc) + _dot(p_p.astype(bf16), vp)
                 + _dot(p_o.astype(bf16), vo) + _dot(p_n.astype(bf16), vn))
            o_ref[:, qs] = (o / den).astype(o_ref.dtype)


def window_attention(sink, qk, vb, geom):
    ntok = qk.shape[0]
    blk = CHUNK
    nblk = geom["S"] // blk
    ctx_blk = geom["L"] // blk
    ctx0 = geom["nx"] // blk
    kvw = ATT_KV_HEADS * ATT_HEAD_DIM
    qw = ATT_HEADS * ATT_HEAD_DIM
    kcol = qw // kvw

    def q_idx(b, i):
        return jnp.where(i < ctx_blk, ctx0 + b * ctx_blk + i, b * nblk + i - ctx_blk)

    def win(off):
        def f(b, i):
            n = jnp.clip(i - ctx_blk + off, 0, nblk - 1)
            return (b * nblk + n, kcol)
        return f

    ctx_spec = pl.BlockSpec((geom["L"], kvw), lambda b, i: (geom["nx"] // geom["L"] + b, kcol))
    wspec = lambda off: pl.BlockSpec((blk, kvw), win(off))
    return pl.pallas_call(
        functools.partial(_att_body, nblk=nblk, ctx_blk=ctx_blk),
        grid=(geom["B"], ctx_blk + nblk),
        in_specs=[pl.BlockSpec(memory_space=pltpu.SMEM),
                  pl.BlockSpec((blk, qw), lambda b, i: (q_idx(b, i), 0)),
                  ctx_spec, ctx_spec,
                  wspec(-1), wspec(0), wspec(1), wspec(-1), wspec(0), wspec(1)],
        out_specs=pl.BlockSpec((blk, qw), lambda b, i: (q_idx(b, i), 0)),
        out_shape=jax.ShapeDtypeStruct((ntok, qw), bf16),
        compiler_params=_cp(("arbitrary", "arbitrary")),
        name="window_attention",
    )(sink, qk, qk, vb, qk, qk, qk, vb, vb, vb)


def _merge_body(r_ref, s_ref, a_ref, gate_ref, x_ref, wb_ref, wo_ref, nw_ref, mod_ref, o_ref):
    acc = None
    for n, ref in enumerate((r_ref, s_ref, a_ref)):
        t = _sigmoid(gate_ref[:, n * D_MODEL:(n + 1) * D_MODEL]) * _dot(ref[...], wb_ref[n])
        acc = t if acc is None else acc + t
    y = _dot(acc.astype(bf16), wo_ref[...])
    o_ref[...] = x_ref[...] + mod_ref[0][2:3] * _rms(y, nw_ref[...])


def _row_tile_mod_idx(geom, tm):
    nxt = geom["nx"] // tm
    per_b = geom["S"] // tm
    return lambda i: (jnp.where(i < nxt, i // per_b, geom["B"]), 0, 0)


def merge(ret_o, ssd_o, att_o, gate, x, wb, wo, nw, mods, geom, rows):
    d = D_MODEL
    tm = geom["tm_merge"]
    row = lambda w: pl.BlockSpec((tm, w), lambda i: (i, 0))
    return pl.pallas_call(
        _merge_body,
        grid=(rows // tm,),
        in_specs=[row(d), row(d), row(d), row(3 * d), row(d),
                  pl.BlockSpec((3, d, d), lambda i: (0, 0, 0)),
                  pl.BlockSpec((d, d), lambda i: (0, 0)),
                  pl.BlockSpec((1, d), lambda i: (0, 0)),
                  pl.BlockSpec((1, 6, d), _row_tile_mod_idx(geom, tm))],
        out_specs=row(d),
        out_shape=jax.ShapeDtypeStruct((rows, d), f32),
        compiler_params=_cp(("arbitrary",), VMEM_LIMIT),
        name="merge",
    )(ret_o, ssd_o, att_o, gate, x, wb, wo, nw, mods)


def _ffn_body(x_ref, npre_ref, npost_ref, mod_ref, w1_ref, w3_ref, w2_ref, o_ref, h_scr, acc_scr):
    j = pl.program_id(1)

    @pl.when(j == 0)
    def _():
        h_scr[...] = _norm_mod(x_ref[...], npre_ref[...], mod_ref[0], 3, 4).astype(bf16)
        acc_scr[...] = jnp.zeros_like(acc_scr)

    h = h_scr[...]
    act = _silu(_dot(h, w1_ref[...])) * _dot(h, w3_ref[...])
    acc_scr[...] += _dot(act.astype(bf16), w2_ref[...])

    @pl.when(j == pl.num_programs(1) - 1)
    def _():
        o_ref[...] = x_ref[...] + mod_ref[0][5:6] * _rms(acc_scr[...], npost_ref[...])


def dense_ffn(x, npre, npost, mods, w1, w3, w2, geom, rows):
    d = D_MODEL
    ff = w1.shape[1]
    tm = geom["tm"]
    tf = 256
    return pl.pallas_call(
        _ffn_body,
        grid=(rows // tm, ff // tf),
        in_specs=[pl.BlockSpec((tm, d), lambda i, j: (i, 0)),
                  pl.BlockSpec((1, d), lambda i, j: (0, 0)),
                  pl.BlockSpec((1, d), lambda i, j: (0, 0)),
                  pl.BlockSpec((1, 6, d), lambda i, j: _row_tile_mod_idx(geom, tm)(i)),
                  pl.BlockSpec((d, tf), lambda i, j: (0, j)),
                  pl.BlockSpec((d, tf), lambda i, j: (0, j)),
                  pl.BlockSpec((tf, d), lambda i, j: (j, 0))],
        out_specs=pl.BlockSpec((tm, d), lambda i, j: (i, 0)),
        out_shape=jax.ShapeDtypeStruct((rows, d), f32),
        scratch_shapes=[pltpu.VMEM((tm, d), bf16), pltpu.VMEM((tm, d), f32)],
        compiler_params=_cp(("arbitrary", "arbitrary"), VMEM_LIMIT),
        name="dense_ffn",
    )(x, npre, npost, mods, w1, w3, w2)


def _router_body(x_ref, npre_ref, mod_ref, wr_ref, h_ref, route_ref):
    h = _norm_mod(x_ref[...], npre_ref[...], mod_ref[0], 3, 4)
    h_ref[...] = h
    h_hi = h.astype(bf16)
    h_lo = (h - h_hi.astype(f32)).astype(bf16)
    wr = wr_ref[...]
    w_hi = wr.astype(bf16)
    w_lo = (wr - w_hi.astype(f32)).astype(bf16)
    logits = _dot(h_hi, w_hi) + _dot(h_hi, w_lo) + _dot(h_lo, w_hi) + _dot(h_lo, w_lo)
    lane = lax.broadcasted_iota(jnp.int32, logits.shape, 1).astype(f32)
    l1 = jnp.where(lane < N_EXPERTS, logits, -jnp.inf)
    m1 = jnp.max(l1, axis=-1, keepdims=True)
    i1 = jnp.min(jnp.where(l1 == m1, lane, float(LANES)), axis=-1, keepdims=True)
    l2 = jnp.where(lane == i1, -jnp.inf, l1)
    m2 = jnp.max(l2, axis=-1, keepdims=True)
    i2 = jnp.min(jnp.where(l2 == m2, lane, float(LANES)), axis=-1, keepdims=True)
    e = jnp.exp(m2 - m1)
    w1 = 1.0 / (1.0 + e)
    w2 = e / (1.0 + e)
    route_ref[...] = jnp.where(lane == 0.0, i1, jnp.where(lane == 1.0, i2,
                                                          jnp.where(lane == 2.0, w1, jnp.where(lane == 3.0, w2, 0.0))))


def moe_router(x, npre, mods, w_router, geom, rows):
    d = D_MODEL
    tm = geom["tm_merge"]
    wr = jnp.pad(w_router, ((0, 0), (0, LANES - w_router.shape[1])))
    return pl.pallas_call(
        _router_body,
        grid=(rows // tm,),
        in_specs=[pl.BlockSpec((tm, d), lambda i: (i, 0)),
                  pl.BlockSpec((1, d), lambda i: (0, 0)),
                  pl.BlockSpec((1, 6, d), _row_tile_mod_idx(geom, tm)),
                  pl.BlockSpec((d, LANES), lambda i: (0, 0))],
        out_specs=[pl.BlockSpec((tm, d), lambda i: (i, 0)), pl.BlockSpec((tm, LANES), lambda i: (i, 0))],
        out_shape=[jax.ShapeDtypeStruct((rows, d), f32), jax.ShapeDtypeStruct((rows, LANES), f32)],
        compiler_params=_cp(("arbitrary",)),
        name="moe_router",
    )(x, npre, mods, wr)


def _row_copy(src, src_row, dst, dst_row, sem):
    return pltpu.make_async_copy(src.at[pl.ds(src_row, 1)], dst.at[pl.ds(dst_row, 1)], sem)


def _dispatch_body(slot_ref, h_hbm, xb_in, xb_hbm, sem, *, rows):
    del xb_in
    base = pl.program_id(0) * rows

    def start(a, c):
        _row_copy(h_hbm, base + a // 2, xb_hbm, slot_ref[0, 0, a], sem).start()
        return c

    def wait(a, c):
        _row_copy(h_hbm, 0, xb_hbm, 0, sem).wait()
        return c

    lax.fori_loop(0, 2 * rows, start, 0)
    lax.fori_loop(0, 2 * rows, wait, 0)


def moe_dispatch(h, slots, n_slots):
    n_tok, d = h.shape
    rows = MOE_DISPATCH_ROWS
    steps = n_tok // rows
    return pl.pallas_call(
        functools.partial(_dispatch_body, rows=rows),
        grid=(steps,),
        in_specs=[pl.BlockSpec((1, 1, 2 * rows), lambda i: (i, 0, 0), memory_space=pltpu.SMEM),
                  pl.BlockSpec(memory_space=pl.ANY),
                  pl.BlockSpec(memory_space=pl.ANY)],
        out_specs=pl.BlockSpec(memory_space=pl.ANY),
        out_shape=jax.ShapeDtypeStruct((n_slots, d), f32),
        scratch_shapes=[pltpu.SemaphoreType.DMA(())],
        input_output_aliases={2: 0},
        compiler_params=_cp(("arbitrary",)),
        name="moe_dispatch",
    )(slots.reshape(steps, 1, 2 * rows), h, jnp.zeros((n_slots, d), f32))


def _expert_body(be_ref, cnt_ref, used_ref, x_ref, w1_ref, w3_ref, w2_ref, o_ref, *, f_split):
    del be_ref, used_ref
    i = pl.program_id(0)

    @pl.when(cnt_ref[i] > 0)
    def _():
        x = x_ref[...].astype(bf16)
        ff = w1_ref.shape[2]
        step = ff // f_split
        acc = None
        for f in range(f_split):
            fs = slice(f * step, (f + 1) * step)
            act = _silu(_dot(x, w1_ref[0, :, fs])) * _dot(x, w3_ref[0, :, fs])
            t = _dot(act.astype(bf16), w2_ref[0, fs, :])
            acc = t if acc is None else acc + t
        o_ref[...] = acc

    @pl.when(cnt_ref[i] == 0)
    def _():
        o_ref[...] = jnp.zeros_like(o_ref)


def moe_experts(xb, blk_e, blk_cnt, n_used, w1, w3, w2):
    n_slots, d = xb.shape
    ff = w1.shape[2]
    tm = MOE_BLOCK
    grid_spec = pltpu.PrefetchScalarGridSpec(
        num_scalar_prefetch=3,
        grid=(n_slots // tm,),
        in_specs=[pl.BlockSpec((tm, d), lambda i, be, cnt, used: (jnp.minimum(i, used[0] - 1), 0)),
                  pl.BlockSpec((1, d, ff), lambda i, be, cnt, used: (be[i], 0, 0)),
                  pl.BlockSpec((1, d, ff), lambda i, be, cnt, used: (be[i], 0, 0)),
                  pl.BlockSpec((1, ff, d), lambda i, be, cnt, used: (be[i], 0, 0))],
        out_specs=pl.BlockSpec((tm, d), lambda i, be, cnt, used: (i, 0)),
    )
    return pl.pallas_call(
        functools.partial(_expert_body, f_split=2),
        grid_spec=grid_spec,
        out_shape=jax.ShapeDtypeStruct((n_slots, d), f32),
        compiler_params=_cp(("arbitrary",), VMEM_LIMIT),
        name="moe_experts",
    )(blk_e, blk_cnt, n_used, xb, w1, w3, w2)


def _combine_body(slot_ref, route_ref, x_ref, npost_ref, mod_ref, yb_hbm, o_ref, y0_scr, y1_scr, sem, *, rows):
    def start(r, c):
        _row_copy(yb_hbm, slot_ref[0, 0, 2 * r], y0_scr, r, sem).start()
        _row_copy(yb_hbm, slot_ref[0, 0, 2 * r + 1], y1_scr, r, sem).start()
        return c

    def wait(r, c):
        _row_copy(yb_hbm, 0, y0_scr, 0, sem).wait()
        return c

    lax.fori_loop(0, rows, start, 0)
    lax.fori_loop(0, 2 * rows, wait, 0)
    y = y0_scr[...] * route_ref[:, 2:3] + y1_scr[...] * route_ref[:, 3:4]
    o_ref[...] = x_ref[...] + mod_ref[0][5:6] * _rms(y, npost_ref[...])


def moe_combine(slots, route, x, npost, mods, yb, geom, rows_total):
    d = D_MODEL
    rows = MOE_COMBINE_ROWS
    steps = rows_total // rows
    return pl.pallas_call(
        functools.partial(_combine_body, rows=rows),
        grid=(steps,),
        in_specs=[pl.BlockSpec((1, 1, 2 * rows), lambda i: (i, 0, 0), memory_space=pltpu.SMEM),
                  pl.BlockSpec((rows, LANES), lambda i: (i, 0)),
                  pl.BlockSpec((rows, d), lambda i: (i, 0)),
                  pl.BlockSpec((1, d), lambda i: (0, 0)),
                  pl.BlockSpec((1, 6, d), _row_tile_mod_idx(geom, rows)),
                  pl.BlockSpec(memory_space=pl.ANY)],
        out_specs=pl.BlockSpec((rows, d), lambda i: (i, 0)),
        out_shape=jax.ShapeDtypeStruct((rows_total, d), f32),
        scratch_shapes=[pltpu.VMEM((rows, d), f32), pltpu.VMEM((rows, d), f32), pltpu.SemaphoreType.DMA(())],
        compiler_params=_cp(("arbitrary",)),
        name="moe_combine",
    )(slots.reshape(steps, 1, 2 * rows), route, x, npost, mods, yb)


def _moe_slots(route, n_tok):
    tm = MOE_BLOCK
    flat_e = route[:, 0:2].astype(jnp.int32).reshape(2 * n_tok)
    onehot = (flat_e[:, None] == jnp.arange(N_EXPERTS, dtype=jnp.int32)[None, :]).astype(jnp.int32)
    csum = jnp.cumsum(onehot, axis=0)
    counts = csum[-1]
    rank = jnp.sum(csum * onehot, axis=1) - 1
    padded = (counts + tm - 1) // tm * tm
    pend = jnp.cumsum(padded)
    pstart = pend - padded
    slots = jnp.sum(onehot * pstart[None, :], axis=1) + rank
    n_blocks = 2 * n_tok // tm + N_EXPERTS
    blk = jnp.arange(n_blocks, dtype=jnp.int32)
    blk_e = jnp.minimum(jnp.searchsorted(pend // tm, blk, side="right"), N_EXPERTS - 1).astype(jnp.int32)
    n_used = (pend[-1] // tm).astype(jnp.int32)
    cnt = jnp.clip(counts[blk_e] - (blk - pstart[blk_e] // tm) * tm, 0, tm)
    blk_cnt = jnp.where(blk < n_used, cnt, 0).astype(jnp.int32)
    return slots.astype(jnp.int32), blk_e, blk_cnt, n_used.reshape(1), n_blocks * tm


def moe_ffn(x, npre, npost, mods, w_router, w1, w3, w2, geom, rows):
    h, route = moe_router(x, npre, mods, w_router, geom, rows)
    slots, blk_e, blk_cnt, n_used, n_slots = _moe_slots(route, rows)
    xb = moe_dispatch(h, slots, n_slots)
    yb = moe_experts(xb, blk_e, blk_cnt, n_used, w1, w3, w2)
    return moe_combine(slots, route, x, npost, mods, yb, geom, rows)


def _rope_tables(geom):
    s, tm = geom["S"], geom["tm"]
    pos = jnp.arange(s)

    def angles(p, dim):
        inv = ROPE_BASE ** (-jnp.arange(0, dim, 2, dtype=f32) / dim)
        return p.astype(f32)[:, None] * inv[None, :]

    def with_identity(t, one):
        return jnp.concatenate([t, jnp.full((tm, LANES), one, f32)], axis=0)

    a = angles(pos, RET_DK)
    ret_cos = jnp.concatenate([jnp.cos(a), jnp.cos(a)], axis=1)
    ret_sin = jnp.concatenate([-jnp.sin(a), jnp.sin(a)], axis=1)
    half = ATT_HEAD_DIM // 2
    ar = angles(pos // GRID_W, half)
    ac = angles(pos % GRID_W, half)
    z = jnp.zeros_like(ar)
    att_cos = jnp.concatenate([jnp.cos(ar), jnp.cos(ar), jnp.cos(ac), jnp.cos(ac)], axis=1)
    att_s_up = jnp.concatenate([-jnp.sin(ar), z, -jnp.sin(ac), z], axis=1)
    att_s_dn = jnp.concatenate([z, jnp.sin(ar), z, jnp.sin(ac)], axis=1)
    ret = ((RET_DK // 2,), with_identity(ret_cos, 1.0), (with_identity(ret_sin, 0.0),))
    att = ((LANES - half // 2, half // 2), with_identity(att_cos, 1.0),
           (with_identity(att_s_up, 0.0), with_identity(att_s_dn, 0.0)))
    return ret, att


def _pack_w_in(w):
    rq, rk, rv, rg, sz, sxbc, sdt, aq, ak, av, bg = jnp.split(w, np.cumsum(IN_SPLITS)[:-1].tolist(), axis=1)
    pad = lambda t, n: jnp.pad(t, ((0, 0), (0, n - t.shape[1])))
    w_a = jnp.concatenate([rq, rk], axis=1)
    w_b = jnp.concatenate([rv, av], axis=1)
    w_c = jnp.concatenate([aq, ak], axis=1)
    w_d1 = jnp.concatenate([rg, sz, sxbc, pad(sdt[:, :SSM_HG], LANES), pad(sdt[:, SSM_HG:], 3 * LANES)], axis=1)
    return tuple(t.astype(bf16) for t in (w_a, w_b, w_c, w_d1, bg))


def _pad_heads(v):
    return jnp.pad(v.astype(f32).reshape(2, SSM_GROUPS, SSM_HG), ((0, 0), (0, 0), (0, LANES - SSM_HG)))


def _geometry(batch, seq, ctx_len):
    tm = 1024
    while seq % tm or (batch * ctx_len) % tm:
        tm //= 2
    tm_merge = min(tm, 512)
    assert ctx_len == SEQ_TILE and seq % SEQ_TILE == 0 and seq % GRID_W == 0
    return {"B": batch, "S": seq, "L": ctx_len, "nx": batch * seq, "tm": tm, "tm_merge": tm_merge}


def kernel(x, c, ctx, c_ctx, ada_w, ada_b, mix_norm_pre, mix_norm_post, ffn_norm_pre, ffn_norm_post, w_in, ret_decay, ssm_conv_w, ssm_conv_b, ssm_dt_bias, ssm_a_log, ssm_d, ssm_norm_w, att_sink, w_branch, w_out, ffn_w1, ffn_w3, ffn_w2, moe_router, moe_w1, moe_w3, moe_w2):
    batch, seq, d = x.shape
    ctx_len = ctx.shape[1]
    depth = ada_w.shape[0]
    geom = _geometry(batch, seq, ctx_len)
    nx = geom["nx"]
    ntok = nx + batch * ctx_len

    stream = jnp.concatenate([x.reshape(nx, d), ctx.reshape(batch * ctx_len, d)], axis=0)
    cond = jnp.concatenate([c, c_ctx[None, :], jnp.zeros((SUBLANES - 1 - batch % SUBLANES, d), f32)], axis=0)
    mods_all = adaln_all(cond, ada_w, ada_b).reshape(depth, cond.shape[0], 6, d)
    rope_ret, rope_att = _rope_tables(geom)
    ret_cs = jnp.concatenate([jnp.ones((1, RET_HEADS * RET_DK), f32),
                              jnp.full((1, RET_HEADS * RET_DK), RET_DK ** -0.5, f32)], axis=1)
    att_cs = jnp.concatenate([jnp.full((1, ATT_HEADS * ATT_HEAD_DIM), ATT_HEAD_DIM ** -0.5, f32),
                              jnp.ones((1, ATT_KV_HEADS * ATT_HEAD_DIM), f32)], axis=1)
    row = lambda v: v.reshape(1, -1)
    dt_cb0 = (2 * D_MODEL + IN_SPLITS[5]) // LANES
    z_cb0 = D_MODEL // SSM_GW
    conv_col0 = 2 * D_MODEL

    for i in range(depth):
        last = i == depth - 1
        rows = nx if last else ntok
        mods = mods_all[i]
        w_a, w_b, w_c, w_d1, w_d2 = _pack_w_in(w_in[i])
        npre = row(mix_norm_pre[i])
        proj = functools.partial(in_proj, stream, npre, mods, geom=geom)
        ret_qk = proj(w=w_a, out_dtype=bf16, tn=512, rope=(rope_ret[0], ret_cs, rope_ret[1], rope_ret[2]))
        val = proj(w=w_b, out_dtype=bf16, tn=640)
        att_qk = proj(w=w_c, out_dtype=bf16, tn=640, rope=(rope_att[0], att_cs, rope_att[1], rope_att[2]))
        d1 = proj(w=w_d1, out_dtype=f32, tn=512)
        gate = proj(w=w_d2, out_dtype=f32, tn=512)

        lg = jax.nn.log_sigmoid(ret_decay[i].astype(f32))
        ret_f = ret_scan(lg, ret_qk, val, geom, bwd=False)
        ret_o = ret_scan(lg, ret_qk, val, geom, bwd=True, d1=d1, prev=ret_f)

        u = ssd_conv(d1, ssm_conv_w[i], ssm_conv_b[i], geom, conv_col0)
        par = jnp.stack([_pad_heads(ssm_dt_bias[i]), _pad_heads(-jnp.exp(ssm_a_log[i].astype(f32)))], axis=2)
        skip = row(jnp.repeat(ssm_d[i, 0].astype(f32) + ssm_d[i, 1].astype(f32), SSM_HEAD_DIM))
        ssd_f = ssd_scan(u, d1, par[0], geom, dt_cb0, bwd=False)
        ssd_o = ssd_scan(u, d1, par[1], geom, dt_cb0, bwd=True, z_cb0=z_cb0, skip=skip,
                         norm_w=row(ssm_norm_w[i]), prev=ssd_f)

        att_o = window_attention(att_sink[i].astype(f32), att_qk, val, geom)

        stream = merge(ret_o, ssd_o, att_o, gate, stream, w_branch[i].astype(bf16), w_out[i].astype(bf16),
                       row(mix_norm_post[i]), mods, geom, rows)

        j = i // 2
        fpre, fpost = row(ffn_norm_pre[i]), row(ffn_norm_post[i])
        if i % 2 == 0:
            stream = dense_ffn(stream, fpre, fpost, mods, ffn_w1[j].astype(bf16), ffn_w3[j].astype(bf16),
                               ffn_w2[j].astype(bf16), geom, rows)
        else:
            stream = moe_ffn(stream, fpre, fpost, mods, moe_router[j], moe_w1[j].astype(bf16),
                             moe_w3[j].astype(bf16), moe_w2[j].astype(bf16), geom, rows)
    return stream[:nx].reshape(batch, seq, d)
```

```python
import functools
import math

import jax
import jax.numpy as jnp
import numpy as np
from jax import lax
from jax.experimental import pallas as pl
from jax.experimental.pallas import tpu as pltpu

f32 = jnp.float32
bf16 = jnp.bfloat16

D_MODEL = 1024
GRID_W = 64
CHUNK = 128
NORM_EPS = 1e-6
ROPE_BASE = 10000.0
NEG_INF = -1e30
RET_HEADS, RET_DK, RET_DV = 4, 128, 256
SSM_HEADS, SSM_HEAD_DIM, SSM_GROUPS, SSM_STATE = 16, 64, 2, 128
SSM_HG = SSM_HEADS // SSM_GROUPS
SSM_GW = SSM_HG * SSM_HEAD_DIM
ATT_HEADS, ATT_KV_HEADS, ATT_HEAD_DIM = 8, 2, 128
ATT_G = ATT_HEADS // ATT_KV_HEADS
D_FF = 2816
N_EXPERTS = 8
IN_SPLITS = (512, 512, 1024, 1024, 1024, 1536, 16, 1024, 256, 256, 3072)

LANES = 128
SUBLANES = 8
SEQ_TILE = 256
MOE_BLOCK = 256
MOE_DISPATCH_ROWS = 256
MOE_COMBINE_ROWS = 256
VMEM_LIMIT = 56 * 2 ** 20


def _cp(sem, vmem=None):
    return pltpu.CompilerParams(dimension_semantics=sem, vmem_limit_bytes=vmem)


def _silu(v):
    return v / (1.0 + jnp.exp(-v))


def _sigmoid(v):
    return 1.0 / (1.0 + jnp.exp(-v))


def _rms(v, w):
    return v * lax.rsqrt(jnp.mean(v * v, axis=-1, keepdims=True) + NORM_EPS) * w


def _norm_mod(x, nw, m, shift_row, scale_row):
    return _rms(x, nw) * (1.0 + m[scale_row:scale_row + 1]) + m[shift_row:shift_row + 1]


def _dot(a, b):
    return jnp.dot(a, b, preferred_element_type=f32)


def _dot_nt(a, b):
    return lax.dot_general(a, b, (((1,), (1,)), ((), ())), preferred_element_type=f32)


def _split3(a):
    hi = a.astype(bf16)
    r = a - hi.astype(f32)
    mid = r.astype(bf16)
    lo = (r - mid.astype(f32)).astype(bf16)
    return hi, mid, lo


def _adaln_body(c_ref, w_ref, b_ref, o_ref):
    s = _silu(c_ref[...])
    o_ref[0] = _dot(s.astype(bf16), w_ref[0].astype(bf16)) + b_ref[0]


def adaln_all(cond, ada_w, ada_b):
    depth, d, n = ada_w.shape
    rows = cond.shape[0]
    tn = 1024
    return pl.pallas_call(
        _adaln_body,
        grid=(depth, n // tn),
        in_specs=[pl.BlockSpec((rows, d), lambda l, j: (0, 0)),
                  pl.BlockSpec((1, d, tn), lambda l, j: (l, 0, j)),
                  pl.BlockSpec((1, 1, tn), lambda l, j: (l, 0, j))],
        out_specs=pl.BlockSpec((1, rows, tn), lambda l, j: (l, 0, j)),
        out_shape=jax.ShapeDtypeStruct((depth, rows, n), f32),
        compiler_params=_cp(("arbitrary", "arbitrary")),
        name="adaln",
    )(cond, ada_w, ada_b.reshape(depth, 1, n))


def _proj_body(*refs, rope_shifts, tn):
    x_ref, nw_ref, mod_ref, w_ref = refs[:4]
    n_rope = len(rope_shifts)
    if n_rope:
        cs_ref, cos_ref = refs[4:6]
        sin_refs = refs[6:6 + n_rope]
    o_ref, h_scr = refs[-2:]

    @pl.when(pl.program_id(1) == 0)
    def _():
        h_scr[...] = _norm_mod(x_ref[...], nw_ref[...], mod_ref[0], 0, 1).astype(bf16)

    acc = _dot(h_scr[...], w_ref[...])
    if not n_rope:
        o_ref[...] = acc.astype(o_ref.dtype)
        return
    cos = cos_ref[...]
    sins = [r[...] for r in sin_refs]
    for c in range(tn // LANES):
        sl = slice(c * LANES, (c + 1) * LANES)
        t = acc[:, sl]
        o = t * cos
        for sh, s in zip(rope_shifts, sins):
            o = o + pltpu.roll(t, sh, 1) * s
        o_ref[:, sl] = (o * cs_ref[:, sl]).astype(o_ref.dtype)


def in_proj(x, nw, mods, w, geom, out_dtype, tn, rope=None):
    ntok, d = x.shape
    n = w.shape[1]
    tm = geom["tm"]
    nxt = geom["nx"] // tm
    per_b = geom["S"] // tm
    nb = geom["B"]

    def mod_idx(i, j):
        return (jnp.where(i < nxt, i // per_b, nb), 0, 0)

    def pos_idx(i, j):
        return (jnp.where(i < nxt, i % per_b, per_b), 0)

    in_specs = [pl.BlockSpec((tm, d), lambda i, j: (i, 0)),
                pl.BlockSpec((1, d), lambda i, j: (0, 0)),
                pl.BlockSpec((1, 6, d), mod_idx),
                pl.BlockSpec((d, tn), lambda i, j: (0, j))]
    args = [x, nw, mods, w]
    shifts = ()
    if rope is not None:
        shifts, colscale, cos, sins = rope
        in_specs.append(pl.BlockSpec((1, tn), lambda i, j: (0, j)))
        args.append(colscale)
        for t in (cos,) + tuple(sins):
            in_specs.append(pl.BlockSpec((tm, LANES), pos_idx))
            args.append(t)
    return pl.pallas_call(
        functools.partial(_proj_body, rope_shifts=tuple(shifts), tn=tn),
        grid=(ntok // tm, n // tn),
        in_specs=in_specs,
        out_specs=pl.BlockSpec((tm, tn), lambda i, j: (i, j)),
        out_shape=jax.ShapeDtypeStruct((ntok, n), out_dtype),
        scratch_shapes=[pltpu.VMEM((tm, d), bf16)],
        compiler_params=_cp(("arbitrary", "arbitrary"), VMEM_LIMIT),
        name="in_proj",
    )(*args)


def _conv_body(x_ref, prev_ref, next_ref, w_ref, b_ref, o_ref, *, tiles_per_seq, nx_tiles):
    i = pl.program_id(0)
    x = x_ref[...]
    rows = x.shape[0]
    t_in = i % tiles_per_seq
    is_x = i < nx_tiles
    use_prev = jnp.logical_and(is_x, t_in > 0)
    use_next = jnp.logical_and(is_x, t_in < tiles_per_seq - 1)
    prow = jnp.where(use_prev, prev_ref[SUBLANES - 1:SUBLANES, :], 0.0)
    nrow = jnp.where(use_next, next_ref[0:1, :], 0.0)
    rid = lax.broadcasted_iota(jnp.int32, x.shape, 0)
    xm = jnp.where(rid == 0, prow, pltpu.roll(x, 1, 0))
    xp = jnp.where(rid == rows - 1, nrow, pltpu.roll(x, rows - 1, 0))
    w = w_ref[...]
    o_ref[...] = _silu(xm * w[0:1] + x * w[1:2] + xp * w[2:3] + b_ref[...])


def ssd_conv(d1, conv_w, conv_b, geom, col0):
    ntok = d1.shape[0]
    width = conv_w.shape[1]
    tc = 512
    cb0 = col0 // tc
    tr = SEQ_TILE
    per8 = tr // SUBLANES
    last8 = ntok // SUBLANES - 1
    return pl.pallas_call(
        functools.partial(_conv_body, tiles_per_seq=geom["S"] // tr, nx_tiles=geom["nx"] // tr),
        grid=(ntok // tr, width // tc),
        in_specs=[pl.BlockSpec((tr, tc), lambda i, j: (i, cb0 + j)),
                  pl.BlockSpec((SUBLANES, tc), lambda i, j: (jnp.maximum(i * per8 - 1, 0), cb0 + j)),
                  pl.BlockSpec((SUBLANES, tc), lambda i, j: (jnp.minimum((i + 1) * per8, last8), cb0 + j)),
                  pl.BlockSpec((3, tc), lambda i, j: (0, j)),
                  pl.BlockSpec((1, tc), lambda i, j: (0, j))],
        out_specs=pl.BlockSpec((tr, tc), lambda i, j: (i, j)),
        out_shape=jax.ShapeDtypeStruct((ntok, width), f32),
        compiler_params=_cp(("arbitrary", "arbitrary")),
        name="ssd_conv",
    )(d1, d1, d1, conv_w, conv_b.reshape(1, width))


def _seq_tile_index(geom, bwd):
    per_b = geom["S"] // SEQ_TILE
    ctx0 = geom["nx"] // SEQ_TILE

    def idx(b, s):
        xt = (per_b - s) if bwd else (s - 1)
        return jnp.where(s == 0, ctx0 + b, b * per_b + xt)

    return idx, per_b + 1


def _expand_heads(a, lane):
    rows = a.shape[0]
    lane = lane[:rows]
    cols = []
    for v in range(SSM_HG // 2):
        left = jnp.broadcast_to(a[:, 2 * v:2 * v + 1], (rows, LANES))
        right = jnp.broadcast_to(a[:, 2 * v + 1:2 * v + 2], (rows, LANES))
        cols.append(jnp.where(lane < SSM_HEAD_DIM, left, right))
    return jnp.concatenate(cols, axis=1)


def _ssd_body(*refs, bwd):
    if bwd:
        xs_ref, b_ref, c_ref, dt_ref, par_ref, z_ref, skip_ref, nw_ref, prev_ref, o_ref, s_scr, y_scr = refs
    else:
        xs_ref, b_ref, c_ref, dt_ref, par_ref, o_ref, s_scr, y_scr = refs

    @pl.when(pl.program_id(2) == 0)
    def _():
        s_scr[...] = jnp.zeros_like(s_scr)

    par = par_ref[0]
    ii = lax.broadcasted_iota(jnp.int32, (CHUNK, CHUNK), 0)
    jj = lax.broadcasted_iota(jnp.int32, (CHUNK, CHUNK), 1)
    tri_mask = (ii <= jj) if bwd else (ii >= jj)
    tri = jnp.where(tri_mask, 1.0, 0.0).astype(bf16)
    lane = jj

    chunks = range(SEQ_TILE // CHUNK)
    for ci in (reversed(chunks) if bwd else chunks):
        sl = slice(ci * CHUNK, (ci + 1) * CHUNK)
        pre = dt_ref[sl, :] + par[0:1]
        dt = jnp.maximum(pre, 0.0) + jnp.log(1.0 + jnp.exp(-jnp.abs(pre)))
        dta = dt * par[1:2]
        hi, mid, lo = _split3(dta)
        cum = _dot(tri, hi) + _dot(tri, mid) + _dot(tri, lo)
        tot = cum[0:1] if bwd else cum[CHUNK - 1:CHUNK]
        cum_t = cum.T
        dt_x = _expand_heads(dt, lane)
        ec_x = _expand_heads(jnp.exp(cum), lane)
        ws_x = _expand_heads(jnp.exp(tot - cum), lane)
        dec_x = _expand_heads(jnp.broadcast_to(jnp.exp(tot), (SUBLANES, LANES)), lane)[0:1]

        xs = xs_ref[sl, :]
        bm = b_ref[sl, :]
        cm = c_ref[sl, :].astype(bf16)
        xdt = xs * dt_x
        xdt_b = xdt.astype(bf16)
        cb = _dot_nt(cm, bm.astype(bf16))
        state = s_scr[...]
        for h in range(SSM_HG):
            seg = cum[:, h:h + 1] - cum_t[h:h + 1, :]
            lmat = jnp.exp(jnp.where(tri_mask, seg, NEG_INF))
            hs = slice(h * SSM_HEAD_DIM, (h + 1) * SSM_HEAD_DIM)
            y_scr[:, hs] = _dot((cb * lmat).astype(bf16), xdt_b[:, hs])
        y = y_scr[...] + _dot(cm, state.astype(bf16)) * ec_x
        s_scr[...] = dec_x * state + _dot(bm.T.astype(bf16), (xdt * ws_x).astype(bf16))
        if bwd:
            y = y + prev_ref[sl, :] + xs * skip_ref[...]
            gz = y * _silu(z_ref[sl, :])
            o_ref[sl, :] = _rms(gz, nw_ref[...]).astype(o_ref.dtype)
        else:
            o_ref[sl, :] = y


def ssd_scan(u, d1, par, geom, dt_cb0, bwd, z_cb0=None, skip=None, norm_w=None, prev=None):
    ntok = u.shape[0]
    idx, steps = _seq_tile_index(geom, bwd)
    gw = SSM_GW
    nb_blk = SSM_HEADS * SSM_HEAD_DIM // SSM_STATE
    tile = lambda w, cb: pl.BlockSpec((SEQ_TILE, w), lambda b, g, s, cb=cb: (idx(b, s), cb(g)))
    in_specs = [tile(gw, lambda g: g),
                tile(SSM_STATE, lambda g: nb_blk + g),
                tile(SSM_STATE, lambda g: nb_blk + SSM_GROUPS + g),
                tile(LANES, lambda g: dt_cb0 + g),
                pl.BlockSpec((1, 2, LANES), lambda b, g, s: (g, 0, 0))]
    args = [u, u, u, d1, par]
    if bwd:
        in_specs += [tile(gw, lambda g: z_cb0 + g),
                     pl.BlockSpec((1, gw), lambda b, g, s: (0, g)),
                     pl.BlockSpec((1, gw), lambda b, g, s: (0, g)),
                     tile(gw, lambda g: g)]
        args += [d1, skip, norm_w, prev]
    return pl.pallas_call(
        functools.partial(_ssd_body, bwd=bwd),
        grid=(geom["B"], SSM_GROUPS, steps),
        in_specs=in_specs,
        out_specs=tile(gw, lambda g: g),
        out_shape=jax.ShapeDtypeStruct((ntok, SSM_GROUPS * gw), bf16 if bwd else f32),
        scratch_shapes=[pltpu.VMEM((SSM_STATE, gw), f32), pltpu.VMEM((CHUNK, gw), f32)],
        compiler_params=_cp(("arbitrary", "arbitrary", "arbitrary")),
        name="ssd_bwd" if bwd else "ssd_fwd",
    )(*args)


def _ret_body(*refs, bwd):
    if bwd:
        lg_ref, q_ref, k_ref, v_ref, g_ref, prev_ref, o_ref, s_scr = refs
    else:
        lg_ref, q_ref, k_ref, v_ref, o_ref, s_scr = refs

    @pl.when(pl.program_id(2) == 0)
    def _():
        s_scr[...] = jnp.zeros_like(s_scr)

    lg = lg_ref[1 if bwd else 0, pl.program_id(1)]
    ii = lax.broadcasted_iota(jnp.int32, (CHUNK, CHUNK), 0).astype(f32)
    jj = lax.broadcasted_iota(jnp.int32, (CHUNK, CHUNK), 1).astype(f32)
    diff = (jj - ii) if bwd else (ii - jj)
    dmat = jnp.where(diff >= 0, jnp.exp(jnp.maximum(diff, 0.0) * lg), 0.0)
    icol = ii[:, 0:1]
    jrow = jj[0:1, :]
    if bwd:
        wq = jnp.exp((CHUNK - icol) * lg)
        ws = jnp.exp(jrow * lg)
    else:
        wq = jnp.exp((icol + 1.0) * lg)
        ws = jnp.exp((CHUNK - 1.0 - jrow) * lg)
    decay = jnp.exp(jnp.full((1, 1), float(CHUNK), f32) * lg)

    chunks = range(SEQ_TILE // CHUNK)
    for ci in (reversed(chunks) if bwd else chunks):
        sl = slice(ci * CHUNK, (ci + 1) * CHUNK)
        q = q_ref[sl, :]
        k = k_ref[sl, :]
        v = v_ref[sl, :]
        state = s_scr[...]
        sc = _dot_nt(q, k) * dmat
        o = _dot(sc.astype(bf16), v) + _dot(q, state.astype(bf16)) * wq
        kt = (k.astype(f32).T * ws).astype(bf16)
        s_scr[...] = decay * state + _dot(kt, v)
        if bwd:
            o = o + prev_ref[sl, :]
            o = o * lax.rsqrt(jnp.mean(o * o, axis=-1, keepdims=True) + NORM_EPS)
            o_ref[sl, :] = (_silu(g_ref[sl, :]) * o).astype(o_ref.dtype)
        else:
            o_ref[sl, :] = o


def ret_scan(lg, qk, vb, geom, bwd, d1=None, prev=None):
    ntok = qk.shape[0]
    idx, steps = _seq_tile_index(geom, bwd)
    tile = lambda w, cb: pl.BlockSpec((SEQ_TILE, w), lambda b, h, s, cb=cb: (idx(b, s), cb(h)))
    in_specs = [pl.BlockSpec(memory_space=pltpu.SMEM),
                tile(RET_DK, lambda h: h),
                tile(RET_DK, lambda h: RET_HEADS + h),
                tile(RET_DV, lambda h: h)]
    args = [lg, qk, qk, vb]
    if bwd:
        in_specs += [tile(RET_DV, lambda h: h), tile(RET_DV, lambda h: h)]
        args += [d1, prev]
    return pl.pallas_call(
        functools.partial(_ret_body, bwd=bwd),
        grid=(geom["B"], RET_HEADS, steps),
        in_specs=in_specs,
        out_specs=tile(RET_DV, lambda h: h),
        out_shape=jax.ShapeDtypeStruct((ntok, RET_HEADS * RET_DV), bf16 if bwd else f32),
        scratch_shapes=[pltpu.VMEM((RET_DK, RET_DV), f32)],
        compiler_params=_cp(("arbitrary", "arbitrary", "arbitrary")),
        name="ret_bwd" if bwd else "ret_fwd",
    )(*args)


def _att_body(sink_ref, q_ref, kc_ref, vc_ref, kp_ref, ko_ref, kn_ref, vp_ref, vo_ref, vn_ref, o_ref, *, nblk, ctx_blk):
    n = pl.program_id(1) - ctx_blk
    own_ok = n >= 0
    prev_ok = n >= 1
    next_ok = jnp.logical_and(n >= 0, n <= nblk - 2)
    qi = lax.broadcasted_iota(jnp.int32, (CHUNK, CHUNK), 0)
    kj = lax.broadcasted_iota(jnp.int32, (CHUNK, CHUNK), 1)
    m_prev = jnp.logical_and(kj >= qi, prev_ok)
    m_next = jnp.logical_and(kj <= qi, next_ok)
    hd = ATT_HEAD_DIM
    for kv in range(ATT_KV_HEADS):
        ks = slice(kv * hd, (kv + 1) * hd)
        kc, vc = kc_ref[:, ks], vc_ref[:, ks]
        kp, ko, kn = kp_ref[:, ks], ko_ref[:, ks], kn_ref[:, ks]
        vp, vo, vn = vp_ref[:, ks], vo_ref[:, ks], vn_ref[:, ks]
        for g in range(ATT_G):
            hq = kv * ATT_G + g
            qs = slice(hq * hd, (hq + 1) * hd)
            q = q_ref[:, qs]
            s_c = _dot_nt(q, kc)
            s_p = jnp.where(m_prev, _dot_nt(q, kp), NEG_INF)
            s_o = jnp.where(own_ok, _dot_nt(q, ko), NEG_INF)
            s_n = jnp.where(m_next, _dot_nt(q, kn), NEG_INF)
            sk = sink_ref[hq]
            mx = jnp.maximum(jnp.maximum(jnp.max(s_c, axis=-1, keepdims=True), jnp.max(s_p, axis=-1, keepdims=True)),
                             jnp.maximum(jnp.max(s_o, axis=-1, keepdims=True), jnp.max(s_n, axis=-1, keepdims=True)))
            mx = jnp.maximum(mx, sk)
            p_c, p_p, p_o, p_n = (jnp.exp(t - mx) for t in (s_c, s_p, s_o, s_n))
            den = (jnp.exp(sk - mx) + jnp.sum(p_c, axis=-1, keepdims=True) + jnp.sum(p_p, axis=-1, keepdims=True)
                   + jnp.sum(p_o, axis=-1, keepdims=True) + jnp.sum(p_n, axis=-1, keepdims=True))
            o = (_dot(p_c.astype(bf16), vc) + _dot(p_p.astype(bf16), vp)
                 + _dot(p_o.astype(bf16), vo) + _dot(p_n.astype(bf16), vn))
            o_ref[:, qs] = (o / den).astype(o_ref.dtype)


def window_attention(sink, qk, vb, geom):
    ntok = qk.shape[0]
    blk = CHUNK
    nblk = geom["S"] // blk
    ctx_blk = geom["L"] // blk
    ctx0 = geom["nx"] // blk
    kvw = ATT_KV_HEADS * ATT_HEAD_DIM
    qw = ATT_HEADS * ATT_HEAD_DIM
    kcol = qw // kvw

    def q_idx(b, i):
        return jnp.where(i < ctx_blk, ctx0 + b * ctx_blk + i, b * nblk + i - ctx_blk)

    def win(off):
        def f(b, i):
            n = jnp.clip(i - ctx_blk + off, 0, nblk - 1)
            return (b * nblk + n, kcol)
        return f

    ctx_spec = pl.BlockSpec((geom["L"], kvw), lambda b, i: (geom["nx"] // geom["L"] + b, kcol))
    wspec = lambda off: pl.BlockSpec((blk, kvw), win(off))
    return pl.pallas_call(
        functools.partial(_att_body, nblk=nblk, ctx_blk=ctx_blk),
        grid=(geom["B"], ctx_blk + nblk),
        in_specs=[pl.BlockSpec(memory_space=pltpu.SMEM),
                  pl.BlockSpec((blk, qw), lambda b, i: (q_idx(b, i), 0)),
                  ctx_spec, ctx_spec,
                  wspec(-1), wspec(0), wspec(1), wspec(-1), wspec(0), wspec(1)],
        out_specs=pl.BlockSpec((blk, qw), lambda b, i: (q_idx(b, i), 0)),
        out_shape=jax.ShapeDtypeStruct((ntok, qw), bf16),
        compiler_params=_cp(("arbitrary", "arbitrary")),
        name="window_attention",
    )(sink, qk, qk, vb, qk, qk, qk, vb, vb, vb)


def _merge_body(r_ref, s_ref, a_ref, gate_ref, x_ref, wb_ref, wo_ref, nw_ref, mod_ref, o_ref):
    acc = None
    for n, ref in enumerate((r_ref, s_ref, a_ref)):
        t = _sigmoid(gate_ref[:, n * D_MODEL:(n + 1) * D_MODEL]) * _dot(ref[...], wb_ref[n])
        acc = t if acc is None else acc + t
    y = _dot(acc.astype(bf16), wo_ref[...])
    o_ref[...] = x_ref[...] + mod_ref[0][2:3] * _rms(y, nw_ref[...])


def _row_tile_mod_idx(geom, tm):
    nxt = geom["nx"] // tm
    per_b = geom["S"] // tm
    return lambda i: (jnp.where(i < nxt, i // per_b, geom["B"]), 0, 0)


def merge(ret_o, ssd_o, att_o, gate, x, wb, wo, nw, mods, geom, rows):
    d = D_MODEL
    tm = geom["tm_merge"]
    row = lambda w: pl.BlockSpec((tm, w), lambda i: (i, 0))
    return pl.pallas_call(
        _merge_body,
        grid=(rows // tm,),
        in_specs=[row(d), row(d), row(d), row(3 * d), row(d),
                  pl.BlockSpec((3, d, d), lambda i: (0, 0, 0)),
                  pl.BlockSpec((d, d), lambda i: (0, 0)),
                  pl.BlockSpec((1, d), lambda i: (0, 0)),
                  pl.BlockSpec((1, 6, d), _row_tile_mod_idx(geom, tm))],
        out_specs=row(d),
        out_shape=jax.ShapeDtypeStruct((rows, d), f32),
        compiler_params=_cp(("arbitrary",), VMEM_LIMIT),
        name="merge",
    )(ret_o, ssd_o, att_o, gate, x, wb, wo, nw, mods)


def _ffn_body(x_ref, npre_ref, npost_ref, mod_ref, w1_ref, w3_ref, w2_ref, o_ref, h_scr, acc_scr):
    j = pl.program_id(1)

    @pl.when(j == 0)
    def _():
        h_scr[...] = _norm_mod(x_ref[...], npre_ref[...], mod_ref[0], 3, 4).astype(bf16)
        acc_scr[...] = jnp.zeros_like(acc_scr)

    h = h_scr[...]
    act = _silu(_dot(h, w1_ref[...])) * _dot(h, w3_ref[...])
    acc_scr[...] += _dot(act.astype(bf16), w2_ref[...])

    @pl.when(j == pl.num_programs(1) - 1)
    def _():
        o_ref[...] = x_ref[...] + mod_ref[0][5:6] * _rms(acc_scr[...], npost_ref[...])


def dense_ffn(x, npre, npost, mods, w1, w3, w2, geom, rows):
    d = D_MODEL
    ff = w1.shape[1]
    tm = geom["tm"]
    tf = 256
    return pl.pallas_call(
        _ffn_body,
        grid=(rows // tm, ff // tf),
        in_specs=[pl.BlockSpec((tm, d), lambda i, j: (i, 0)),
                  pl.BlockSpec((1, d), lambda i, j: (0, 0)),
                  pl.BlockSpec((1, d), lambda i, j: (0, 0)),
                  pl.BlockSpec((1, 6, d), lambda i, j: _row_tile_mod_idx(geom, tm)(i)),
                  pl.BlockSpec((d, tf), lambda i, j: (0, j)),
                  pl.BlockSpec((d, tf), lambda i, j: (0, j)),
                  pl.BlockSpec((tf, d), lambda i, j: (j, 0))],
        out_specs=pl.BlockSpec((tm, d), lambda i, j: (i, 0)),
        out_shape=jax.ShapeDtypeStruct((rows, d), f32),
        scratch_shapes=[pltpu.VMEM((tm, d), bf16), pltpu.VMEM((tm, d), f32)],
        compiler_params=_cp(("arbitrary", "arbitrary"), VMEM_LIMIT),
        name="dense_ffn",
    )(x, npre, npost, mods, w1, w3, w2)


def _router_body(x_ref, npre_ref, mod_ref, wr_ref, h_ref, route_ref):
    h = _norm_mod(x_ref[...], npre_ref[...], mod_ref[0], 3, 4)
    h_ref[...] = h
    h_hi = h.astype(bf16)
    h_lo = (h - h_hi.astype(f32)).astype(bf16)
    wr = wr_ref[...]
    w_hi = wr.astype(bf16)
    w_lo = (wr - w_hi.astype(f32)).astype(bf16)
    logits = _dot(h_hi, w_hi) + _dot(h_hi, w_lo) + _dot(h_lo, w_hi) + _dot(h_lo, w_lo)
    lane = lax.broadcasted_iota(jnp.int32, logits.shape, 1).astype(f32)
    l1 = jnp.where(lane < N_EXPERTS, logits, -jnp.inf)
    m1 = jnp.max(l1, axis=-1, keepdims=True)
    i1 = jnp.min(jnp.where(l1 == m1, lane, float(LANES)), axis=-1, keepdims=True)
    l2 = jnp.where(lane == i1, -jnp.inf, l1)
    m2 = jnp.max(l2, axis=-1, keepdims=True)
    i2 = jnp.min(jnp.where(l2 == m2, lane, float(LANES)), axis=-1, keepdims=True)
    e = jnp.exp(m2 - m1)
    w1 = 1.0 / (1.0 + e)
    w2 = e / (1.0 + e)
    route_ref[...] = jnp.where(lane == 0.0, i1, jnp.where(lane == 1.0, i2,
                                                          jnp.where(lane == 2.0, w1, jnp.where(lane == 3.0, w2, 0.0))))


def moe_router(x, npre, mods, w_router, geom, rows):
    d = D_MODEL
    tm = geom["tm_merge"]
    wr = jnp.pad(w_router, ((0, 0), (0, LANES - w_router.shape[1])))
    return pl.pallas_call(
        _router_body,
        grid=(rows // tm,),
        in_specs=[pl.BlockSpec((tm, d), lambda i: (i, 0)),
                  pl.BlockSpec((1, d), lambda i: (0, 0)),
                  pl.BlockSpec((1, 6, d), _row_tile_mod_idx(geom, tm)),
                  pl.BlockSpec((d, LANES), lambda i: (0, 0))],
        out_specs=[pl.BlockSpec((tm, d), lambda i: (i, 0)), pl.BlockSpec((tm, LANES), lambda i: (i, 0))],
        out_shape=[jax.ShapeDtypeStruct((rows, d), f32), jax.ShapeDtypeStruct((rows, LANES), f32)],
        compiler_params=_cp(("arbitrary",)),
        name="moe_router",
    )(x, npre, mods, wr)


def _row_copy(src, src_row, dst, dst_row, sem):
    return pltpu.make_async_copy(src.at[pl.ds(src_row, 1)], dst.at[pl.ds(dst_row, 1)], sem)


def _dispatch_body(slot_ref, h_ref, xb_in, xb_hbm, sem, *, rows):
    del xb_in

    def start(r, c):
        _row_copy(h_ref, r, xb_hbm, slot_ref[0, 0, 2 * r], sem).start()
        _row_copy(h_ref, r, xb_hbm, slot_ref[0, 0, 2 * r + 1], sem).start()
        return c

    def wait(a, c):
        _row_copy(h_ref, 0, xb_hbm, 0, sem).wait()
        return c

    lax.fori_loop(0, rows, start, 0)
    lax.fori_loop(0, 2 * rows, wait, 0)


def moe_dispatch(h, slots, n_slots):
    n_tok, d = h.shape
    rows = MOE_DISPATCH_ROWS
    steps = n_tok // rows
    return pl.pallas_call(
        functools.partial(_dispatch_body, rows=rows),
        grid=(steps,),
        in_specs=[pl.BlockSpec((1, 1, 2 * rows), lambda i: (i, 0, 0), memory_space=pltpu.SMEM),
                  pl.BlockSpec((rows, d), lambda i: (i, 0)),
                  pl.BlockSpec(memory_space=pl.ANY)],
        out_specs=pl.BlockSpec(memory_space=pl.ANY),
        out_shape=jax.ShapeDtypeStruct((n_slots, d), f32),
        scratch_shapes=[pltpu.SemaphoreType.DMA(())],
        input_output_aliases={2: 0},
        compiler_params=_cp(("arbitrary",)),
        name="moe_dispatch",
    )(slots.reshape(steps, 1, 2 * rows), h, jnp.zeros((n_slots, d), f32))


def _expert_body(be_ref, cnt_ref, used_ref, x_ref, w1_ref, w3_ref, w2_ref, o_ref, *, f_split):
    del be_ref, used_ref
    i = pl.program_id(0)

    @pl.when(cnt_ref[i] > 0)
    def _():
        x = x_ref[...].astype(bf16)
        ff = w1_ref.shape[2]
        step = ff // f_split
        acc = None
        for f in range(f_split):
            fs = slice(f * step, (f + 1) * step)
            act = _silu(_dot(x, w1_ref[0, :, fs])) * _dot(x, w3_ref[0, :, fs])
            t = _dot(act.astype(bf16), w2_ref[0, fs, :])
            acc = t if acc is None else acc + t
        o_ref[...] = acc

    @pl.when(cnt_ref[i] == 0)
    def _():
        o_ref[...] = jnp.zeros_like(o_ref)


def moe_experts(xb, blk_e, blk_cnt, n_used, w1, w3, w2):
    n_slots, d = xb.shape
    ff = w1.shape[2]
    tm = MOE_BLOCK
    grid_spec = pltpu.PrefetchScalarGridSpec(
        num_scalar_prefetch=3,
        grid=(n_slots // tm,),
        in_specs=[pl.BlockSpec((tm, d), lambda i, be, cnt, used: (jnp.minimum(i, used[0] - 1), 0)),
                  pl.BlockSpec((1, d, ff), lambda i, be, cnt, used: (be[i], 0, 0)),
                  pl.BlockSpec((1, d, ff), lambda i, be, cnt, used: (be[i], 0, 0)),
                  pl.BlockSpec((1, ff, d), lambda i, be, cnt, used: (be[i], 0, 0))],
        out_specs=pl.BlockSpec((tm, d), lambda i, be, cnt, used: (i, 0)),
    )
    return pl.pallas_call(
        functools.partial(_expert_body, f_split=2),
        grid_spec=grid_spec,
        out_shape=jax.ShapeDtypeStruct((n_slots, d), f32),
        compiler_params=_cp(("arbitrary",), VMEM_LIMIT),
        name="moe_experts",
    )(blk_e, blk_cnt, n_used, xb, w1, w3, w2)


def _combine_body(slot_ref, route_ref, x_ref, npost_ref, mod_ref, yb_hbm, o_ref, y0_scr, y1_scr, sem, *, rows):
    def start(r, c):
        _row_copy(yb_hbm, slot_ref[0, 0, 2 * r], y0_scr, r, sem).start()
        _row_copy(yb_hbm, slot_ref[0, 0, 2 * r + 1], y1_scr, r, sem).start()
        return c

    def wait(r, c):
        _row_copy(yb_hbm, 0, y0_scr, 0, sem).wait()
        return c

    lax.fori_loop(0, rows, start, 0)
    lax.fori_loop(0, 2 * rows, wait, 0)
    y = y0_scr[...] * route_ref[:, 2:3] + y1_scr[...] * route_ref[:, 3:4]
    o_ref[...] = x_ref[...] + mod_ref[0][5:6] * _rms(y, npost_ref[...])


def moe_combine(slots, route, x, npost, mods, yb, geom, rows_total):
    d = D_MODEL
    rows = MOE_COMBINE_ROWS
    steps = rows_total // rows
    return pl.pallas_call(
        functools.partial(_combine_body, rows=rows),
        grid=(steps,),
        in_specs=[pl.BlockSpec((1, 1, 2 * rows), lambda i: (i, 0, 0), memory_space=pltpu.SMEM),
                  pl.BlockSpec((rows, LANES), lambda i: (i, 0)),
                  pl.BlockSpec((rows, d), lambda i: (i, 0)),
                  pl.BlockSpec((1, d), lambda i: (0, 0)),
                  pl.BlockSpec((1, 6, d), _row_tile_mod_idx(geom, rows)),
                  pl.BlockSpec(memory_space=pl.ANY)],
        out_specs=pl.BlockSpec((rows, d), lambda i: (i, 0)),
        out_shape=jax.ShapeDtypeStruct((rows_total, d), f32),
        scratch_shapes=[pltpu.VMEM((rows, d), f32), pltpu.VMEM((rows, d), f32), pltpu.SemaphoreType.DMA(())],
        compiler_params=_cp(("arbitrary",)),
        name="moe_combine",
    )(slots.reshape(steps, 1, 2 * rows), route, x, npost, mods, yb)


def _moe_slots(route, n_tok):
    tm = MOE_BLOCK
    flat_e = route[:, 0:2].astype(jnp.int32).reshape(2 * n_tok)
    onehot = (flat_e[:, None] == jnp.arange(N_EXPERTS, dtype=jnp.int32)[None, :]).astype(jnp.int32)
    csum = jnp.cumsum(onehot, axis=0)
    counts = csum[-1]
    rank = jnp.sum(csum * onehot, axis=1) - 1
    padded = (counts + tm - 1) // tm * tm
    pend = jnp.cumsum(padded)
    pstart = pend - padded
    slots = jnp.sum(onehot * pstart[None, :], axis=1) + rank
    n_blocks = 2 * n_tok // tm + N_EXPERTS
    blk = jnp.arange(n_blocks, dtype=jnp.int32)
    blk_e = jnp.minimum(jnp.sum((blk[:, None] >= (pend // tm)[None, :]).astype(jnp.int32), axis=1), N_EXPERTS - 1)
    n_used = (pend[-1] // tm).astype(jnp.int32)
    cnt = jnp.clip(counts[blk_e] - (blk - pstart[blk_e] // tm) * tm, 0, tm)
    blk_cnt = jnp.where(blk < n_used, cnt, 0).astype(jnp.int32)
    return slots.astype(jnp.int32), blk_e, blk_cnt, n_used.reshape(1), n_blocks * tm


def moe_ffn(x, npre, npost, mods, w_router, w1, w3, w2, geom, rows):
    h, route = moe_router(x, npre, mods, w_router, geom, rows)
    slots, blk_e, blk_cnt, n_used, n_slots = _moe_slots(route, rows)
    xb = moe_dispatch(h, slots, n_slots)
    yb = moe_experts(xb, blk_e, blk_cnt, n_used, w1, w3, w2)
    return moe_combine(slots, route, x, npost, mods, yb, geom, rows)


def _rope_tables(geom):
    s, tm = geom["S"], geom["tm"]
    pos = jnp.arange(s)

    def angles(p, dim):
        inv = ROPE_BASE ** (-jnp.arange(0, dim, 2, dtype=f32) / dim)
        return p.astype(f32)[:, None] * inv[None, :]

    def with_identity(t, one):
        return jnp.concatenate([t, jnp.full((tm, LANES), one, f32)], axis=0)

    a = angles(pos, RET_DK)
    ret_cos = jnp.concatenate([jnp.cos(a), jnp.cos(a)], axis=1)
    ret_sin = jnp.concatenate([-jnp.sin(a), jnp.sin(a)], axis=1)
    half = ATT_HEAD_DIM // 2
    ar = angles(pos // GRID_W, half)
    ac = angles(pos % GRID_W, half)
    z = jnp.zeros_like(ar)
    att_cos = jnp.concatenate([jnp.cos(ar), jnp.cos(ar), jnp.cos(ac), jnp.cos(ac)], axis=1)
    att_s_up = jnp.concatenate([-jnp.sin(ar), z, -jnp.sin(ac), z], axis=1)
    att_s_dn = jnp.concatenate([z, jnp.sin(ar), z, jnp.sin(ac)], axis=1)
    ret = ((RET_DK // 2,), with_identity(ret_cos, 1.0), (with_identity(ret_sin, 0.0),))
    att = ((LANES - half // 2, half // 2), with_identity(att_cos, 1.0),
           (with_identity(att_s_up, 0.0), with_identity(att_s_dn, 0.0)))
    return ret, att


def _pack_w_in(w):
    rq, rk, rv, rg, sz, sxbc, sdt, aq, ak, av, bg = jnp.split(w, np.cumsum(IN_SPLITS)[:-1].tolist(), axis=1)
    pad = lambda t, n: jnp.pad(t, ((0, 0), (0, n - t.shape[1])))
    w_a = jnp.concatenate([rq, rk], axis=1)
    w_b = jnp.concatenate([rv, av], axis=1)
    w_c = jnp.concatenate([aq, ak], axis=1)
    w_d1 = jnp.concatenate([rg, sz, sxbc, pad(sdt[:, :SSM_HG], LANES), pad(sdt[:, SSM_HG:], 3 * LANES)], axis=1)
    return tuple(t.astype(bf16) for t in (w_a, w_b, w_c, w_d1, bg))


def _pad_heads(v):
    return jnp.pad(v.astype(f32).reshape(2, SSM_GROUPS, SSM_HG), ((0, 0), (0, 0), (0, LANES - SSM_HG)))


def _geometry(batch, seq, ctx_len):
    tm = 1024
    while seq % tm or (batch * ctx_len) % tm:
        tm //= 2
    tm_merge = min(tm, 512)
    assert ctx_len == SEQ_TILE and seq % SEQ_TILE == 0 and seq % GRID_W == 0
    return {"B": batch, "S": seq, "L": ctx_len, "nx": batch * seq, "tm": tm, "tm_merge": tm_merge}


def kernel(x, c, ctx, c_ctx, ada_w, ada_b, mix_norm_pre, mix_norm_post, ffn_norm_pre, ffn_norm_post, w_in, ret_decay, ssm_conv_w, ssm_conv_b, ssm_dt_bias, ssm_a_log, ssm_d, ssm_norm_w, att_sink, w_branch, w_out, ffn_w1, ffn_w3, ffn_w2, moe_router, moe_w1, moe_w3, moe_w2):
    batch, seq, d = x.shape
    ctx_len = ctx.shape[1]
    depth = ada_w.shape[0]
    geom = _geometry(batch, seq, ctx_len)
    nx = geom["nx"]
    ntok = nx + batch * ctx_len

    stream = jnp.concatenate([x.reshape(nx, d), ctx.reshape(batch * ctx_len, d)], axis=0)
    cond = jnp.concatenate([c, c_ctx[None, :], jnp.zeros((SUBLANES - 1 - batch % SUBLANES, d), f32)], axis=0)
    mods_all = adaln_all(cond, ada_w, ada_b).reshape(depth, cond.shape[0], 6, d)
    rope_ret, rope_att = _rope_tables(geom)
    ret_cs = jnp.concatenate([jnp.ones((1, RET_HEADS * RET_DK), f32),
                              jnp.full((1, RET_HEADS * RET_DK), RET_DK ** -0.5, f32)], axis=1)
    att_cs = jnp.concatenate([jnp.full((1, ATT_HEADS * ATT_HEAD_DIM), ATT_HEAD_DIM ** -0.5, f32),
                              jnp.ones((1, ATT_KV_HEADS * ATT_HEAD_DIM), f32)], axis=1)
    row = lambda v: v.reshape(1, -1)
    dt_cb0 = (2 * D_MODEL + IN_SPLITS[5]) // LANES
    z_cb0 = D_MODEL // SSM_GW
    conv_col0 = 2 * D_MODEL

    for i in range(depth):
        last = i == depth - 1
        rows = nx if last else ntok
        mods = mods_all[i]
        w_a, w_b, w_c, w_d1, w_d2 = _pack_w_in(w_in[i])
        npre = row(mix_norm_pre[i])
        proj = functools.partial(in_proj, stream, npre, mods, geom=geom)
        ret_qk = proj(w=w_a, out_dtype=bf16, tn=512, rope=(rope_ret[0], ret_cs, rope_ret[1], rope_ret[2]))
        val = proj(w=w_b, out_dtype=bf16, tn=640)
        att_qk = proj(w=w_c, out_dtype=bf16, tn=640, rope=(rope_att[0], att_cs, rope_att[1], rope_att[2]))
        d1 = proj(w=w_d1, out_dtype=f32, tn=512)
        gate = proj(w=w_d2, out_dtype=f32, tn=512)

        lg = jax.nn.log_sigmoid(ret_decay[i].astype(f32))
        ret_f = ret_scan(lg, ret_qk, val, geom, bwd=False)
        ret_o = ret_scan(lg, ret_qk, val, geom, bwd=True, d1=d1, prev=ret_f)

        u = ssd_conv(d1, ssm_conv_w[i], ssm_conv_b[i], geom, conv_col0)
        par = jnp.stack([_pad_heads(ssm_dt_bias[i]), _pad_heads(-jnp.exp(ssm_a_log[i].astype(f32)))], axis=2)
        skip = row(jnp.repeat(ssm_d[i, 0].astype(f32) + ssm_d[i, 1].astype(f32), SSM_HEAD_DIM))
        ssd_f = ssd_scan(u, d1, par[0], geom, dt_cb0, bwd=False)
        ssd_o = ssd_scan(u, d1, par[1], geom, dt_cb0, bwd=True, z_cb0=z_cb0, skip=skip,
                         norm_w=row(ssm_norm_w[i]), prev=ssd_f)

        att_o = window_attention(att_sink[i].astype(f32), att_qk, val, geom)

        stream = merge(ret_o, ssd_o, att_o, gate, stream, w_branch[i].astype(bf16), w_out[i].astype(bf16),
                       row(mix_norm_post[i]), mods, geom, rows)

        j = i // 2
        fpre, fpost = row(ffn_norm_pre[i]), row(ffn_norm_post[i])
        if i % 2 == 0:
            stream = dense_ffn(stream, fpre, fpost, mods, ffn_w1[j].astype(bf16), ffn_w3[j].astype(bf16),
                               ffn_w2[j].astype(bf16), geom, rows)
        else:
            stream = moe_ffn(stream, fpre, fpost, mods, moe_router[j], moe_w1[j].astype(bf16),
                             moe_w3[j].astype(bf16), moe_w2[j].astype(bf16), geom, rows)
    return stream[:nx].reshape(batch, seq, d)
```

```python
import functools
import math

import jax
import jax.numpy as jnp
import numpy as np
from jax import lax
from jax.experimental import pallas as pl
from jax.experimental.pallas import tpu as pltpu

f32 = jnp.float32
bf16 = jnp.bfloat16

D_MODEL = 1024
GRID_W = 64
CHUNK = 128
NORM_EPS = 1e-6
ROPE_BASE = 10000.0
NEG_INF = -1e30
RET_HEADS, RET_DK, RET_DV = 4, 128, 256
SSM_HEADS, SSM_HEAD_DIM, SSM_GROUPS, SSM_STATE = 16, 64, 2, 128
SSM_HG = SSM_HEADS // SSM_GROUPS
SSM_GW = SSM_HG * SSM_HEAD_DIM
ATT_HEADS, ATT_KV_HEADS, ATT_HEAD_DIM = 8, 2, 128
ATT_G = ATT_HEADS // ATT_KV_HEADS
D_FF = 2816
N_EXPERTS = 8
IN_SPLITS = (512, 512, 1024, 1024, 1024, 1536, 16, 1024, 256, 256, 3072)

LANES = 128
SUBLANES = 8
MOE_BLOCK = 256
MOE_DISPATCH_ROWS = 256
MOE_COMBINE_ROWS = 256
VMEM_LIMIT = 56 * 2 ** 20
DMA_ISSUE_UNROLL = 8


def _cp(sem, vmem=None):
    return pltpu.CompilerParams(dimension_semantics=sem, vmem_limit_bytes=vmem)


def _silu(v):
    return v / (1.0 + jnp.exp(-v))


def _sigmoid(v):
    return 1.0 / (1.0 + jnp.exp(-v))


def _rms(v, w):
    return v * lax.rsqrt(jnp.mean(v * v, axis=-1, keepdims=True) + NORM_EPS) * w


def _norm_mod(x, nw, m, shift_row, scale_row):
    return _rms(x, nw) * (1.0 + m[scale_row:scale_row + 1]) + m[shift_row:shift_row + 1]


def _dot(a, b):
    return jnp.dot(a, b, preferred_element_type=f32)


def _dot_nt(a, b):
    return lax.dot_general(a, b, (((1,), (1,)), ((), ())), preferred_element_type=f32)


def _split3(a):
    hi = a.astype(bf16)
    r = a - hi.astype(f32)
    mid = r.astype(bf16)
    lo = (r - mid.astype(f32)).astype(bf16)
    return hi, mid, lo


def _adaln_body(c_ref, w_ref, b_ref, o_ref):
    s = _silu(c_ref[...])
    o_ref[0] = _dot(s.astype(bf16), w_ref[0].astype(bf16)) + b_ref[0]


def adaln_all(cond, ada_w, ada_b):
    depth, d, n = ada_w.shape
    rows = cond.shape[0]
    tn = 1024
    return pl.pallas_call(
        _adaln_body,
        grid=(depth, n // tn),
        in_specs=[pl.BlockSpec((rows, d), lambda l, j: (0, 0)),
                  pl.BlockSpec((1, d, tn), lambda l, j: (l, 0, j)),
                  pl.BlockSpec((1, 1, tn), lambda l, j: (l, 0, j))],
        out_specs=pl.BlockSpec((1, rows, tn), lambda l, j: (l, 0, j)),
        out_shape=jax.ShapeDtypeStruct((depth, rows, n), f32),
        compiler_params=_cp(("arbitrary", "arbitrary")),
        name="adaln",
    )(cond, ada_w, ada_b.reshape(depth, 1, n))


def _proj_body(*refs, rope_shifts, tn):
    x_ref, nw_ref, mod_ref, w_ref = refs[:4]
    n_rope = len(rope_shifts)
    if n_rope:
        cs_ref, cos_ref = refs[4:6]
        sin_refs = refs[6:6 + n_rope]
    o_ref, h_scr = refs[-2:]

    @pl.when(pl.program_id(1) == 0)
    def _():
        h_scr[...] = _norm_mod(x_ref[...], nw_ref[...], mod_ref[0], 0, 1).astype(bf16)

    acc = _dot(h_scr[...], w_ref[...])
    if not n_rope:
        o_ref[...] = acc.astype(o_ref.dtype)
        return
    cos = cos_ref[...]
    sins = [r[...] for r in sin_refs]
    for c in range(tn // LANES):
        sl = slice(c * LANES, (c + 1) * LANES)
        t = acc[:, sl]
        o = t * cos
        for sh, s in zip(rope_shifts, sins):
            o = o + pltpu.roll(t, sh, 1) * s
        o_ref[:, sl] = (o * cs_ref[:, sl]).astype(o_ref.dtype)


def in_proj(x, nw, mods, w, geom, out_dtype, tn, rope=None):
    ntok, d = x.shape
    n = w.shape[1]
    tm = geom["tm"]
    nxt = geom["nx"] // tm
    per_b = geom["S"] // tm
    nb = geom["B"]

    def mod_idx(i, j):
        return (jnp.where(i < nxt, i // per_b, nb), 0, 0)

    def pos_idx(i, j):
        return (jnp.where(i < nxt, i % per_b, per_b), 0)

    in_specs = [pl.BlockSpec((tm, d), lambda i, j: (i, 0)),
                pl.BlockSpec((1, d), lambda i, j: (0, 0)),
                pl.BlockSpec((1, 6, d), mod_idx),
                pl.BlockSpec((d, tn), lambda i, j: (0, j))]
    args = [x, nw, mods, w]
    shifts = ()
    if rope is not None:
        shifts, colscale, cos, sins = rope
        in_specs.append(pl.BlockSpec((1, tn), lambda i, j: (0, j)))
        args.append(colscale)
        for t in (cos,) + tuple(sins):
            in_specs.append(pl.BlockSpec((tm, LANES), pos_idx))
            args.append(t)
    return pl.pallas_call(
        functools.partial(_proj_body, rope_shifts=tuple(shifts), tn=tn),
        grid=(ntok // tm, n // tn),
        in_specs=in_specs,
        out_specs=pl.BlockSpec((tm, tn), lambda i, j: (i, j)),
        out_shape=jax.ShapeDtypeStruct((ntok, n), out_dtype),
        scratch_shapes=[pltpu.VMEM((tm, d), bf16)],
        compiler_params=_cp(("arbitrary", "arbitrary"), VMEM_LIMIT),
        name="in_proj",
    )(*args)


def _conv_body(x_ref, prev_ref, next_ref, w_ref, b_ref, o_ref, *, tiles_per_seq, nx_tiles, ctx_len):
    i = pl.program_id(0)
    x = x_ref[...]
    rows = x.shape[0]
    t_in = i % tiles_per_seq
    is_x = i < nx_tiles
    use_prev = jnp.logical_and(is_x, t_in > 0)
    use_next = jnp.logical_and(is_x, t_in < tiles_per_seq - 1)
    prow = jnp.where(use_prev, prev_ref[SUBLANES - 1:SUBLANES, :], 0.0)
    nrow = jnp.where(use_next, next_ref[0:1, :], 0.0)
    rid = lax.broadcasted_iota(jnp.int32, x.shape, 0)
    xm = jnp.where(rid == 0, prow, pltpu.roll(x, 1, 0))
    xp = jnp.where(rid == rows - 1, nrow, pltpu.roll(x, rows - 1, 0))
    seg = lax.rem(rid, ctx_len)
    is_ctx = jnp.logical_not(is_x)
    xm = jnp.where(jnp.logical_and(is_ctx, seg == 0), 0.0, xm)
    xp = jnp.where(jnp.logical_and(is_ctx, seg == ctx_len - 1), 0.0, xp)
    w = w_ref[...]
    o_ref[...] = _silu(xm * w[0:1] + x * w[1:2] + xp * w[2:3] + b_ref[...])


def ssd_conv(d1, conv_w, conv_b, geom, col0):
    ntok = d1.shape[0]
    width = conv_w.shape[1]
    tc = 512
    cb0 = col0 // tc
    tr = geom["T"]
    per8 = tr // SUBLANES
    last8 = ntok // SUBLANES - 1
    return pl.pallas_call(
        functools.partial(_conv_body, tiles_per_seq=geom["S"] // tr, nx_tiles=geom["nx"] // tr, ctx_len=geom["L"]),
        grid=(ntok // tr, width // tc),
        in_specs=[pl.BlockSpec((tr, tc), lambda i, j: (i, cb0 + j)),
                  pl.BlockSpec((SUBLANES, tc), lambda i, j: (jnp.maximum(i * per8 - 1, 0), cb0 + j)),
                  pl.BlockSpec((SUBLANES, tc), lambda i, j: (jnp.minimum((i + 1) * per8, last8), cb0 + j)),
                  pl.BlockSpec((3, tc), lambda i, j: (0, j)),
                  pl.BlockSpec((1, tc), lambda i, j: (0, j))],
        out_specs=pl.BlockSpec((tr, tc), lambda i, j: (i, j)),
        out_shape=jax.ShapeDtypeStruct((ntok, width), f32),
        compiler_params=_cp(("arbitrary", "arbitrary")),
        name="ssd_conv",
    )(d1, d1, d1, conv_w, conv_b.reshape(1, width))


def _scan_row_index(geom, bwd):
    tile = geom["T"]
    per_b = geom["S"] // tile
    ctx_tile = geom["nx"] // tile

    def idx(s):
        t = s - 1
        xt = (t // per_b) * per_b + (per_b - 1 - t % per_b) if bwd else t
        return jnp.where(s == 0, ctx_tile, xt)

    return idx, ctx_tile + 1


def _scan_driver(chunk, s_all, *, bwd, geom):
    tile, ctx_len, nbatch = geom["T"], geom["L"], geom["B"]
    per_b = geom["S"] // tile
    s = pl.program_id(1)

    @pl.when(s == 0)
    def _():
        for b in range(nbatch):
            st = s_all.at[b]
            st[...] = jnp.zeros_like(st)
            cs = range(ctx_len // CHUNK)
            for ci in (reversed(cs) if bwd else cs):
                chunk(b * ctx_len + ci * CHUNK, st)

    @pl.when(s > 0)
    def _():
        st = s_all.at[(s - 1) // per_b]
        n_pairs = tile // (2 * CHUNK)

        def body(p, c):
            base = pl.multiple_of(((n_pairs - 1 - p) if bwd else p) * (2 * CHUNK), 2 * CHUNK)
            for ci in ((1, 0) if bwd else (0, 1)):
                chunk(base + ci * CHUNK, st)
            return c

        lax.fori_loop(0, n_pairs, body, 0)


def _expand_heads(a, lane):
    rows = a.shape[0]
    lane = lane[:rows]
    cols = []
    for v in range(SSM_HG // 2):
        left = jnp.broadcast_to(a[:, 2 * v:2 * v + 1], (rows, LANES))
        right = jnp.broadcast_to(a[:, 2 * v + 1:2 * v + 2], (rows, LANES))
        cols.append(jnp.where(lane < SSM_HEAD_DIM, left, right))
    return jnp.concatenate(cols, axis=1)


def _ssd_body(*refs, bwd, geom):
    if bwd:
        xs_ref, b_ref, c_ref, dt_ref, par_ref, z_ref, skip_ref, nw_ref, prev_ref, o_ref, s_all = refs
    else:
        xs_ref, b_ref, c_ref, dt_ref, par_ref, o_ref, s_all = refs

    par = par_ref[0]
    ii = lax.broadcasted_iota(jnp.int32, (CHUNK, CHUNK), 0)
    jj = lax.broadcasted_iota(jnp.int32, (CHUNK, CHUNK), 1)
    tri_mask = (ii <= jj) if bwd else (ii >= jj)
    tri = jnp.where(tri_mask, 1.0, 0.0).astype(bf16)
    lane = jj
    last = 0 if bwd else CHUNK - 1
    pair_keep = [jnp.where((lane < SSM_HEAD_DIM) == (a == 0), 1.0, 0.0).astype(bf16) for a in range(2)]

    def chunk(r0, st):
        sl = pl.ds(r0, CHUNK)
        pre = dt_ref[sl, :] + par[0:1]
        dt = jnp.maximum(pre, 0.0) + jnp.log(1.0 + jnp.exp(-jnp.abs(pre)))
        hi, mid, lo = _split3(dt * par[1:2])
        cum = _dot(tri, hi) + _dot(tri, mid) + _dot(tri, lo)
        tot = cum[last:last + 1]
        cum_t = cum.T[0:SUBLANES]
        dt_t = dt.T[0:SUBLANES]
        dtws_t = dt_t * jnp.exp(cum_t[:, last:last + 1] - cum_t)
        dec_x = _expand_heads(jnp.broadcast_to(jnp.exp(tot), (SUBLANES, LANES)), lane)[0:1]

        xs = xs_ref[sl, :]
        xs_b = xs.astype(bf16)
        bm = b_ref[sl, :]
        cm = c_ref[sl, :]
        cb = _dot_nt(cm.astype(bf16), bm.astype(bf16))
        bm_t = bm.T
        state = st[...]
        state_b = state.astype(bf16)
        y_parts, upd_parts = [], []
        for v in range(SSM_HG // 2):
            ps = slice(v * LANES, (v + 1) * LANES)
            y_p = upd_p = None
            for a in range(2):
                h = 2 * v + a
                xs_h = xs_b[:, ps] * pair_keep[a]
                st_h = state_b[:, ps] * pair_keep[a]
                cbc = jnp.broadcast_to(cum[:, h:h + 1], (CHUNK, CHUNK))
                lmat = jnp.exp(jnp.where(tri_mask, cbc - cum_t[h:h + 1, :], NEG_INF))
                intra = (cb * lmat * dt_t[h:h + 1, :]).astype(bf16)
                inter = (cm * jnp.exp(cbc)).astype(bf16)
                t = _dot(intra, xs_h) + _dot(inter, st_h)
                y_p = t if y_p is None else y_p + t
                t = _dot((bm_t * dtws_t[h:h + 1, :]).astype(bf16), xs_h)
                upd_p = t if upd_p is None else upd_p + t
            y_parts.append(y_p)
            upd_parts.append(upd_p)
        y = jnp.concatenate(y_parts, axis=1)
        st[...] = dec_x * state + jnp.concatenate(upd_parts, axis=1)
        if bwd:
            y = y + prev_ref[sl, :] + xs * skip_ref[...]
            gz = y * _silu(z_ref[sl, :])
            o_ref[sl, :] = _rms(gz, nw_ref[...]).astype(o_ref.dtype)
        else:
            o_ref[sl, :] = y

    _scan_driver(chunk, s_all, bwd=bwd, geom=geom)


def ssd_scan(u, d1, par, geom, dt_cb0, bwd, z_cb0=None, skip=None, norm_w=None, prev=None):
    ntok = u.shape[0]
    idx, steps = _scan_row_index(geom, bwd)
    gw = SSM_GW
    nb_blk = SSM_HEADS * SSM_HEAD_DIM // SSM_STATE
    tile = lambda w, cb: pl.BlockSpec((geom["T"], w), lambda g, s, cb=cb: (idx(s), cb(g)))
    in_specs = [tile(gw, lambda g: g),
                tile(SSM_STATE, lambda g: nb_blk + g),
                tile(SSM_STATE, lambda g: nb_blk + SSM_GROUPS + g),
                tile(LANES, lambda g: dt_cb0 + g),
                pl.BlockSpec((1, 2, LANES), lambda g, s: (g, 0, 0))]
    args = [u, u, u, d1, par]
    if bwd:
        in_specs += [tile(gw, lambda g: z_cb0 + g),
                     pl.BlockSpec((1, gw), lambda g, s: (0, g)),
                     pl.BlockSpec((1, gw), lambda g, s: (0, g)),
                     tile(gw, lambda g: g)]
        args += [d1, skip, norm_w, prev]
    return pl.pallas_call(
        functools.partial(_ssd_body, bwd=bwd, geom=geom),
        grid=(SSM_GROUPS, steps),
        in_specs=in_specs,
        out_specs=tile(gw, lambda g: g),
        out_shape=jax.ShapeDtypeStruct((ntok, SSM_GROUPS * gw), bf16 if bwd else f32),
        scratch_shapes=[pltpu.VMEM((geom["B"], SSM_STATE, gw), f32)],
        compiler_params=_cp(("arbitrary", "arbitrary"), VMEM_LIMIT),
        name="ssd_bwd" if bwd else "ssd_fwd",
    )(*args)


def _ret_body(*refs, bwd, geom):
    if bwd:
        lg_ref, q_ref, k_ref, v_ref, g_ref, prev_ref, o_ref, s_all = refs
    else:
        lg_ref, q_ref, k_ref, v_ref, o_ref, s_all = refs

    lg = lg_ref[1 if bwd else 0, pl.program_id(0)]
    ii = lax.broadcasted_iota(jnp.int32, (CHUNK, CHUNK), 0).astype(f32)
    jj = lax.broadcasted_iota(jnp.int32, (CHUNK, CHUNK), 1).astype(f32)
    diff = (jj - ii) if bwd else (ii - jj)
    dmat = jnp.where(diff >= 0, jnp.exp(jnp.maximum(diff, 0.0) * lg), 0.0)
    icol = ii[:, 0:1]
    jrow = jj[0:1, :]
    if bwd:
        wq = jnp.exp((CHUNK - icol) * lg)
        ws = jnp.exp(jrow * lg)
    else:
        wq = jnp.exp((icol + 1.0) * lg)
        ws = jnp.exp((CHUNK - 1.0 - jrow) * lg)
    decay = jnp.exp(jnp.full((1, 1), float(CHUNK), f32) * lg)

    def chunk(r0, st):
        sl = pl.ds(r0, CHUNK)
        q = q_ref[sl, :]
        k = k_ref[sl, :]
        v = v_ref[sl, :]
        state = st[...]
        sc = _dot_nt(q, k) * dmat
        o = _dot(sc.astype(bf16), v) + _dot(q, state.astype(bf16)) * wq
        kt = (k.astype(f32).T * ws).astype(bf16)
        st[...] = decay * state + _dot(kt, v)
        if bwd:
            o = o + prev_ref[sl, :]
            o = o * lax.rsqrt(jnp.mean(o * o, axis=-1, keepdims=True) + NORM_EPS)
            o_ref[sl, :] = (_silu(g_ref[sl, :]) * o).astype(o_ref.dtype)
        else:
            o_ref[sl, :] = o

    _scan_driver(chunk, s_all, bwd=bwd, geom=geom)


def ret_scan(lg, qk, vb, geom, bwd, d1=None, prev=None):
    ntok = qk.shape[0]
    idx, steps = _scan_row_index(geom, bwd)
    tile = lambda w, cb: pl.BlockSpec((geom["T"], w), lambda h, s, cb=cb: (idx(s), cb(h)))
    in_specs = [pl.BlockSpec(memory_space=pltpu.SMEM),
                tile(RET_DK, lambda h: h),
                tile(RET_DK, lambda h: RET_HEADS + h),
                tile(RET_DV, lambda h: h)]
    args = [lg, qk, qk, vb]
    if bwd:
        in_specs += [tile(RET_DV, lambda h: h), tile(RET_DV, lambda h: h)]
        args += [d1, prev]
    return pl.pallas_call(
        functools.partial(_ret_body, bwd=bwd, geom=geom),
        grid=(RET_HEADS, steps),
        in_specs=in_specs,
        out_specs=tile(RET_DV, lambda h: h),
        out_shape=jax.ShapeDtypeStruct((ntok, RET_HEADS * RET_DV), bf16 if bwd else f32),
        scratch_shapes=[pltpu.VMEM((geom["B"], RET_DK, RET_DV), f32)],
        compiler_params=_cp(("arbitrary", "arbitrary")),
        name="ret_bwd" if bwd else "ret_fwd",
    )(*args)


def _att_body(sink_ref, q_ref, kc_ref, vc_ref, kp_ref, ko_ref, kn_ref, vp_ref, vo_ref, vn_ref, o_ref, *, nblk, ctx_blk):
    n = pl.program_id(1) - ctx_blk
    own_ok = n >= 0
    prev_ok = n >= 1
    next_ok = jnp.logical_and(n >= 0, n <= nblk - 2)
    rows = ATT_G * CHUNK
    qi = lax.rem(lax.broadcasted_iota(jnp.int32, (rows, CHUNK), 0), CHUNK)
    kj = lax.broadcasted_iota(jnp.int32, (rows, CHUNK), 1)
    m_prev = jnp.logical_and(kj >= qi, prev_ok)
    m_next = jnp.logical_and(kj <= qi, next_ok)
    hd = ATT_HEAD_DIM
    for kv in range(ATT_KV_HEADS):
        ks = slice(kv * hd, (kv + 1) * hd)
        kc, vc = kc_ref[:, ks], vc_ref[:, ks]
        heads = [kv * ATT_G + g for g in range(ATT_G)]
        q = jnp.concatenate([q_ref[:, h * hd:(h + 1) * hd] for h in heads], axis=0)
        sk = jnp.concatenate([jnp.full((CHUNK, 1), sink_ref[h], f32) for h in heads], axis=0)
        s_c = _dot_nt(q, kc)
        s_p = jnp.where(m_prev, _dot_nt(q, kp_ref[:, ks]), NEG_INF)
        s_o = jnp.where(own_ok, _dot_nt(q, ko_ref[:, ks]), NEG_INF)
        s_n = jnp.where(m_next, _dot_nt(q, kn_ref[:, ks]), NEG_INF)
        parts = [s_c[:, c * CHUNK:(c + 1) * CHUNK] for c in range(s_c.shape[1] // CHUNK)] + [s_p, s_o, s_n]
        mx = jnp.maximum(jnp.max(functools.reduce(jnp.maximum, parts), axis=-1, keepdims=True), sk)
        probs = [jnp.exp(t - mx) for t in parts]
        den = jnp.exp(sk - mx) + jnp.sum(functools.reduce(jnp.add, probs), axis=-1, keepdims=True)
        n_c = len(parts) - 3
        p_c = jnp.concatenate(probs[:n_c], axis=1)
        o = (_dot(p_c.astype(bf16), vc) + _dot(probs[n_c].astype(bf16), vp_ref[:, ks])
             + _dot(probs[n_c + 1].astype(bf16), vo_ref[:, ks]) + _dot(probs[n_c + 2].astype(bf16), vn_ref[:, ks]))
        o = o / den
        for g, h in enumerate(heads):
            o_ref[:, h * hd:(h + 1) * hd] = o[g * CHUNK:(g + 1) * CHUNK].astype(o_ref.dtype)


def window_attention(sink, qk, vb, geom):
    ntok = qk.shape[0]
    blk = CHUNK
    nblk = geom["S"] // blk
    ctx_blk = geom["L"] // blk
    ctx0 = geom["nx"] // blk
    kvw = ATT_KV_HEADS * ATT_HEAD_DIM
    qw = ATT_HEADS * ATT_HEAD_DIM
    kcol = qw // kvw

    def q_idx(b, i):
        return jnp.where(i < ctx_blk, ctx0 + b * ctx_blk + i, b * nblk + i - ctx_blk)

    def win(off):
        def f(b, i):
            n = jnp.clip(i - ctx_blk + off, 0, nblk - 1)
            return (b * nblk + n, kcol)
        return f

    ctx_spec = pl.BlockSpec((geom["L"], kvw), lambda b, i: (geom["nx"] // geom["L"] + b, kcol))
    wspec = lambda off: pl.BlockSpec((blk, kvw), win(off))
    return pl.pallas_call(
        functools.partial(_att_body, nblk=nblk, ctx_blk=ctx_blk),
        grid=(geom["B"], ctx_blk + nblk),
        in_specs=[pl.BlockSpec(memory_space=pltpu.SMEM),
                  pl.BlockSpec((blk, qw), lambda b, i: (q_idx(b, i), 0)),
                  ctx_spec, ctx_spec,
                  wspec(-1), wspec(0), wspec(1), wspec(-1), wspec(0), wspec(1)],
        out_specs=pl.BlockSpec((blk, qw), lambda b, i: (q_idx(b, i), 0)),
        out_shape=jax.ShapeDtypeStruct((ntok, qw), bf16),
        compiler_params=_cp(("arbitrary", "arbitrary")),
        name="window_attention",
    )(sink, qk, qk, vb, qk, qk, qk, vb, vb, vb)


def _merge_body(r_ref, s_ref, a_ref, x_ref, npre_ref, wg_ref, wb_ref, wo_ref, nw_ref, mod_ref, o_ref):
    x = x_ref[...]
    m = mod_ref[0]
    h = _norm_mod(x, npre_ref[...], m, 0, 1).astype(bf16)
    acc = None
    for n, ref in enumerate((r_ref, s_ref, a_ref)):
        gate = _dot(h, wg_ref[:, n * D_MODEL:(n + 1) * D_MODEL])
        t = _sigmoid(gate) * _dot(ref[...], wb_ref[n])
        acc = t if acc is None else acc + t
    y = _dot(acc.astype(bf16), wo_ref[...])
    o_ref[...] = x + m[2:3] * _rms(y, nw_ref[...])


def _row_tile_mod_idx(geom, tm):
    nxt = geom["nx"] // tm
    per_b = geom["S"] // tm
    return lambda i: (jnp.where(i < nxt, i // per_b, geom["B"]), 0, 0)


def _resident(shape):
    return pl.BlockSpec(shape, lambda i: (0,) * len(shape), pipeline_mode=pl.Buffered(1))


def merge(ret_o, ssd_o, att_o, x, npre, wg, wb, wo, nw, mods, geom, rows):
    d = D_MODEL
    tm = geom["tm_merge"]
    row = lambda w: pl.BlockSpec((tm, w), lambda i: (i, 0))
    return pl.pallas_call(
        _merge_body,
        grid=(rows // tm,),
        in_specs=[row(d), row(d), row(d), row(d),
                  pl.BlockSpec((1, d), lambda i: (0, 0)),
                  _resident((d, 3 * d)), _resident((3, d, d)), _resident((d, d)),
                  pl.BlockSpec((1, d), lambda i: (0, 0)),
                  pl.BlockSpec((1, 6, d), _row_tile_mod_idx(geom, tm))],
        out_specs=row(d),
        out_shape=jax.ShapeDtypeStruct((rows, d), f32),
        compiler_params=_cp(("arbitrary",), VMEM_LIMIT),
        name="merge",
    )(ret_o, ssd_o, att_o, x, npre, wg, wb, wo, nw, mods)


def _ffn_body(x_ref, npre_ref, npost_ref, mod_ref, w1_ref, w3_ref, w2_ref, o_ref, h_scr, acc_scr):
    j = pl.program_id(1)

    @pl.when(j == 0)
    def _():
        h_scr[...] = _norm_mod(x_ref[...], npre_ref[...], mod_ref[0], 3, 4).astype(bf16)
        acc_scr[...] = jnp.zeros_like(acc_scr)

    h = h_scr[...]
    act = _silu(_dot(h, w1_ref[...])) * _dot(h, w3_ref[...])
    acc_scr[...] += _dot(act.astype(bf16), w2_ref[...])

    @pl.when(j == pl.num_programs(1) - 1)
    def _():
        o_ref[...] = x_ref[...] + mod_ref[0][5:6] * _rms(acc_scr[...], npost_ref[...])


def dense_ffn(x, npre, npost, mods, w1, w3, w2, geom, rows):
    d = D_MODEL
    ff = w1.shape[1]
    tm = geom["tm"]
    tf = 256
    return pl.pallas_call(
        _ffn_body,
        grid=(rows // tm, ff // tf),
        in_specs=[pl.BlockSpec((tm, d), lambda i, j: (i, 0)),
                  pl.BlockSpec((1, d), lambda i, j: (0, 0)),
                  pl.BlockSpec((1, d), lambda i, j: (0, 0)),
                  pl.BlockSpec((1, 6, d), lambda i, j: _row_tile_mod_idx(geom, tm)(i)),
                  pl.BlockSpec((d, tf), lambda i, j: (0, j)),
                  pl.BlockSpec((d, tf), lambda i, j: (0, j)),
                  pl.BlockSpec((tf, d), lambda i, j: (j, 0))],
        out_specs=pl.BlockSpec((tm, d), lambda i, j: (i, 0)),
        out_shape=jax.ShapeDtypeStruct((rows, d), f32),
        scratch_shapes=[pltpu.VMEM((tm, d), bf16), pltpu.VMEM((tm, d), f32)],
        compiler_params=_cp(("arbitrary", "arbitrary"), VMEM_LIMIT),
        name="dense_ffn",
    )(x, npre, npost, mods, w1, w3, w2)


def _router_body(x_ref, npre_ref, mod_ref, wr_ref, h_ref, route_ref):
    h = _norm_mod(x_ref[...], npre_ref[...], mod_ref[0], 3, 4)
    h_ref[...] = h
    h_hi = h.astype(bf16)
    h_lo = (h - h_hi.astype(f32)).astype(bf16)
    wr = wr_ref[...]
    w_hi = wr.astype(bf16)
    w_lo = (wr - w_hi.astype(f32)).astype(bf16)
    logits = _dot(h_hi, w_hi) + _dot(h_hi, w_lo) + _dot(h_lo, w_hi) + _dot(h_lo, w_lo)
    lane = lax.broadcasted_iota(jnp.int32, logits.shape, 1).astype(f32)
    l1 = jnp.where(lane < N_EXPERTS, logits, -jnp.inf)
    m1 = jnp.max(l1, axis=-1, keepdims=True)
    i1 = jnp.min(jnp.where(l1 == m1, lane, float(LANES)), axis=-1, keepdims=True)
    l2 = jnp.where(lane == i1, -jnp.inf, l1)
    m2 = jnp.max(l2, axis=-1, keepdims=True)
    i2 = jnp.min(jnp.where(l2 == m2, lane, float(LANES)), axis=-1, keepdims=True)
    e = jnp.exp(m2 - m1)
    w1 = 1.0 / (1.0 + e)
    w2 = e / (1.0 + e)
    route_ref[...] = jnp.where(lane == 0.0, i1, jnp.where(lane == 1.0, i2,
                                                          jnp.where(lane == 2.0, w1, jnp.where(lane == 3.0, w2, 0.0))))


def moe_router(x, npre, mods, w_router, geom, rows):
    d = D_MODEL
    tm = geom["tm_merge"]
    wr = jnp.pad(w_router, ((0, 0), (0, LANES - w_router.shape[1])))
    return pl.pallas_call(
        _router_body,
        grid=(rows // tm,),
        in_specs=[pl.BlockSpec((tm, d), lambda i: (i, 0)),
                  pl.BlockSpec((1, d), lambda i: (0, 0)),
                  pl.BlockSpec((1, 6, d), _row_tile_mod_idx(geom, tm)),
                  pl.BlockSpec((d, LANES), lambda i: (0, 0))],
        out_specs=[pl.BlockSpec((tm, d), lambda i: (i, 0)), pl.BlockSpec((tm, LANES), lambda i: (i, 0))],
        out_shape=[jax.ShapeDtypeStruct((rows, d), f32), jax.ShapeDtypeStruct((rows, LANES), f32)],
        compiler_params=_cp(("arbitrary",)),
        name="moe_router",
    )(x, npre, mods, wr)


def _row_copy(src, src_row, dst, dst_row, sem):
    return pltpu.make_async_copy(src.at[pl.ds(src_row, 1)], dst.at[pl.ds(dst_row, 1)], sem)


def _dispatch_body(slot_ref, h_ref, xb_in, xb_hbm, sem, *, rows):
    del xb_in

    def start(r, c):
        _row_copy(h_ref, r, xb_hbm, slot_ref[0, 0, 2 * r], sem).start()
        _row_copy(h_ref, r, xb_hbm, slot_ref[0, 0, 2 * r + 1], sem).start()
        return c

    lax.fori_loop(0, rows, start, 0, unroll=DMA_ISSUE_UNROLL)
    for _ in range(2):
        pltpu.make_async_copy(h_ref, xb_hbm.at[pl.ds(0, rows)], sem).wait()


def moe_dispatch(h, slots, n_slots):
    n_tok, d = h.shape
    rows = MOE_DISPATCH_ROWS
    steps = n_tok // rows
    return pl.pallas_call(
        functools.partial(_dispatch_body, rows=rows),
        grid=(steps,),
        in_specs=[pl.BlockSpec((1, 1, 2 * rows), lambda i: (i, 0, 0), memory_space=pltpu.SMEM),
                  pl.BlockSpec((rows, d), lambda i: (i, 0)),
                  pl.BlockSpec(memory_space=pl.ANY)],
        out_specs=pl.BlockSpec(memory_space=pl.ANY),
        out_shape=jax.ShapeDtypeStruct((n_slots, d), f32),
        scratch_shapes=[pltpu.SemaphoreType.DMA(())],
        input_output_aliases={2: 0},
        compiler_params=_cp(("arbitrary",)),
        name="moe_dispatch",
    )(slots.reshape(steps, 1, 2 * rows), h, jnp.zeros((n_slots, d), f32))


def _expert_body(be_ref, cnt_ref, used_ref, x_ref, w1_ref, w3_ref, w2_ref, o_ref, *, f_split):
    del be_ref, used_ref
    i = pl.program_id(0)

    @pl.when(cnt_ref[i] > 0)
    def _():
        x = x_ref[...].astype(bf16)
        ff = w1_ref.shape[2]
        step = ff // f_split
        acc = None
        for f in range(f_split):
            fs = slice(f * step, (f + 1) * step)
            act = _silu(_dot(x, w1_ref[0, :, fs])) * _dot(x, w3_ref[0, :, fs])
            t = _dot(act.astype(bf16), w2_ref[0, fs, :])
            acc = t if acc is None else acc + t
        o_ref[...] = acc

    @pl.when(cnt_ref[i] == 0)
    def _():
        o_ref[...] = jnp.zeros_like(o_ref)


def moe_experts(xb, blk_e, blk_cnt, n_used, w1, w3, w2):
    n_slots, d = xb.shape
    ff = w1.shape[2]
    tm = MOE_BLOCK
    grid_spec = pltpu.PrefetchScalarGridSpec(
        num_scalar_prefetch=3,
        grid=(n_slots // tm,),
        in_specs=[pl.BlockSpec((tm, d), lambda i, be, cnt, used: (jnp.minimum(i, used[0] - 1), 0)),
                  pl.BlockSpec((1, d, ff), lambda i, be, cnt, used: (be[i], 0, 0)),
                  pl.BlockSpec((1, d, ff), lambda i, be, cnt, used: (be[i], 0, 0)),
                  pl.BlockSpec((1, ff, d), lambda i, be, cnt, used: (be[i], 0, 0))],
        out_specs=pl.BlockSpec((tm, d), lambda i, be, cnt, used: (i, 0)),
    )
    return pl.pallas_call(
        functools.partial(_expert_body, f_split=2),
        grid_spec=grid_spec,
        out_shape=jax.ShapeDtypeStruct((n_slots, d), f32),
        compiler_params=_cp(("arbitrary",), VMEM_LIMIT),
        name="moe_experts",
    )(blk_e, blk_cnt, n_used, xb, w1, w3, w2)


def _combine_body(slot_ref, route_ref, x_ref, npost_ref, mod_ref, yb_hbm, o_ref, y0_scr, y1_scr, sem, *, rows):
    def start(r, c):
        _row_copy(yb_hbm, slot_ref[0, 0, 2 * r], y0_scr, r, sem).start()
        _row_copy(yb_hbm, slot_ref[0, 0, 2 * r + 1], y1_scr, r, sem).start()
        return c

    lax.fori_loop(0, rows, start, 0, unroll=DMA_ISSUE_UNROLL)
    for buf in (y0_scr, y1_scr):
        pltpu.make_async_copy(yb_hbm.at[pl.ds(0, rows)], buf, sem).wait()
    y = y0_scr[...] * route_ref[:, 2:3] + y1_scr[...] * route_ref[:, 3:4]
    o_ref[...] = x_ref[...] + mod_ref[0][5:6] * _rms(y, npost_ref[...])


def moe_combine(slots, route, x, npost, mods, yb, geom, rows_total):
    d = D_MODEL
    rows = MOE_COMBINE_ROWS
    steps = rows_total // rows
    return pl.pallas_call(
        functools.partial(_combine_body, rows=rows),
        grid=(steps,),
        in_specs=[pl.BlockSpec((1, 1, 2 * rows), lambda i: (i, 0, 0), memory_space=pltpu.SMEM),
                  pl.BlockSpec((rows, LANES), lambda i: (i, 0)),
                  pl.BlockSpec((rows, d), lambda i: (i, 0)),
                  pl.BlockSpec((1, d), lambda i: (0, 0)),
                  pl.BlockSpec((1, 6, d), _row_tile_mod_idx(geom, rows)),
                  pl.BlockSpec(memory_space=pl.ANY)],
        out_specs=pl.BlockSpec((rows, d), lambda i: (i, 0)),
        out_shape=jax.ShapeDtypeStruct((rows_total, d), f32),
        scratch_shapes=[pltpu.VMEM((rows, d), f32), pltpu.VMEM((rows, d), f32), pltpu.SemaphoreType.DMA(())],
        compiler_params=_cp(("arbitrary",)),
        name="moe_combine",
    )(slots.reshape(steps, 1, 2 * rows), route, x, npost, mods, yb)


def _moe_slots(route, n_tok):
    tm = MOE_BLOCK
    flat_e = route[:, 0:2].astype(jnp.int32).reshape(2 * n_tok)
    onehot = (flat_e[:, None] == jnp.arange(N_EXPERTS, dtype=jnp.int32)[None, :]).astype(jnp.int32)
    csum = jnp.cumsum(onehot, axis=0)
    counts = csum[-1]
    rank = jnp.sum(csum * onehot, axis=1) - 1
    padded = (counts + tm - 1) // tm * tm
    pend = jnp.cumsum(padded)
    pstart = pend - padded
    slots = jnp.sum(onehot * pstart[None, :], axis=1) + rank
    n_blocks = 2 * n_tok // tm + N_EXPERTS
    blk = jnp.arange(n_blocks, dtype=jnp.int32)
    blk_e = jnp.minimum(jnp.sum((blk[:, None] >= (pend // tm)[None, :]).astype(jnp.int32), axis=1), N_EXPERTS - 1)
    n_used = (pend[-1] // tm).astype(jnp.int32)
    cnt = jnp.clip(counts[blk_e] - (blk - pstart[blk_e] // tm) * tm, 0, tm)
    blk_cnt = jnp.where(blk < n_used, cnt, 0).astype(jnp.int32)
    return slots.astype(jnp.int32), blk_e, blk_cnt, n_used.reshape(1), n_blocks * tm


def moe_ffn(x, npre, npost, mods, w_router, w1, w3, w2, geom, rows):
    h, route = moe_router(x, npre, mods, w_router, geom, rows)
    slots, blk_e, blk_cnt, n_used, n_slots = _moe_slots(route, rows)
    xb = moe_dispatch(h, slots, n_slots)
    yb = moe_experts(xb, blk_e, blk_cnt, n_used, w1, w3, w2)
    return moe_combine(slots, route, x, npost, mods, yb, geom, rows)


def _rope_tables(geom):
    s, tm = geom["S"], geom["tm"]
    pos = jnp.arange(s)

    def angles(p, dim):
        inv = ROPE_BASE ** (-jnp.arange(0, dim, 2, dtype=f32) / dim)
        return p.astype(f32)[:, None] * inv[None, :]

    def with_identity(t, one):
        return jnp.concatenate([t, jnp.full((tm, LANES), one, f32)], axis=0)

    a = angles(pos, RET_DK)
    ret_cos = jnp.concatenate([jnp.cos(a), jnp.cos(a)], axis=1)
    ret_sin = jnp.concatenate([-jnp.sin(a), jnp.sin(a)], axis=1)
    half = ATT_HEAD_DIM // 2
    ar = angles(pos // GRID_W, half)
    ac = angles(pos % GRID_W, half)
    z = jnp.zeros_like(ar)
    att_cos = jnp.concatenate([jnp.cos(ar), jnp.cos(ar), jnp.cos(ac), jnp.cos(ac)], axis=1)
    att_s_up = jnp.concatenate([-jnp.sin(ar), z, -jnp.sin(ac), z], axis=1)
    att_s_dn = jnp.concatenate([z, jnp.sin(ar), z, jnp.sin(ac)], axis=1)
    ret = ((RET_DK // 2,), with_identity(ret_cos, 1.0), (with_identity(ret_sin, 0.0),))
    att = ((LANES - half // 2, half // 2), with_identity(att_cos, 1.0),
           (with_identity(att_s_up, 0.0), with_identity(att_s_dn, 0.0)))
    return ret, att


def _pack_w_in(w):
    rq, rk, rv, rg, sz, sxbc, sdt, aq, ak, av, bg = jnp.split(w, np.cumsum(IN_SPLITS)[:-1].tolist(), axis=1)
    pad = lambda t, n: jnp.pad(t, ((0, 0), (0, n - t.shape[1])))
    w_a = jnp.concatenate([rq, rk], axis=1)
    w_b = jnp.concatenate([rv, av], axis=1)
    w_c = jnp.concatenate([aq, ak], axis=1)
    w_d1 = jnp.concatenate([rg, sz, sxbc, pad(sdt[:, :SSM_HG], LANES), pad(sdt[:, SSM_HG:], 3 * LANES)], axis=1)
    return tuple(t.astype(bf16) for t in (w_a, w_b, w_c, w_d1, bg))


def _pad_heads(v):
    return jnp.pad(v.astype(f32).reshape(2, SSM_GROUPS, SSM_HG), ((0, 0), (0, 0), (0, LANES - SSM_HG)))


def _geometry(batch, seq, ctx_len):
    tm = 1024
    while seq % tm or (batch * ctx_len) % tm:
        tm //= 2
    tm_merge = min(tm, 512)
    scan_tile = batch * ctx_len
    assert seq % scan_tile == 0 and scan_tile % (2 * CHUNK) == 0 and ctx_len % CHUNK == 0 and seq % GRID_W == 0
    return {"B": batch, "S": seq, "L": ctx_len, "nx": batch * seq, "tm": tm, "tm_merge": tm_merge, "T": scan_tile}


def kernel(x, c, ctx, c_ctx, ada_w, ada_b, mix_norm_pre, mix_norm_post, ffn_norm_pre, ffn_norm_post, w_in, ret_decay, ssm_conv_w, ssm_conv_b, ssm_dt_bias, ssm_a_log, ssm_d, ssm_norm_w, att_sink, w_branch, w_out, ffn_w1, ffn_w3, ffn_w2, moe_router, moe_w1, moe_w3, moe_w2):
    batch, seq, d = x.shape
    ctx_len = ctx.shape[1]
    depth = ada_w.shape[0]
    geom = _geometry(batch, seq, ctx_len)
    nx = geom["nx"]
    ntok = nx + batch * ctx_len

    stream = jnp.concatenate([x.reshape(nx, d), ctx.reshape(batch * ctx_len, d)], axis=0)
    cond = jnp.concatenate([c, c_ctx[None, :], jnp.zeros((SUBLANES - 1 - batch % SUBLANES, d), f32)], axis=0)
    mods_all = adaln_all(cond, ada_w, ada_b).reshape(depth, cond.shape[0], 6, d)
    rope_ret, rope_att = _rope_tables(geom)
    ret_cs = jnp.concatenate([jnp.ones((1, RET_HEADS * RET_DK), f32),
                              jnp.full((1, RET_HEADS * RET_DK), RET_DK ** -0.5, f32)], axis=1)
    att_cs = jnp.concatenate([jnp.full((1, ATT_HEADS * ATT_HEAD_DIM), ATT_HEAD_DIM ** -0.5, f32),
                              jnp.ones((1, ATT_KV_HEADS * ATT_HEAD_DIM), f32)], axis=1)
    row = lambda v: v.reshape(1, -1)
    dt_cb0 = (2 * D_MODEL + IN_SPLITS[5]) // LANES
    z_cb0 = D_MODEL // SSM_GW
    conv_col0 = 2 * D_MODEL

    for i in range(depth):
        last = i == depth - 1
        rows = nx if last else ntok
        mods = mods_all[i]
        w_a, w_b, w_c, w_d1, w_gate = _pack_w_in(w_in[i])
        npre = row(mix_norm_pre[i])
        proj = functools.partial(in_proj, stream, npre, mods, geom=geom)
        ret_qk = proj(w=w_a, out_dtype=bf16, tn=512, rope=(rope_ret[0], ret_cs, rope_ret[1], rope_ret[2]))
        val = proj(w=w_b, out_dtype=bf16, tn=640)
        att_qk = proj(w=w_c, out_dtype=bf16, tn=640, rope=(rope_att[0], att_cs, rope_att[1], rope_att[2]))
        d1 = proj(w=w_d1, out_dtype=f32, tn=1024)

        lg = jax.nn.log_sigmoid(ret_decay[i].astype(f32))
        ret_f = ret_scan(lg, ret_qk, val, geom, bwd=False)
        ret_o = ret_scan(lg, ret_qk, val, geom, bwd=True, d1=d1, prev=ret_f)

        u = ssd_conv(d1, ssm_conv_w[i], ssm_conv_b[i], geom, conv_col0)
        par = jnp.stack([_pad_heads(ssm_dt_bias[i]), _pad_heads(-jnp.exp(ssm_a_log[i].astype(f32)))], axis=2)
        skip = row(jnp.repeat(ssm_d[i, 0].astype(f32) + ssm_d[i, 1].astype(f32), SSM_HEAD_DIM))
        ssd_f = ssd_scan(u, d1, par[0], geom, dt_cb0, bwd=False)
        ssd_o = ssd_scan(u, d1, par[1], geom, dt_cb0, bwd=True, z_cb0=z_cb0, skip=skip,
                         norm_w=row(ssm_norm_w[i]), prev=ssd_f)

        att_o = window_attention(att_sink[i].astype(f32), att_qk, val, geom)

        stream = merge(ret_o, ssd_o, att_o, stream, npre, w_gate, w_branch[i].astype(bf16), w_out[i].astype(bf16),
                       row(mix_norm_post[i]), mods, geom, rows)

        j = i // 2
        fpre, fpost = row(ffn_norm_pre[i]), row(ffn_norm_post[i])
        if i % 2 == 0:
            stream = dense_ffn(stream, fpre, fpost, mods, ffn_w1[j].astype(bf16), ffn_w3[j].astype(bf16),
                               ffn_w2[j].astype(bf16), geom, rows)
        else:
            stream = moe_ffn(stream, fpre, fpost, mods, moe_router[j], moe_w1[j].astype(bf16),
                             moe_w3[j].astype(bf16), moe_w2[j].astype(bf16), geom, rows)
    return stream[:nx].reshape(batch, seq, d)
```

```python
import functools
import math

import jax
import jax.numpy as jnp
import numpy as np
from jax import lax
from jax.experimental import pallas as pl
from jax.experimental.pallas import tpu as pltpu

f32 = jnp.float32
bf16 = jnp.bfloat16

D_MODEL = 1024
GRID_W = 64
CHUNK = 128
NORM_EPS = 1e-6
ROPE_BASE = 10000.0
NEG_INF = -1e30
RET_HEADS, RET_DK, RET_DV = 4, 128, 256
SSM_HEADS, SSM_HEAD_DIM, SSM_GROUPS, SSM_STATE = 16, 64, 2, 128
SSM_HG = SSM_HEADS // SSM_GROUPS
SSM_GW = SSM_HG * SSM_HEAD_DIM
ATT_HEADS, ATT_KV_HEADS, ATT_HEAD_DIM = 8, 2, 128
ATT_G = ATT_HEADS // ATT_KV_HEADS
D_FF = 2816
N_EXPERTS = 8
IN_SPLITS = (512, 512, 1024, 1024, 1024, 1536, 16, 1024, 256, 256, 3072)

LANES = 128
SUBLANES = 8
MOE_BLOCK = 256
MOE_DISPATCH_ROWS = 256
MOE_COMBINE_ROWS = 256
VMEM_LIMIT = 56 * 2 ** 20
DMA_ISSUE_UNROLL = 8
FFN_SPLIT = 2


def _cp(sem, vmem=None):
    return pltpu.CompilerParams(dimension_semantics=sem, vmem_limit_bytes=vmem)


def _silu(v):
    return v / (1.0 + jnp.exp(-v))


def _sigmoid(v):
    return 1.0 / (1.0 + jnp.exp(-v))


def _rms(v, w):
    return v * lax.rsqrt(jnp.mean(v * v, axis=-1, keepdims=True) + NORM_EPS) * w


def _norm_mod(x, nw, m, shift_row, scale_row):
    return _rms(x, nw) * (1.0 + m[scale_row:scale_row + 1]) + m[shift_row:shift_row + 1]


def _dot(a, b):
    return jnp.dot(a, b, preferred_element_type=f32)


def _dot_nt(a, b):
    return lax.dot_general(a, b, (((1,), (1,)), ((), ())), preferred_element_type=f32)


def _split3(a):
    hi = a.astype(bf16)
    r = a - hi.astype(f32)
    mid = r.astype(bf16)
    lo = (r - mid.astype(f32)).astype(bf16)
    return hi, mid, lo


def _row_tile_mod_idx(geom, tm):
    nxt = geom["nx"] // tm
    per_b = geom["S"] // tm
    return lambda i: (jnp.where(i < nxt, i // per_b, geom["B"]), 0, 0)


def _resident(shape):
    return pl.BlockSpec(shape, lambda i: (0,) * len(shape), pipeline_mode=pl.Buffered(1))


def _adaln_body(c_ref, w_ref, b_ref, o_ref):
    s = _silu(c_ref[...])
    o_ref[0] = _dot(s.astype(bf16), w_ref[0].astype(bf16)) + b_ref[0]


def adaln_all(cond, ada_w, ada_b):
    depth, d, n = ada_w.shape
    rows = cond.shape[0]
    tn = 1024
    return pl.pallas_call(
        _adaln_body,
        grid=(depth, n // tn),
        in_specs=[pl.BlockSpec((rows, d), lambda l, j: (0, 0)),
                  pl.BlockSpec((1, d, tn), lambda l, j: (l, 0, j)),
                  pl.BlockSpec((1, 1, tn), lambda l, j: (l, 0, j))],
        out_specs=pl.BlockSpec((1, rows, tn), lambda l, j: (l, 0, j)),
        out_shape=jax.ShapeDtypeStruct((depth, rows, n), f32),
        compiler_params=_cp(("arbitrary", "arbitrary")),
        name="adaln",
    )(cond, ada_w, ada_b.reshape(depth, 1, n))


PROJ_CHUNK = 512


def _proj_plan():
    rq, rk = RET_HEADS * RET_DK, RET_HEADS * RET_DK
    aq, ak = ATT_HEADS * ATT_HEAD_DIM, ATT_KV_HEADS * ATT_HEAD_DIM
    groups = [(0, rq, "ret", 1.0), (0, rk, "ret", RET_DK ** -0.5),
              (1, aq, "att", ATT_HEAD_DIM ** -0.5), (1, ak, "att", 1.0),
              (2, RET_HEADS * RET_DV + ak, None, 1.0),
              (3, 4 * D_MODEL, None, 1.0)]
    plan, wcol, ocol = [], 0, {}
    for out, width, kind, scale in groups:
        done = 0
        while done < width:
            step = min(PROJ_CHUNK, width - done)
            plan.append((wcol, step, out, ocol.get(out, 0), kind, scale))
            wcol += step
            ocol[out] = ocol.get(out, 0) + step
            done += step
    return plan, wcol, [ocol[o] for o in range(4)]


def _proj_body(x_ref, nw_ref, mod_ref, w_ref, rcos_ref, rsin_ref, acos_ref, aup_ref, adn_ref,
               o_ret, o_att, o_val, o_d1, *, plan, ret_shift, att_shifts):
    outs = (o_ret, o_att, o_val, o_d1)
    h = _norm_mod(x_ref[...], nw_ref[...], mod_ref[0], 0, 1).astype(bf16)
    for wcol, width, out, ocol, kind, scale in plan:
        acc = _dot(h, w_ref[:, wcol:wcol + width])
        o_ref = outs[out]
        if kind is None:
            o_ref[:, ocol:ocol + width] = acc.astype(o_ref.dtype)
            continue
        for c in range(width // LANES):
            t = acc[:, c * LANES:(c + 1) * LANES]
            if kind == "ret":
                r = t * rcos_ref[...] + pltpu.roll(t, ret_shift, 1) * rsin_ref[...]
            else:
                r = (t * acos_ref[...] + pltpu.roll(t, att_shifts[0], 1) * aup_ref[...]
                     + pltpu.roll(t, att_shifts[1], 1) * adn_ref[...])
            o_ref[:, ocol + c * LANES:ocol + (c + 1) * LANES] = (r * scale).astype(o_ref.dtype)


def in_proj(x, nw, mods, w, rope_ret, rope_att, geom):
    ntok, d = x.shape
    tm = geom["tm_merge"]
    plan, wcols, widths = _proj_plan()
    assert w.shape == (d, wcols)
    nxt = geom["nx"] // tm
    per_b = geom["S"] // tm
    pos_idx = lambda i: (jnp.where(i < nxt, i % per_b, per_b), 0)
    table = pl.BlockSpec((tm, LANES), pos_idx)
    dtypes = (bf16, bf16, bf16, f32)
    return pl.pallas_call(
        functools.partial(_proj_body, plan=plan, ret_shift=rope_ret[0][0], att_shifts=rope_att[0]),
        grid=(ntok // tm,),
        in_specs=[pl.BlockSpec((tm, d), lambda i: (i, 0)),
                  pl.BlockSpec((1, d), lambda i: (0, 0)),
                  pl.BlockSpec((1, 6, d), _row_tile_mod_idx(geom, tm)),
                  _resident((d, wcols)),
                  table, table, table, table, table],
        out_specs=[pl.BlockSpec((tm, n), lambda i: (i, 0)) for n in widths],
        out_shape=[jax.ShapeDtypeStruct((ntok, n), dt) for n, dt in zip(widths, dtypes)],
        compiler_params=_cp(("arbitrary",), VMEM_LIMIT),
        name="in_proj",
    )(x, nw, mods, w, rope_ret[1], rope_ret[2][0], rope_att[1], rope_att[2][0], rope_att[2][1])


def _conv_body(x_ref, prev_ref, next_ref, w_ref, b_ref, o_ref, *, tiles_per_seq, nx_tiles, ctx_len):
    i = pl.program_id(0)
    x = x_ref[...]
    rows = x.shape[0]
    t_in = i % tiles_per_seq
    is_x = i < nx_tiles
    use_prev = jnp.logical_and(is_x, t_in > 0)
    use_next = jnp.logical_and(is_x, t_in < tiles_per_seq - 1)
    prow = jnp.where(use_prev, prev_ref[SUBLANES - 1:SUBLANES, :], 0.0)
    nrow = jnp.where(use_next, next_ref[0:1, :], 0.0)
    rid = lax.broadcasted_iota(jnp.int32, x.shape, 0)
    xm = jnp.where(rid == 0, prow, pltpu.roll(x, 1, 0))
    xp = jnp.where(rid == rows - 1, nrow, pltpu.roll(x, rows - 1, 0))
    seg = lax.rem(rid, ctx_len)
    is_ctx = jnp.logical_not(is_x)
    xm = jnp.where(jnp.logical_and(is_ctx, seg == 0), 0.0, xm)
    xp = jnp.where(jnp.logical_and(is_ctx, seg == ctx_len - 1), 0.0, xp)
    w = w_ref[...]
    o_ref[...] = _silu(xm * w[0:1] + x * w[1:2] + xp * w[2:3] + b_ref[...])


def ssd_conv(d1, conv_w, conv_b, geom, col0):
    ntok = d1.shape[0]
    width = conv_w.shape[1]
    tc = 512
    cb0 = col0 // tc
    tr = geom["T"]
    per8 = tr // SUBLANES
    last8 = ntok // SUBLANES - 1
    return pl.pallas_call(
        functools.partial(_conv_body, tiles_per_seq=geom["S"] // tr, nx_tiles=geom["nx"] // tr, ctx_len=geom["L"]),
        grid=(ntok // tr, width // tc),
        in_specs=[pl.BlockSpec((tr, tc), lambda i, j: (i, cb0 + j)),
                  pl.BlockSpec((SUBLANES, tc), lambda i, j: (jnp.maximum(i * per8 - 1, 0), cb0 + j)),
                  pl.BlockSpec((SUBLANES, tc), lambda i, j: (jnp.minimum((i + 1) * per8, last8), cb0 + j)),
                  pl.BlockSpec((3, tc), lambda i, j: (0, j)),
                  pl.BlockSpec((1, tc), lambda i, j: (0, j))],
        out_specs=pl.BlockSpec((tr, tc), lambda i, j: (i, j)),
        out_shape=jax.ShapeDtypeStruct((ntok, width), f32),
        compiler_params=_cp(("arbitrary", "arbitrary")),
        name="ssd_conv",
    )(d1, d1, d1, conv_w, conv_b.reshape(1, width))


def _scan_row_index(geom, bwd):
    tile = geom["T"]
    per_b = geom["S"] // tile
    ctx_tile = geom["nx"] // tile

    def idx(s):
        t = s - 1
        xt = (t // per_b) * per_b + (per_b - 1 - t % per_b) if bwd else t
        return jnp.where(s == 0, ctx_tile, xt)

    return idx, ctx_tile + 1


def _scan_driver(chunk, s_all, *, bwd, geom, group):
    tile, ctx_len, nbatch = geom["T"], geom["L"], geom["B"]
    per_b = geom["S"] // tile
    s = pl.program_id(1)

    @pl.when(s == 0)
    def _():
        for b in range(nbatch):
            state = jnp.zeros(s_all.shape[1:], f32)
            cs = range(ctx_len // CHUNK)
            for ci in (reversed(cs) if bwd else cs):
                state = chunk(b * ctx_len + ci * CHUNK, state)
            s_all[b] = state

    @pl.when(s > 0)
    def _():
        st = s_all.at[(s - 1) // per_b]
        n_groups = tile // (group * CHUNK)
        order = tuple(reversed(range(group))) if bwd else tuple(range(group))

        def body(p, state):
            base = ((n_groups - 1 - p) if bwd else p) * (group * CHUNK)
            if not isinstance(base, int):
                base = pl.multiple_of(base, group * CHUNK)
            for ci in order:
                state = chunk(base + ci * CHUNK, state)
            return state

        st[...] = body(0, st[...]) if n_groups == 1 else lax.fori_loop(0, n_groups, body, st[...])


def _expand_heads(a, lane):
    rows = a.shape[0]
    lane = lane[:rows]
    cols = []
    for v in range(SSM_HG // 2):
        left = jnp.broadcast_to(a[:, 2 * v:2 * v + 1], (rows, LANES))
        right = jnp.broadcast_to(a[:, 2 * v + 1:2 * v + 2], (rows, LANES))
        cols.append(jnp.where(lane < SSM_HEAD_DIM, left, right))
    return jnp.concatenate(cols, axis=1)


def _ssd_body(*refs, bwd, geom):
    if bwd:
        xs_ref, b_ref, c_ref, dt_ref, par_ref, z_ref, skip_ref, nw_ref, prev_ref, o_ref, s_all = refs
    else:
        xs_ref, b_ref, c_ref, dt_ref, par_ref, o_ref, s_all = refs

    par = par_ref[0]
    ii = lax.broadcasted_iota(jnp.int32, (CHUNK, CHUNK), 0)
    jj = lax.broadcasted_iota(jnp.int32, (CHUNK, CHUNK), 1)
    tri_mask = (ii <= jj) if bwd else (ii >= jj)
    tri = jnp.where(tri_mask, 1.0, 0.0).astype(bf16)
    lane = jj
    last = 0 if bwd else CHUNK - 1
    pair_keep = [jnp.where((lane < SSM_HEAD_DIM) == (a == 0), 1.0, 0.0).astype(bf16) for a in range(2)]

    def chunk(r0, state):
        sl = pl.ds(r0, CHUNK)
        pre = dt_ref[sl, :] + par[0:1]
        dt = jnp.maximum(pre, 0.0) + jnp.log(1.0 + jnp.exp(-jnp.abs(pre)))
        hi, mid, lo = _split3(dt * par[1:2])
        cum = _dot(tri, hi) + _dot(tri, mid) + _dot(tri, lo)
        tot = cum[last:last + 1]
        cum_t = cum.T[0:SUBLANES]
        dt_t = dt.T[0:SUBLANES]
        dtws_t = dt_t * jnp.exp(cum_t[:, last:last + 1] - cum_t)
        dec_x = _expand_heads(jnp.broadcast_to(jnp.exp(tot), (SUBLANES, LANES)), lane)[0:1]

        xs = xs_ref[sl, :]
        xs_b = xs.astype(bf16)
        bm = b_ref[sl, :]
        cm = c_ref[sl, :]
        cb = _dot_nt(cm.astype(bf16), bm.astype(bf16))
        bm_t = bm.T
        state_b = state.astype(bf16)
        y_parts, upd_parts = [], []
        for v in range(SSM_HG // 2):
            ps = slice(v * LANES, (v + 1) * LANES)
            y_p = upd_p = None
            for a in range(2):
                h = 2 * v + a
                xs_h = xs_b[:, ps] * pair_keep[a]
                st_h = state_b[:, ps] * pair_keep[a]
                cbc = jnp.broadcast_to(cum[:, h:h + 1], (CHUNK, CHUNK))
                lmat = jnp.exp(jnp.where(tri_mask, cbc - cum_t[h:h + 1, :], NEG_INF))
                intra = (cb * lmat * dt_t[h:h + 1, :]).astype(bf16)
                inter = (cm * jnp.exp(cbc)).astype(bf16)
                t = _dot(intra, xs_h) + _dot(inter, st_h)
                y_p = t if y_p is None else y_p + t
                t = _dot((bm_t * dtws_t[h:h + 1, :]).astype(bf16), xs_h)
                upd_p = t if upd_p is None else upd_p + t
            y_parts.append(y_p)
            upd_parts.append(upd_p)
        y = jnp.concatenate(y_parts, axis=1)
        if bwd:
            y = y + prev_ref[sl, :] + xs * skip_ref[...]
            gz = y * _silu(z_ref[sl, :])
            o_ref[sl, :] = _rms(gz, nw_ref[...]).astype(o_ref.dtype)
        else:
            o_ref[sl, :] = y
        return dec_x * state + jnp.concatenate(upd_parts, axis=1)

    _scan_driver(chunk, s_all, bwd=bwd, geom=geom, group=2)


def ssd_scan(u, d1, par, geom, dt_cb0, bwd, z_cb0=None, skip=None, norm_w=None, prev=None):
    ntok = u.shape[0]
    idx, steps = _scan_row_index(geom, bwd)
    gw = SSM_GW
    nb_blk = SSM_HEADS * SSM_HEAD_DIM // SSM_STATE
    tile = lambda w, cb: pl.BlockSpec((geom["T"], w), lambda g, s, cb=cb: (idx(s), cb(g)))
    in_specs = [tile(gw, lambda g: g),
                tile(SSM_STATE, lambda g: nb_blk + g),
                tile(SSM_STATE, lambda g: nb_blk + SSM_GROUPS + g),
                tile(LANES, lambda g: dt_cb0 + g),
                pl.BlockSpec((1, 2, LANES), lambda g, s: (g, 0, 0))]
    args = [u, u, u, d1, par]
    if bwd:
        in_specs += [tile(gw, lambda g: z_cb0 + g),
                     pl.BlockSpec((1, gw), lambda g, s: (0, g)),
                     pl.BlockSpec((1, gw), lambda g, s: (0, g)),
                     tile(gw, lambda g: g)]
        args += [d1, skip, norm_w, prev]
    return pl.pallas_call(
        functools.partial(_ssd_body, bwd=bwd, geom=geom),
        grid=(SSM_GROUPS, steps),
        in_specs=in_specs,
        out_specs=tile(gw, lambda g: g),
        out_shape=jax.ShapeDtypeStruct((ntok, SSM_GROUPS * gw), bf16 if bwd else f32),
        scratch_shapes=[pltpu.VMEM((geom["B"], SSM_STATE, gw), f32)],
        compiler_params=_cp(("arbitrary", "arbitrary"), VMEM_LIMIT),
        name="ssd_bwd" if bwd else "ssd_fwd",
    )(*args)


def _ret_body(*refs, bwd, geom):
    if bwd:
        lg_ref, q_ref, k_ref, v_ref, g_ref, prev_ref, o_ref, s_all = refs
    else:
        lg_ref, q_ref, k_ref, v_ref, o_ref, s_all = refs

    lg = lg_ref[1 if bwd else 0, pl.program_id(0)]
    ii = lax.broadcasted_iota(jnp.int32, (CHUNK, CHUNK), 0).astype(f32)
    jj = lax.broadcasted_iota(jnp.int32, (CHUNK, CHUNK), 1).astype(f32)
    diff = (jj - ii) if bwd else (ii - jj)
    dmat = jnp.where(diff >= 0, jnp.exp(jnp.maximum(diff, 0.0) * lg), 0.0)
    icol = ii[:, 0:1]
    jrow = jj[0:1, :]
    if bwd:
        wq = jnp.exp((CHUNK - icol) * lg)
        ws = jnp.exp(jrow * lg)
    else:
        wq = jnp.exp((icol + 1.0) * lg)
        ws = jnp.exp((CHUNK - 1.0 - jrow) * lg)
    decay = jnp.exp(jnp.full((1, 1), float(CHUNK), f32) * lg)

    def chunk(r0, state):
        sl = pl.ds(r0, CHUNK)
        q = q_ref[sl, :]
        k = k_ref[sl, :]
        v = v_ref[sl, :]
        sc = _dot_nt(q, k) * dmat
        o = _dot(sc.astype(bf16), v) + _dot(q, state.astype(bf16)) * wq
        kt = (k.astype(f32).T * ws).astype(bf16)
        if bwd:
            o = o + prev_ref[sl, :]
            o = o * lax.rsqrt(jnp.mean(o * o, axis=-1, keepdims=True) + NORM_EPS)
            o_ref[sl, :] = (_silu(g_ref[sl, :]) * o).astype(o_ref.dtype)
        else:
            o_ref[sl, :] = o
        return decay * state + _dot(kt, v)

    _scan_driver(chunk, s_all, bwd=bwd, geom=geom, group=geom["T"] // CHUNK)


def ret_scan(lg, qk, vb, geom, bwd, d1=None, prev=None):
    ntok = qk.shape[0]
    idx, steps = _scan_row_index(geom, bwd)
    tile = lambda w, cb: pl.BlockSpec((geom["T"], w), lambda h, s, cb=cb: (idx(s), cb(h)))
    in_specs = [pl.BlockSpec(memory_space=pltpu.SMEM),
                tile(RET_DK, lambda h: h),
                tile(RET_DK, lambda h: RET_HEADS + h),
                tile(RET_DV, lambda h: h)]
    args = [lg, qk, qk, vb]
    if bwd:
        in_specs += [tile(RET_DV, lambda h: h), tile(RET_DV, lambda h: h)]
        args += [d1, prev]
    return pl.pallas_call(
        functools.partial(_ret_body, bwd=bwd, geom=geom),
        grid=(RET_HEADS, steps),
        in_specs=in_specs,
        out_specs=tile(RET_DV, lambda h: h),
        out_shape=jax.ShapeDtypeStruct((ntok, RET_HEADS * RET_DV), bf16 if bwd else f32),
        scratch_shapes=[pltpu.VMEM((geom["B"], RET_DK, RET_DV), f32)],
        compiler_params=_cp(("arbitrary", "arbitrary")),
        name="ret_bwd" if bwd else "ret_fwd",
    )(*args)


def _att_body(sink_ref, q_ref, kc_ref, vc_ref, kp_ref, ko_ref, kn_ref, vp_ref, vo_ref, vn_ref, o_ref, *, nblk, ctx_blk):
    n = pl.program_id(1) - ctx_blk
    own_ok = n >= 0
    prev_ok = n >= 1
    next_ok = jnp.logical_and(n >= 0, n <= nblk - 2)
    rows = ATT_G * CHUNK
    qi = lax.rem(lax.broadcasted_iota(jnp.int32, (rows, CHUNK), 0), CHUNK)
    kj = lax.broadcasted_iota(jnp.int32, (rows, CHUNK), 1)
    m_prev = jnp.logical_and(kj >= qi, prev_ok)
    m_next = jnp.logical_and(kj <= qi, next_ok)
    hd = ATT_HEAD_DIM
    for kv in range(ATT_KV_HEADS):
        ks = slice(kv * hd, (kv + 1) * hd)
        kc, vc = kc_ref[:, ks], vc_ref[:, ks]
        heads = [kv * ATT_G + g for g in range(ATT_G)]
        q = jnp.concatenate([q_ref[:, h * hd:(h + 1) * hd] for h in heads], axis=0)
        sk = jnp.concatenate([jnp.full((CHUNK, 1), sink_ref[h], f32) for h in heads], axis=0)
        s_c = _dot_nt(q, kc)
        s_p = jnp.where(m_prev, _dot_nt(q, kp_ref[:, ks]), NEG_INF)
        s_o = jnp.where(own_ok, _dot_nt(q, ko_ref[:, ks]), NEG_INF)
        s_n = jnp.where(m_next, _dot_nt(q, kn_ref[:, ks]), NEG_INF)
        parts = [s_c[:, c * CHUNK:(c + 1) * CHUNK] for c in range(s_c.shape[1] // CHUNK)] + [s_p, s_o, s_n]
        mx = jnp.maximum(jnp.max(functools.reduce(jnp.maximum, parts), axis=-1, keepdims=True), sk)
        probs = [jnp.exp(t - mx) for t in parts]
        den = jnp.exp(sk - mx) + jnp.sum(functools.reduce(jnp.add, probs), axis=-1, keepdims=True)
        n_c = len(parts) - 3
        p_c = jnp.concatenate(probs[:n_c], axis=1)
        o = (_dot(p_c.astype(bf16), vc) + _dot(probs[n_c].astype(bf16), vp_ref[:, ks])
             + _dot(probs[n_c + 1].astype(bf16), vo_ref[:, ks]) + _dot(probs[n_c + 2].astype(bf16), vn_ref[:, ks]))
        o = o / den
        for g, h in enumerate(heads):
            o_ref[:, h * hd:(h + 1) * hd] = o[g * CHUNK:(g + 1) * CHUNK].astype(o_ref.dtype)


def window_attention(sink, qk, vb, geom):
    ntok = qk.shape[0]
    blk = CHUNK
    nblk = geom["S"] // blk
    ctx_blk = geom["L"] // blk
    ctx0 = geom["nx"] // blk
    kvw = ATT_KV_HEADS * ATT_HEAD_DIM
    qw = ATT_HEADS * ATT_HEAD_DIM
    kcol = qw // kvw

    def q_idx(b, i):
        return jnp.where(i < ctx_blk, ctx0 + b * ctx_blk + i, b * nblk + i - ctx_blk)

    def win(off):
        def f(b, i):
            n = jnp.clip(i - ctx_blk + off, 0, nblk - 1)
            return (b * nblk + n, kcol)
        return f

    ctx_spec = pl.BlockSpec((geom["L"], kvw), lambda b, i: (geom["nx"] // geom["L"] + b, kcol))
    wspec = lambda off: pl.BlockSpec((blk, kvw), win(off))
    return pl.pallas_call(
        functools.partial(_att_body, nblk=nblk, ctx_blk=ctx_blk),
        grid=(geom["B"], ctx_blk + nblk),
        in_specs=[pl.BlockSpec(memory_space=pltpu.SMEM),
                  pl.BlockSpec((blk, qw), lambda b, i: (q_idx(b, i), 0)),
                  ctx_spec, ctx_spec,
                  wspec(-1), wspec(0), wspec(1), wspec(-1), wspec(0), wspec(1)],
        out_specs=pl.BlockSpec((blk, qw), lambda b, i: (q_idx(b, i), 0)),
        out_shape=jax.ShapeDtypeStruct((ntok, qw), bf16),
        compiler_params=_cp(("arbitrary", "arbitrary")),
        name="window_attention",
    )(sink, qk, qk, vb, qk, qk, qk, vb, vb, vb)


def _merge_body(r_ref, s_ref, a_ref, x_ref, npre_ref, wg_ref, wb_ref, wo_ref, nw_ref, mod_ref, o_ref):
    x = x_ref[...]
    m = mod_ref[0]
    h = _norm_mod(x, npre_ref[...], m, 0, 1).astype(bf16)
    acc = None
    for n, ref in enumerate((r_ref, s_ref, a_ref)):
        gate = _dot(h, wg_ref[:, n * D_MODEL:(n + 1) * D_MODEL])
        t = _sigmoid(gate) * _dot(ref[...], wb_ref[n])
        acc = t if acc is None else acc + t
    y = _dot(acc.astype(bf16), wo_ref[...])
    o_ref[...] = x + m[2:3] * _rms(y, nw_ref[...])


def merge(ret_o, ssd_o, att_o, x, npre, wg, wb, wo, nw, mods, geom, rows):
    d = D_MODEL
    tm = geom["tm_merge"]
    row = lambda w: pl.BlockSpec((tm, w), lambda i: (i, 0))
    return pl.pallas_call(
        _merge_body,
        grid=(rows // tm,),
        in_specs=[row(d), row(d), row(d), row(d),
                  pl.BlockSpec((1, d), lambda i: (0, 0)),
                  _resident((d, 3 * d)), _resident((3, d, d)), _resident((d, d)),
                  pl.BlockSpec((1, d), lambda i: (0, 0)),
                  pl.BlockSpec((1, 6, d), _row_tile_mod_idx(geom, tm))],
        out_specs=row(d),
        out_shape=jax.ShapeDtypeStruct((rows, d), f32),
        compiler_params=_cp(("arbitrary",), VMEM_LIMIT),
        name="merge",
    )(ret_o, ssd_o, att_o, x, npre, wg, wb, wo, nw, mods)


def _swiglu(h, w1, w3, w2, f_split):
    step = w1.shape[1] // f_split
    acc = None
    for f in range(f_split):
        fs = slice(f * step, (f + 1) * step)
        act = _silu(_dot(h, w1[:, fs])) * _dot(h, w3[:, fs])
        t = _dot(act.astype(bf16), w2[fs, :])
        acc = t if acc is None else acc + t
    return acc


def _ffn_body(x_ref, npre_ref, npost_ref, mod_ref, w1_ref, w3_ref, w2_ref, o_ref):
    x = x_ref[...]
    m = mod_ref[0]
    h = _norm_mod(x, npre_ref[...], m, 3, 4).astype(bf16)
    y = _swiglu(h, w1_ref, w3_ref, w2_ref, FFN_SPLIT)
    o_ref[...] = x + m[5:6] * _rms(y, npost_ref[...])


def dense_ffn(x, npre, npost, mods, w1, w3, w2, geom, rows):
    d = D_MODEL
    ff = w1.shape[1]
    tm = geom["tm_merge"]
    return pl.pallas_call(
        _ffn_body,
        grid=(rows // tm,),
        in_specs=[pl.BlockSpec((tm, d), lambda i: (i, 0)),
                  pl.BlockSpec((1, d), lambda i: (0, 0)),
                  pl.BlockSpec((1, d), lambda i: (0, 0)),
                  pl.BlockSpec((1, 6, d), _row_tile_mod_idx(geom, tm)),
                  _resident((d, ff)), _resident((d, ff)), _resident((ff, d))],
        out_specs=pl.BlockSpec((tm, d), lambda i: (i, 0)),
        out_shape=jax.ShapeDtypeStruct((rows, d), f32),
        compiler_params=_cp(("arbitrary",), VMEM_LIMIT),
        name="dense_ffn",
    )(x, npre, npost, mods, w1, w3, w2)


def _router_body(x_ref, npre_ref, mod_ref, wr_ref, h_ref, route_ref):
    h = _norm_mod(x_ref[...], npre_ref[...], mod_ref[0], 3, 4)
    h_ref[...] = h
    h_hi = h.astype(bf16)
    h_lo = (h - h_hi.astype(f32)).astype(bf16)
    wr = wr_ref[...]
    w_hi = wr.astype(bf16)
    w_lo = (wr - w_hi.astype(f32)).astype(bf16)
    logits = _dot(h_hi, w_hi) + _dot(h_hi, w_lo) + _dot(h_lo, w_hi) + _dot(h_lo, w_lo)
    lane = lax.broadcasted_iota(jnp.int32, logits.shape, 1).astype(f32)
    l1 = jnp.where(lane < N_EXPERTS, logits, -jnp.inf)
    m1 = jnp.max(l1, axis=-1, keepdims=True)
    i1 = jnp.min(jnp.where(l1 == m1, lane, float(LANES)), axis=-1, keepdims=True)
    l2 = jnp.where(lane == i1, -jnp.inf, l1)
    m2 = jnp.max(l2, axis=-1, keepdims=True)
    i2 = jnp.min(jnp.where(l2 == m2, lane, float(LANES)), axis=-1, keepdims=True)
    e = jnp.exp(m2 - m1)
    w1 = 1.0 / (1.0 + e)
    w2 = e / (1.0 + e)
    route_ref[...] = jnp.where(lane == 0.0, i1, jnp.where(lane == 1.0, i2,
                                                          jnp.where(lane == 2.0, w1, jnp.where(lane == 3.0, w2, 0.0))))


def moe_router(x, npre, mods, w_router, geom, rows):
    d = D_MODEL
    tm = geom["tm_merge"]
    wr = jnp.pad(w_router, ((0, 0), (0, LANES - w_router.shape[1])))
    return pl.pallas_call(
        _router_body,
        grid=(rows // tm,),
        in_specs=[pl.BlockSpec((tm, d), lambda i: (i, 0)),
                  pl.BlockSpec((1, d), lambda i: (0, 0)),
                  pl.BlockSpec((1, 6, d), _row_tile_mod_idx(geom, tm)),
                  pl.BlockSpec((d, LANES), lambda i: (0, 0))],
        out_specs=[pl.BlockSpec((tm, d), lambda i: (i, 0)), pl.BlockSpec((tm, LANES), lambda i: (i, 0))],
        out_shape=[jax.ShapeDtypeStruct((rows, d), f32), jax.ShapeDtypeStruct((rows, LANES), f32)],
        compiler_params=_cp(("arbitrary",)),
        name="moe_router",
    )(x, npre, mods, wr)


def _row_copy(src, src_row, dst, dst_row, sem):
    return pltpu.make_async_copy(src.at[pl.ds(src_row, 1)], dst.at[pl.ds(dst_row, 1)], sem)


def _dispatch_body(slot_ref, h_ref, xb_in, xb_hbm, sem, *, rows):
    del xb_in

    def start(r, c):
        _row_copy(h_ref, r, xb_hbm, slot_ref[0, 0, 2 * r], sem).start()
        _row_copy(h_ref, r, xb_hbm, slot_ref[0, 0, 2 * r + 1], sem).start()
        return c

    lax.fori_loop(0, rows, start, 0, unroll=DMA_ISSUE_UNROLL)
    for _ in range(2):
        pltpu.make_async_copy(h_ref, xb_hbm.at[pl.ds(0, rows)], sem).wait()


def moe_dispatch(h, slots, n_slots):
    n_tok, d = h.shape
    rows = MOE_DISPATCH_ROWS
    steps = n_tok // rows
    return pl.pallas_call(
        functools.partial(_dispatch_body, rows=rows),
        grid=(steps,),
        in_specs=[pl.BlockSpec((1, 1, 2 * rows), lambda i: (i, 0, 0), memory_space=pltpu.SMEM),
                  pl.BlockSpec((rows, d), lambda i: (i, 0)),
                  pl.BlockSpec(memory_space=pl.ANY)],
        out_specs=pl.BlockSpec(memory_space=pl.ANY),
        out_shape=jax.ShapeDtypeStruct((n_slots, d), f32),
        scratch_shapes=[pltpu.SemaphoreType.DMA(())],
        input_output_aliases={2: 0},
        compiler_params=_cp(("arbitrary",)),
        name="moe_dispatch",
    )(slots.reshape(steps, 1, 2 * rows), h, jnp.zeros((n_slots, d), f32))


def _expert_body(be_ref, cnt_ref, used_ref, x_ref, w1_ref, w3_ref, w2_ref, o_ref):
    del be_ref, used_ref
    i = pl.program_id(0)

    @pl.when(cnt_ref[i] > 0)
    def _():
        o_ref[...] = _swiglu(x_ref[...].astype(bf16), w1_ref.at[0], w3_ref.at[0], w2_ref.at[0], FFN_SPLIT)

    @pl.when(cnt_ref[i] == 0)
    def _():
        o_ref[...] = jnp.zeros_like(o_ref)


def moe_experts(xb, blk_e, blk_cnt, n_used, w1, w3, w2):
    n_slots, d = xb.shape
    ff = w1.shape[2]
    tm = MOE_BLOCK
    grid_spec = pltpu.PrefetchScalarGridSpec(
        num_scalar_prefetch=3,
        grid=(n_slots // tm,),
        in_specs=[pl.BlockSpec((tm, d), lambda i, be, cnt, used: (jnp.minimum(i, used[0] - 1), 0)),
                  pl.BlockSpec((1, d, ff), lambda i, be, cnt, used: (be[i], 0, 0)),
                  pl.BlockSpec((1, d, ff), lambda i, be, cnt, used: (be[i], 0, 0)),
                  pl.BlockSpec((1, ff, d), lambda i, be, cnt, used: (be[i], 0, 0))],
        out_specs=pl.BlockSpec((tm, d), lambda i, be, cnt, used: (i, 0)),
    )
    return pl.pallas_call(
        _expert_body,
        grid_spec=grid_spec,
        out_shape=jax.ShapeDtypeStruct((n_slots, d), f32),
        compiler_params=_cp(("arbitrary",), VMEM_LIMIT),
        name="moe_experts",
    )(blk_e, blk_cnt, n_used, xb, w1, w3, w2)


def _combine_body(slot_ref, route_ref, x_ref, npost_ref, mod_ref, yb_hbm, o_ref, y0_scr, y1_scr, sem, *, rows):
    def start(r, c):
        _row_copy(yb_hbm, slot_ref[0, 0, 2 * r], y0_scr, r, sem).start()
        _row_copy(yb_hbm, slot_ref[0, 0, 2 * r + 1], y1_scr, r, sem).start()
        return c

    lax.fori_loop(0, rows, start, 0, unroll=DMA_ISSUE_UNROLL)
    for buf in (y0_scr, y1_scr):
        pltpu.make_async_copy(yb_hbm.at[pl.ds(0, rows)], buf, sem).wait()
    y = y0_scr[...] * route_ref[:, 2:3] + y1_scr[...] * route_ref[:, 3:4]
    o_ref[...] = x_ref[...] + mod_ref[0][5:6] * _rms(y, npost_ref[...])


def moe_combine(slots, route, x, npost, mods, yb, geom, rows_total):
    d = D_MODEL
    rows = MOE_COMBINE_ROWS
    steps = rows_total // rows
    return pl.pallas_call(
        functools.partial(_combine_body, rows=rows),
        grid=(steps,),
        in_specs=[pl.BlockSpec((1, 1, 2 * rows), lambda i: (i, 0, 0), memory_space=pltpu.SMEM),
                  pl.BlockSpec((rows, LANES), lambda i: (i, 0)),
                  pl.BlockSpec((rows, d), lambda i: (i, 0)),
                  pl.BlockSpec((1, d), lambda i: (0, 0)),
                  pl.BlockSpec((1, 6, d), _row_tile_mod_idx(geom, rows)),
                  pl.BlockSpec(memory_space=pl.ANY)],
        out_specs=pl.BlockSpec((rows, d), lambda i: (i, 0)),
        out_shape=jax.ShapeDtypeStruct((rows_total, d), f32),
        scratch_shapes=[pltpu.VMEM((rows, d), f32), pltpu.VMEM((rows, d), f32), pltpu.SemaphoreType.DMA(())],
        compiler_params=_cp(("arbitrary",)),
        name="moe_combine",
    )(slots.reshape(steps, 1, 2 * rows), route, x, npost, mods, yb)


def _moe_slots(route, n_tok):
    tm = MOE_BLOCK
    flat_e = route[:, 0:2].astype(jnp.int32).reshape(2 * n_tok)
    onehot = (flat_e[:, None] == jnp.arange(N_EXPERTS, dtype=jnp.int32)[None, :]).astype(jnp.int32)
    csum = jnp.cumsum(onehot, axis=0)
    counts = csum[-1]
    rank = jnp.sum(csum * onehot, axis=1) - 1
    padded = (counts + tm - 1) // tm * tm
    pend = jnp.cumsum(padded)
    pstart = pend - padded
    slots = jnp.sum(onehot * pstart[None, :], axis=1) + rank
    n_blocks = 2 * n_tok // tm + N_EXPERTS
    blk = jnp.arange(n_blocks, dtype=jnp.int32)
    blk_e = jnp.minimum(jnp.sum((blk[:, None] >= (pend // tm)[None, :]).astype(jnp.int32), axis=1), N_EXPERTS - 1)
    n_used = (pend[-1] // tm).astype(jnp.int32)
    cnt = jnp.clip(counts[blk_e] - (blk - pstart[blk_e] // tm) * tm, 0, tm)
    blk_cnt = jnp.where(blk < n_used, cnt, 0).astype(jnp.int32)
    return slots.astype(jnp.int32), blk_e, blk_cnt, n_used.reshape(1), n_blocks * tm


def moe_ffn(x, npre, npost, mods, w_router, w1, w3, w2, geom, rows):
    h, route = moe_router(x, npre, mods, w_router, geom, rows)
    slots, blk_e, blk_cnt, n_used, n_slots = _moe_slots(route, rows)
    xb = moe_dispatch(h, slots, n_slots)
    yb = moe_experts(xb, blk_e, blk_cnt, n_used, w1, w3, w2)
    return moe_combine(slots, route, x, npost, mods, yb, geom, rows)


def _rope_tables(geom):
    s, tm = geom["S"], geom["tm"]
    pos = jnp.arange(s)

    def angles(p, dim):
        inv = ROPE_BASE ** (-jnp.arange(0, dim, 2, dtype=f32) / dim)
        return p.astype(f32)[:, None] * inv[None, :]

    def with_identity(t, one):
        return jnp.concatenate([t, jnp.full((tm, LANES), one, f32)], axis=0)

    a = angles(pos, RET_DK)
    ret_cos = jnp.concatenate([jnp.cos(a), jnp.cos(a)], axis=1)
    ret_sin = jnp.concatenate([-jnp.sin(a), jnp.sin(a)], axis=1)
    half = ATT_HEAD_DIM // 2
    ar = angles(pos // GRID_W, half)
    ac = angles(pos % GRID_W, half)
    z = jnp.zeros_like(ar)
    att_cos = jnp.concatenate([jnp.cos(ar), jnp.cos(ar), jnp.cos(ac), jnp.cos(ac)], axis=1)
    att_s_up = jnp.concatenate([-jnp.sin(ar), z, -jnp.sin(ac), z], axis=1)
    att_s_dn = jnp.concatenate([z, jnp.sin(ar), z, jnp.sin(ac)], axis=1)
    ret = ((RET_DK // 2,), with_identity(ret_cos, 1.0), (with_identity(ret_sin, 0.0),))
    att = ((LANES - half // 2, half // 2), with_identity(att_cos, 1.0),
           (with_identity(att_s_up, 0.0), with_identity(att_s_dn, 0.0)))
    return ret, att


def _pack_w_in(w):
    rq, rk, rv, rg, sz, sxbc, sdt, aq, ak, av, bg = jnp.split(w, np.cumsum(IN_SPLITS)[:-1].tolist(), axis=1)
    pad = lambda t, n: jnp.pad(t, ((0, 0), (0, n - t.shape[1])))
    packed = jnp.concatenate([rq, rk, aq, ak, rv, av, rg, sz, sxbc,
                              pad(sdt[:, :SSM_HG], LANES), pad(sdt[:, SSM_HG:], 3 * LANES)], axis=1)
    return packed.astype(bf16), bg.astype(bf16)


def _pad_heads(v):
    return jnp.pad(v.astype(f32).reshape(2, SSM_GROUPS, SSM_HG), ((0, 0), (0, 0), (0, LANES - SSM_HG)))


def _geometry(batch, seq, ctx_len):
    tm = 1024
    while seq % tm or (batch * ctx_len) % tm:
        tm //= 2
    tm_merge = min(tm, 512)
    scan_tile = batch * ctx_len
    assert seq % scan_tile == 0 and scan_tile % (2 * CHUNK) == 0 and ctx_len % CHUNK == 0 and seq % GRID_W == 0
    return {"B": batch, "S": seq, "L": ctx_len, "nx": batch * seq, "tm": tm, "tm_merge": tm_merge, "T": scan_tile}


def kernel(x, c, ctx, c_ctx, ada_w, ada_b, mix_norm_pre, mix_norm_post, ffn_norm_pre, ffn_norm_post, w_in, ret_decay, ssm_conv_w, ssm_conv_b, ssm_dt_bias, ssm_a_log, ssm_d, ssm_norm_w, att_sink, w_branch, w_out, ffn_w1, ffn_w3, ffn_w2, moe_router, moe_w1, moe_w3, moe_w2):
    batch, seq, d = x.shape
    ctx_len = ctx.shape[1]
    depth = ada_w.shape[0]
    geom = _geometry(batch, seq, ctx_len)
    nx = geom["nx"]
    ntok = nx + batch * ctx_len

    stream = jnp.concatenate([x.reshape(nx, d), ctx.reshape(batch * ctx_len, d)], axis=0)
    cond = jnp.concatenate([c, c_ctx[None, :], jnp.zeros((SUBLANES - 1 - batch % SUBLANES, d), f32)], axis=0)
    mods_all = adaln_all(cond, ada_w, ada_b).reshape(depth, cond.shape[0], 6, d)
    rope_ret, rope_att = _rope_tables(geom)
    row = lambda v: v.reshape(1, -1)
    dt_cb0 = (2 * D_MODEL + IN_SPLITS[5]) // LANES
    z_cb0 = D_MODEL // SSM_GW
    conv_col0 = 2 * D_MODEL

    for i in range(depth):
        last = i == depth - 1
        rows = nx if last else ntok
        mods = mods_all[i]
        w_packed, w_gate = _pack_w_in(w_in[i])
        npre = row(mix_norm_pre[i])
        ret_qk, att_qk, val, d1 = in_proj(stream, npre, mods, w_packed, rope_ret, rope_att, geom)

        lg = jax.nn.log_sigmoid(ret_decay[i].astype(f32))
        ret_f = ret_scan(lg, ret_qk, val, geom, bwd=False)
        ret_o = ret_scan(lg, ret_qk, val, geom, bwd=True, d1=d1, prev=ret_f)

        u = ssd_conv(d1, ssm_conv_w[i], ssm_conv_b[i], geom, conv_col0)
        par = jnp.stack([_pad_heads(ssm_dt_bias[i]), _pad_heads(-jnp.exp(ssm_a_log[i].astype(f32)))], axis=2)
        skip = row(jnp.repeat(ssm_d[i, 0].astype(f32) + ssm_d[i, 1].astype(f32), SSM_HEAD_DIM))
        ssd_f = ssd_scan(u, d1, par[0], geom, dt_cb0, bwd=False)
        ssd_o = ssd_scan(u, d1, par[1], geom, dt_cb0, bwd=True, z_cb0=z_cb0, skip=skip,
                         norm_w=row(ssm_norm_w[i]), prev=ssd_f)

        att_o = window_attention(att_sink[i].astype(f32), att_qk, val, geom)

        stream = merge(ret_o, ssd_o, att_o, stream, npre, w_gate, w_branch[i].astype(bf16), w_out[i].astype(bf16),
                       row(mix_norm_post[i]), mods, geom, rows)

        j = i // 2
        fpre, fpost = row(ffn_norm_pre[i]), row(ffn_norm_post[i])
        if i % 2 == 0:
            stream = dense_ffn(stream, fpre, fpost, mods, ffn_w1[j].astype(bf16), ffn_w3[j].astype(bf16),
                               ffn_w2[j].astype(bf16), geom, rows)
        else:
            stream = moe_ffn(stream, fpre, fpost, mods, moe_router[j], moe_w1[j].astype(bf16),
                             moe_w3[j].astype(bf16), moe_w2[j].astype(bf16), geom, rows)
    return stream[:nx].reshape(batch, seq, d)
```

```python
import functools
import math

import jax
import jax.numpy as jnp
import numpy as np
from jax import lax
from jax.experimental import pallas as pl
from jax.experimental.pallas import tpu as pltpu

f32 = jnp.float32
bf16 = jnp.bfloat16

D_MODEL = 1024
GRID_W = 64
CHUNK = 128
NORM_EPS = 1e-6
ROPE_BASE = 10000.0
NEG_INF = -1e30
RET_HEADS, RET_DK, RET_DV = 4, 128, 256
SSM_HEADS, SSM_HEAD_DIM, SSM_GROUPS, SSM_STATE = 16, 64, 2, 128
SSM_HG = SSM_HEADS // SSM_GROUPS
SSM_GW = SSM_HG * SSM_HEAD_DIM
ATT_HEADS, ATT_KV_HEADS, ATT_HEAD_DIM = 8, 2, 128
ATT_G = ATT_HEADS // ATT_KV_HEADS
D_FF = 2816
N_EXPERTS = 8
IN_SPLITS = (512, 512, 1024, 1024, 1024, 1536, 16, 1024, 256, 256, 3072)

LANES = 128
SUBLANES = 8
MOE_BLOCK = 256
MOE_DISPATCH_ROWS = 256
MOE_COMBINE_ROWS = 256
VMEM_LIMIT = 56 * 2 ** 20
DMA_ISSUE_UNROLL = 8
ATT_STACK = ATT_G
SSD_CHUNK_GROUP = 4
RET_CHUNK_GROUP = 2
FFN_SPLIT = 2


def _cp(sem, vmem=None):
    return pltpu.CompilerParams(dimension_semantics=sem, vmem_limit_bytes=vmem)


def _silu(v):
    return v / (1.0 + jnp.exp(-v))


def _sigmoid(v):
    return 1.0 / (1.0 + jnp.exp(-v))


def _rms(v, w):
    return v * lax.rsqrt(jnp.mean(v * v, axis=-1, keepdims=True) + NORM_EPS) * w


def _norm_mod(x, nw, m, shift_row, scale_row):
    return _rms(x, nw) * (1.0 + m[scale_row:scale_row + 1]) + m[shift_row:shift_row + 1]


def _dot(a, b):
    return jnp.dot(a, b, preferred_element_type=f32)


def _dot_nt(a, b):
    return lax.dot_general(a, b, (((1,), (1,)), ((), ())), preferred_element_type=f32)


def _split3(a):
    hi = a.astype(bf16)
    r = a - hi.astype(f32)
    mid = r.astype(bf16)
    lo = (r - mid.astype(f32)).astype(bf16)
    return hi, mid, lo


def _row_tile_mod_idx(geom, tm):
    nxt = geom["nx"] // tm
    per_b = geom["S"] // tm
    return lambda i: (jnp.where(i < nxt, i // per_b, geom["B"]), 0, 0)


def _resident(shape):
    return pl.BlockSpec(shape, lambda i: (0,) * len(shape), pipeline_mode=pl.Buffered(1))


def _adaln_body(c_ref, w_ref, b_ref, o_ref):
    s = _silu(c_ref[...])
    o_ref[0] = _dot(s.astype(bf16), w_ref[0].astype(bf16)) + b_ref[0]


def adaln_all(cond, ada_w, ada_b):
    depth, d, n = ada_w.shape
    rows = cond.shape[0]
    tn = 1024
    return pl.pallas_call(
        _adaln_body,
        grid=(depth, n // tn),
        in_specs=[pl.BlockSpec((rows, d), lambda l, j: (0, 0)),
                  pl.BlockSpec((1, d, tn), lambda l, j: (l, 0, j)),
                  pl.BlockSpec((1, 1, tn), lambda l, j: (l, 0, j))],
        out_specs=pl.BlockSpec((1, rows, tn), lambda l, j: (l, 0, j)),
        out_shape=jax.ShapeDtypeStruct((depth, rows, n), f32),
        compiler_params=_cp(("arbitrary", "arbitrary")),
        name="adaln",
    )(cond, ada_w, ada_b.reshape(depth, 1, n))


PROJ_CHUNK = 512


def _proj_plan():
    rq, rk = RET_HEADS * RET_DK, RET_HEADS * RET_DK
    aq, ak = ATT_HEADS * ATT_HEAD_DIM, ATT_KV_HEADS * ATT_HEAD_DIM
    groups = [(0, rq, "ret", 1.0), (0, rk, "ret", RET_DK ** -0.5),
              (1, aq, "att", ATT_HEAD_DIM ** -0.5), (1, ak, "att", 1.0),
              (2, RET_HEADS * RET_DV + ak, None, 1.0),
              (3, 4 * D_MODEL, None, 1.0)]
    plan, wcol, ocol = [], 0, {}
    for out, width, kind, scale in groups:
        done = 0
        while done < width:
            step = min(PROJ_CHUNK, width - done)
            plan.append((wcol, step, out, ocol.get(out, 0), kind, scale))
            wcol += step
            ocol[out] = ocol.get(out, 0) + step
            done += step
    return plan, wcol, [ocol[o] for o in range(4)]


def _proj_body(x_ref, nw_ref, mod_ref, w_ref, rcos_ref, rsin_ref, acos_ref, aup_ref, adn_ref,
               o_ret, o_att, o_val, o_d1, *, plan, ret_shift, att_shifts):
    outs = (o_ret, o_att, o_val, o_d1)
    h = _norm_mod(x_ref[...], nw_ref[...], mod_ref[0], 0, 1).astype(bf16)
    for wcol, width, out, ocol, kind, scale in plan:
        acc = _dot(h, w_ref[:, wcol:wcol + width])
        o_ref = outs[out]
        if kind is None:
            o_ref[:, ocol:ocol + width] = acc.astype(o_ref.dtype)
            continue
        for c in range(width // LANES):
            t = acc[:, c * LANES:(c + 1) * LANES]
            if kind == "ret":
                r = t * rcos_ref[...] + pltpu.roll(t, ret_shift, 1) * rsin_ref[...]
            else:
                r = (t * acos_ref[...] + pltpu.roll(t, att_shifts[0], 1) * aup_ref[...]
                     + pltpu.roll(t, att_shifts[1], 1) * adn_ref[...])
            o_ref[:, ocol + c * LANES:ocol + (c + 1) * LANES] = (r * scale).astype(o_ref.dtype)


def in_proj(x, nw, mods, w, rope_ret, rope_att, geom):
    ntok, d = x.shape
    tm = geom["tm_merge"]
    plan, wcols, widths = _proj_plan()
    assert w.shape == (d, wcols)
    nxt = geom["nx"] // tm
    per_b = geom["S"] // tm
    pos_idx = lambda i: (jnp.where(i < nxt, i % per_b, per_b), 0)
    table = pl.BlockSpec((tm, LANES), pos_idx)
    dtypes = (bf16, bf16, bf16, f32)
    return pl.pallas_call(
        functools.partial(_proj_body, plan=plan, ret_shift=rope_ret[0][0], att_shifts=rope_att[0]),
        grid=(ntok // tm,),
        in_specs=[pl.BlockSpec((tm, d), lambda i: (i, 0)),
                  pl.BlockSpec((1, d), lambda i: (0, 0)),
                  pl.BlockSpec((1, 6, d), _row_tile_mod_idx(geom, tm)),
                  _resident((d, wcols)),
                  table, table, table, table, table],
        out_specs=[pl.BlockSpec((tm, n), lambda i: (i, 0)) for n in widths],
        out_shape=[jax.ShapeDtypeStruct((ntok, n), dt) for n, dt in zip(widths, dtypes)],
        compiler_params=_cp(("arbitrary",), VMEM_LIMIT),
        name="in_proj",
    )(x, nw, mods, w, rope_ret[1], rope_ret[2][0], rope_att[1], rope_att[2][0], rope_att[2][1])


def _conv_body(x_ref, prev_ref, next_ref, w_ref, b_ref, o_ref, *, tiles_per_seq, nx_tiles, ctx_len):
    i = pl.program_id(0)
    x = x_ref[...]
    rows = x.shape[0]
    t_in = i % tiles_per_seq
    is_x = i < nx_tiles
    use_prev = jnp.logical_and(is_x, t_in > 0)
    use_next = jnp.logical_and(is_x, t_in < tiles_per_seq - 1)
    prow = jnp.where(use_prev, prev_ref[SUBLANES - 1:SUBLANES, :], 0.0)
    nrow = jnp.where(use_next, next_ref[0:1, :], 0.0)
    rid = lax.broadcasted_iota(jnp.int32, x.shape, 0)
    xm = jnp.where(rid == 0, prow, pltpu.roll(x, 1, 0))
    xp = jnp.where(rid == rows - 1, nrow, pltpu.roll(x, rows - 1, 0))
    seg = lax.rem(rid, ctx_len)
    is_ctx = jnp.logical_not(is_x)
    xm = jnp.where(jnp.logical_and(is_ctx, seg == 0), 0.0, xm)
    xp = jnp.where(jnp.logical_and(is_ctx, seg == ctx_len - 1), 0.0, xp)
    w = w_ref[...]
    o_ref[...] = _silu(xm * w[0:1] + x * w[1:2] + xp * w[2:3] + b_ref[...])


def ssd_conv(d1, conv_w, conv_b, geom, col0):
    ntok = d1.shape[0]
    width = conv_w.shape[1]
    tc = 512
    cb0 = col0 // tc
    tr = geom["T"]
    per8 = tr // SUBLANES
    last8 = ntok // SUBLANES - 1
    return pl.pallas_call(
        functools.partial(_conv_body, tiles_per_seq=geom["S"] // tr, nx_tiles=geom["nx"] // tr, ctx_len=geom["L"]),
        grid=(ntok // tr, width // tc),
        in_specs=[pl.BlockSpec((tr, tc), lambda i, j: (i, cb0 + j)),
                  pl.BlockSpec((SUBLANES, tc), lambda i, j: (jnp.maximum(i * per8 - 1, 0), cb0 + j)),
                  pl.BlockSpec((SUBLANES, tc), lambda i, j: (jnp.minimum((i + 1) * per8, last8), cb0 + j)),
                  pl.BlockSpec((3, tc), lambda i, j: (0, j)),
                  pl.BlockSpec((1, tc), lambda i, j: (0, j))],
        out_specs=pl.BlockSpec((tr, tc), lambda i, j: (i, j)),
        out_shape=jax.ShapeDtypeStruct((ntok, width), f32),
        compiler_params=_cp(("arbitrary", "arbitrary")),
        name="ssd_conv",
    )(d1, d1, d1, conv_w, conv_b.reshape(1, width))


def _scan_row_index(geom, bwd):
    tile = geom["T"]
    per_b = geom["S"] // tile
    ctx_tile = geom["nx"] // tile

    def idx(s):
        t = s - 1
        xt = (t // per_b) * per_b + (per_b - 1 - t % per_b) if bwd else t
        return jnp.where(s == 0, ctx_tile, xt)

    return idx, ctx_tile + 1


def _scan_driver(chunk, s_all, *, bwd, geom, group):
    tile, ctx_len, nbatch = geom["T"], geom["L"], geom["B"]
    per_b = geom["S"] // tile
    s = pl.program_id(1)

    @pl.when(s == 0)
    def _():
        for b in range(nbatch):
            state = jnp.zeros(s_all.shape[1:], f32)
            cs = range(ctx_len // CHUNK)
            for ci in (reversed(cs) if bwd else cs):
                state = chunk(b * ctx_len + ci * CHUNK, state)
            s_all[b] = state

    @pl.when(s > 0)
    def _():
        st = s_all.at[(s - 1) // per_b]
        n_groups = tile // (group * CHUNK)
        order = tuple(reversed(range(group))) if bwd else tuple(range(group))

        def body(p, state):
            base = ((n_groups - 1 - p) if bwd else p) * (group * CHUNK)
            if not isinstance(base, int):
                base = pl.multiple_of(base, group * CHUNK)
            for ci in order:
                state = chunk(base + ci * CHUNK, state)
            return state

        st[...] = body(0, st[...]) if n_groups == 1 else lax.fori_loop(0, n_groups, body, st[...])


def _expand_heads(a, lane):
    rows = a.shape[0]
    lane = lane[:rows]
    cols = []
    for v in range(SSM_HG // 2):
        left = jnp.broadcast_to(a[:, 2 * v:2 * v + 1], (rows, LANES))
        right = jnp.broadcast_to(a[:, 2 * v + 1:2 * v + 2], (rows, LANES))
        cols.append(jnp.where(lane < SSM_HEAD_DIM, left, right))
    return jnp.concatenate(cols, axis=1)


def _ssd_body(*refs, bwd, geom):
    if bwd:
        xs_ref, b_ref, c_ref, dt_ref, par_ref, z_ref, skip_ref, nw_ref, prev_ref, o_ref, s_all = refs
    else:
        xs_ref, b_ref, c_ref, dt_ref, par_ref, o_ref, s_all = refs

    par = par_ref[0]
    ii = lax.broadcasted_iota(jnp.int32, (CHUNK, CHUNK), 0)
    jj = lax.broadcasted_iota(jnp.int32, (CHUNK, CHUNK), 1)
    tri_mask = (ii <= jj) if bwd else (ii >= jj)
    tri = jnp.where(tri_mask, 1.0, 0.0).astype(bf16)
    lane = jj
    last = 0 if bwd else CHUNK - 1
    pair_keep = [jnp.where((lane < SSM_HEAD_DIM) == (a == 0), 1.0, 0.0).astype(bf16) for a in range(2)]

    def chunk(r0, state):
        sl = pl.ds(r0, CHUNK)
        pre = dt_ref[sl, :] + par[0:1]
        dt = jnp.maximum(pre, 0.0) + jnp.log(1.0 + jnp.exp(-jnp.abs(pre)))
        hi, mid, lo = _split3(dt * par[1:2])
        cum = _dot(tri, hi) + _dot(tri, mid) + _dot(tri, lo)
        tot = cum[last:last + 1]
        cum_t = cum.T[0:SUBLANES]
        dt_t = dt.T[0:SUBLANES]
        dtws_t = dt_t * jnp.exp(cum_t[:, last:last + 1] - cum_t)
        dec_x = _expand_heads(jnp.broadcast_to(jnp.exp(tot), (SUBLANES, LANES)), lane)[0:1]

        xs = xs_ref[sl, :]
        xs_b = xs.astype(bf16)
        bm = b_ref[sl, :]
        cm = c_ref[sl, :]
        cb = _dot_nt(cm.astype(bf16), bm.astype(bf16))
        bm_t = bm.T
        state_b = state.astype(bf16)
        y_parts, upd_parts = [], []
        for v in range(SSM_HG // 2):
            ps = slice(v * LANES, (v + 1) * LANES)
            y_p = upd_p = None
            for a in range(2):
                h = 2 * v + a
                xs_h = xs_b[:, ps] * pair_keep[a]
                st_h = state_b[:, ps] * pair_keep[a]
                cbc = jnp.broadcast_to(cum[:, h:h + 1], (CHUNK, CHUNK))
                lmat = jnp.exp(jnp.where(tri_mask, cbc - cum_t[h:h + 1, :], NEG_INF))
                intra = (cb * lmat * dt_t[h:h + 1, :]).astype(bf16)
                inter = (cm * jnp.exp(cbc)).astype(bf16)
                t = _dot(intra, xs_h) + _dot(inter, st_h)
                y_p = t if y_p is None else y_p + t
                t = _dot((bm_t * dtws_t[h:h + 1, :]).astype(bf16), xs_h)
                upd_p = t if upd_p is None else upd_p + t
            y_parts.append(y_p)
            upd_parts.append(upd_p)
        y = jnp.concatenate(y_parts, axis=1)
        if bwd:
            y = y + prev_ref[sl, :] + xs * skip_ref[...]
            gz = y * _silu(z_ref[sl, :])
            o_ref[sl, :] = _rms(gz, nw_ref[...]).astype(o_ref.dtype)
        else:
            o_ref[sl, :] = y
        return dec_x * state + jnp.concatenate(upd_parts, axis=1)

    _scan_driver(chunk, s_all, bwd=bwd, geom=geom, group=SSD_CHUNK_GROUP)


def ssd_scan(u, d1, par, geom, dt_cb0, bwd, z_cb0=None, skip=None, norm_w=None, prev=None):
    ntok = u.shape[0]
    idx, steps = _scan_row_index(geom, bwd)
    gw = SSM_GW
    nb_blk = SSM_HEADS * SSM_HEAD_DIM // SSM_STATE
    tile = lambda w, cb: pl.BlockSpec((geom["T"], w), lambda g, s, cb=cb: (idx(s), cb(g)))
    in_specs = [tile(gw, lambda g: g),
                tile(SSM_STATE, lambda g: nb_blk + g),
                tile(SSM_STATE, lambda g: nb_blk + SSM_GROUPS + g),
                tile(LANES, lambda g: dt_cb0 + g),
                pl.BlockSpec((1, 2, LANES), lambda g, s: (g, 0, 0))]
    args = [u, u, u, d1, par]
    if bwd:
        in_specs += [tile(gw, lambda g: z_cb0 + g),
                     pl.BlockSpec((1, gw), lambda g, s: (0, g)),
                     pl.BlockSpec((1, gw), lambda g, s: (0, g)),
                     tile(gw, lambda g: g)]
        args += [d1, skip, norm_w, prev]
    return pl.pallas_call(
        functools.partial(_ssd_body, bwd=bwd, geom=geom),
        grid=(SSM_GROUPS, steps),
        in_specs=in_specs,
        out_specs=tile(gw, lambda g: g),
        out_shape=jax.ShapeDtypeStruct((ntok, SSM_GROUPS * gw), bf16 if bwd else f32),
        scratch_shapes=[pltpu.VMEM((geom["B"], SSM_STATE, gw), f32)],
        compiler_params=_cp(("arbitrary", "arbitrary"), VMEM_LIMIT),
        name="ssd_bwd" if bwd else "ssd_fwd",
    )(*args)


def _ret_body(*refs, bwd, geom):
    if bwd:
        lg_ref, q_ref, k_ref, v_ref, g_ref, prev_ref, o_ref, s_all = refs
    else:
        lg_ref, q_ref, k_ref, v_ref, o_ref, s_all = refs

    ii = lax.broadcasted_iota(jnp.int32, (CHUNK, CHUNK), 0).astype(f32)
    jj = lax.broadcasted_iota(jnp.int32, (CHUNK, CHUNK), 1).astype(f32)
    diff = (jj - ii) if bwd else (ii - jj)
    icol = ii[:, 0:1]
    jrow = jj[0:1, :]
    consts = []
    for h in range(RET_HEADS):
        lg = lg_ref[1 if bwd else 0, h]
        dmat = jnp.where(diff >= 0, jnp.exp(jnp.maximum(diff, 0.0) * lg), 0.0)
        if bwd:
            wq = jnp.exp((CHUNK - icol) * lg)
            ws = jnp.exp(jrow * lg)
        else:
            wq = jnp.exp((icol + 1.0) * lg)
            ws = jnp.exp((CHUNK - 1.0 - jrow) * lg)
        consts.append((dmat, wq, ws, jnp.exp(jnp.full((1, 1), float(CHUNK), f32) * lg)))

    def chunk(r0, state):
        sl = pl.ds(r0, CHUNK)
        new_state = []
        for h, (dmat, wq, ws, decay) in enumerate(consts):
            ks = slice(h * RET_DK, (h + 1) * RET_DK)
            vs = slice(h * RET_DV, (h + 1) * RET_DV)
            q = q_ref[sl, ks]
            k = k_ref[sl, ks]
            v = v_ref[sl, vs]
            sc = _dot_nt(q, k) * dmat
            o = _dot(sc.astype(bf16), v) + _dot(q, state[h].astype(bf16)) * wq
            kt = (k.astype(f32).T * ws).astype(bf16)
            if bwd:
                o = o + prev_ref[sl, vs]
                o = o * lax.rsqrt(jnp.mean(o * o, axis=-1, keepdims=True) + NORM_EPS)
                o_ref[sl, vs] = (_silu(g_ref[sl, vs]) * o).astype(o_ref.dtype)
            else:
                o_ref[sl, vs] = o
            new_state.append(decay * state[h] + _dot(kt, v))
        return jnp.stack(new_state)

    _scan_driver(chunk, s_all, bwd=bwd, geom=geom, group=RET_CHUNK_GROUP)


def ret_scan(lg, qk, vb, geom, bwd, d1=None, prev=None):
    ntok = qk.shape[0]
    idx, steps = _scan_row_index(geom, bwd)
    kw, vw = RET_HEADS * RET_DK, RET_HEADS * RET_DV
    tile = lambda w, cb: pl.BlockSpec((geom["T"], w), lambda z, s: (idx(s), cb))
    in_specs = [pl.BlockSpec(memory_space=pltpu.SMEM), tile(kw, 0), tile(kw, 1), tile(vw, 0)]
    args = [lg, qk, qk, vb]
    if bwd:
        in_specs += [tile(vw, 0), tile(vw, 0)]
        args += [d1, prev]
    return pl.pallas_call(
        functools.partial(_ret_body, bwd=bwd, geom=geom),
        grid=(1, steps),
        in_specs=in_specs,
        out_specs=tile(vw, 0),
        out_shape=jax.ShapeDtypeStruct((ntok, vw), bf16 if bwd else f32),
        scratch_shapes=[pltpu.VMEM((geom["B"], RET_HEADS, RET_DK, RET_DV), f32)],
        compiler_params=_cp(("arbitrary", "arbitrary"), VMEM_LIMIT),
        name="ret_bwd" if bwd else "ret_fwd",
    )(*args)


def _att_body(sink_ref, q_ref, kc_ref, vc_ref, kp_ref, ko_ref, kn_ref, vp_ref, vo_ref, vn_ref, o_ref, *, nblk, ctx_blk):
    n = pl.program_id(1) - ctx_blk
    own_ok = n >= 0
    prev_ok = n >= 1
    next_ok = jnp.logical_and(n >= 0, n <= nblk - 2)
    rows = ATT_STACK * CHUNK
    qi = lax.rem(lax.broadcasted_iota(jnp.int32, (rows, CHUNK), 0), CHUNK)
    kj = lax.broadcasted_iota(jnp.int32, (rows, CHUNK), 1)
    m_prev = jnp.logical_and(kj >= qi, prev_ok)
    m_next = jnp.logical_and(kj <= qi, next_ok)
    hd = ATT_HEAD_DIM
    for h0 in range(0, ATT_HEADS, ATT_STACK):
        kv = h0 // ATT_G
        ks = slice(kv * hd, (kv + 1) * hd)
        kc, vc = kc_ref[:, ks], vc_ref[:, ks]
        heads = list(range(h0, h0 + ATT_STACK))
        q = jnp.concatenate([q_ref[:, h * hd:(h + 1) * hd] for h in heads], axis=0)
        sk = jnp.concatenate([jnp.full((CHUNK, 1), sink_ref[h], f32) for h in heads], axis=0)
        s_c = _dot_nt(q, kc)
        s_p = jnp.where(m_prev, _dot_nt(q, kp_ref[:, ks]), NEG_INF)
        s_o = jnp.where(own_ok, _dot_nt(q, ko_ref[:, ks]), NEG_INF)
        s_n = jnp.where(m_next, _dot_nt(q, kn_ref[:, ks]), NEG_INF)
        parts = [s_c[:, c * CHUNK:(c + 1) * CHUNK] for c in range(s_c.shape[1] // CHUNK)] + [s_p, s_o, s_n]
        mx = jnp.maximum(jnp.max(functools.reduce(jnp.maximum, parts), axis=-1, keepdims=True), sk)
        probs = [jnp.exp(t - mx) for t in parts]
        den = jnp.exp(sk - mx) + jnp.sum(functools.reduce(jnp.add, probs), axis=-1, keepdims=True)
        n_c = len(parts) - 3
        p_c = jnp.concatenate(probs[:n_c], axis=1)
        o = (_dot(p_c.astype(bf16), vc) + _dot(probs[n_c].astype(bf16), vp_ref[:, ks])
             + _dot(probs[n_c + 1].astype(bf16), vo_ref[:, ks]) + _dot(probs[n_c + 2].astype(bf16), vn_ref[:, ks]))
        o = o / den
        for g, h in enumerate(heads):
            o_ref[:, h * hd:(h + 1) * hd] = o[g * CHUNK:(g + 1) * CHUNK].astype(o_ref.dtype)


def window_attention(sink, qk, vb, geom):
    ntok = qk.shape[0]
    blk = CHUNK
    nblk = geom["S"] // blk
    ctx_blk = geom["L"] // blk
    ctx0 = geom["nx"] // blk
    kvw = ATT_KV_HEADS * ATT_HEAD_DIM
    qw = ATT_HEADS * ATT_HEAD_DIM
    kcol = qw // kvw

    def q_idx(b, i):
        return jnp.where(i < ctx_blk, ctx0 + b * ctx_blk + i, b * nblk + i - ctx_blk)

    def win(off):
        def f(b, i):
            n = jnp.clip(i - ctx_blk + off, 0, nblk - 1)
            return (b * nblk + n, kcol)
        return f

    ctx_spec = pl.BlockSpec((geom["L"], kvw), lambda b, i: (geom["nx"] // geom["L"] + b, kcol))
    wspec = lambda off: pl.BlockSpec((blk, kvw), win(off))
    return pl.pallas_call(
        functools.partial(_att_body, nblk=nblk, ctx_blk=ctx_blk),
        grid=(geom["B"], ctx_blk + nblk),
        in_specs=[pl.BlockSpec(memory_space=pltpu.SMEM),
                  pl.BlockSpec((blk, qw), lambda b, i: (q_idx(b, i), 0)),
                  ctx_spec, ctx_spec,
                  wspec(-1), wspec(0), wspec(1), wspec(-1), wspec(0), wspec(1)],
        out_specs=pl.BlockSpec((blk, qw), lambda b, i: (q_idx(b, i), 0)),
        out_shape=jax.ShapeDtypeStruct((ntok, qw), bf16),
        compiler_params=_cp(("arbitrary", "arbitrary")),
        name="window_attention",
    )(sink, qk, qk, vb, qk, qk, qk, vb, vb, vb)


def _merge_body(r_ref, s_ref, a_ref, x_ref, npre_ref, wg_ref, wb_ref, wo_ref, nw_ref, mod_ref, o_ref):
    x = x_ref[...]
    m = mod_ref[0]
    h = _norm_mod(x, npre_ref[...], m, 0, 1).astype(bf16)
    acc = None
    for n, ref in enumerate((r_ref, s_ref, a_ref)):
        gate = _dot(h, wg_ref[:, n * D_MODEL:(n + 1) * D_MODEL])
        t = _sigmoid(gate) * _dot(ref[...], wb_ref[n])
        acc = t if acc is None else acc + t
    y = _dot(acc.astype(bf16), wo_ref[...])
    o_ref[...] = x + m[2:3] * _rms(y, nw_ref[...])


def merge(ret_o, ssd_o, att_o, x, npre, wg, wb, wo, nw, mods, geom, rows):
    d = D_MODEL
    tm = geom["tm_merge"]
    row = lambda w: pl.BlockSpec((tm, w), lambda i: (i, 0))
    return pl.pallas_call(
        _merge_body,
        grid=(rows // tm,),
        in_specs=[row(d), row(d), row(d), row(d),
                  pl.BlockSpec((1, d), lambda i: (0, 0)),
                  _resident((d, 3 * d)), _resident((3, d, d)), _resident((d, d)),
                  pl.BlockSpec((1, d), lambda i: (0, 0)),
                  pl.BlockSpec((1, 6, d), _row_tile_mod_idx(geom, tm))],
        out_specs=row(d),
        out_shape=jax.ShapeDtypeStruct((rows, d), f32),
        compiler_params=_cp(("arbitrary",), VMEM_LIMIT),
        name="merge",
    )(ret_o, ssd_o, att_o, x, npre, wg, wb, wo, nw, mods)


def _swiglu(h, w1, w3, w2, f_split):
    step = w1.shape[1] // f_split
    acc = None
    for f in range(f_split):
        fs = slice(f * step, (f + 1) * step)
        act = _silu(_dot(h, w1[:, fs])) * _dot(h, w3[:, fs])
        t = _dot(act.astype(bf16), w2[fs, :])
        acc = t if acc is None else acc + t
    return acc


def _ffn_body(x_ref, npre_ref, npost_ref, mod_ref, w1_ref, w3_ref, w2_ref, o_ref):
    x = x_ref[...]
    m = mod_ref[0]
    h = _norm_mod(x, npre_ref[...], m, 3, 4).astype(bf16)
    y = _swiglu(h, w1_ref, w3_ref, w2_ref, FFN_SPLIT)
    o_ref[...] = x + m[5:6] * _rms(y, npost_ref[...])


def dense_ffn(x, npre, npost, mods, w1, w3, w2, geom, rows):
    d = D_MODEL
    ff = w1.shape[1]
    tm = geom["tm_merge"]
    return pl.pallas_call(
        _ffn_body,
        grid=(rows // tm,),
        in_specs=[pl.BlockSpec((tm, d), lambda i: (i, 0)),
                  pl.BlockSpec((1, d), lambda i: (0, 0)),
                  pl.BlockSpec((1, d), lambda i: (0, 0)),
                  pl.BlockSpec((1, 6, d), _row_tile_mod_idx(geom, tm)),
                  _resident((d, ff)), _resident((d, ff)), _resident((ff, d))],
        out_specs=pl.BlockSpec((tm, d), lambda i: (i, 0)),
        out_shape=jax.ShapeDtypeStruct((rows, d), f32),
        compiler_params=_cp(("arbitrary",), VMEM_LIMIT),
        name="dense_ffn",
    )(x, npre, npost, mods, w1, w3, w2)


def _router_body(x_ref, npre_ref, mod_ref, wr_ref, h_ref, route_ref):
    h = _norm_mod(x_ref[...], npre_ref[...], mod_ref[0], 3, 4)
    h_ref[...] = h
    h_hi = h.astype(bf16)
    h_lo = (h - h_hi.astype(f32)).astype(bf16)
    wr = wr_ref[...]
    w_hi = wr.astype(bf16)
    w_lo = (wr - w_hi.astype(f32)).astype(bf16)
    logits = _dot(h_hi, w_hi) + _dot(h_hi, w_lo) + _dot(h_lo, w_hi) + _dot(h_lo, w_lo)
    lane = lax.broadcasted_iota(jnp.int32, logits.shape, 1).astype(f32)
    l1 = jnp.where(lane < N_EXPERTS, logits, -jnp.inf)
    m1 = jnp.max(l1, axis=-1, keepdims=True)
    i1 = jnp.min(jnp.where(l1 == m1, lane, float(LANES)), axis=-1, keepdims=True)
    l2 = jnp.where(lane == i1, -jnp.inf, l1)
    m2 = jnp.max(l2, axis=-1, keepdims=True)
    i2 = jnp.min(jnp.where(l2 == m2, lane, float(LANES)), axis=-1, keepdims=True)
    e = jnp.exp(m2 - m1)
    w1 = 1.0 / (1.0 + e)
    w2 = e / (1.0 + e)
    route_ref[...] = jnp.where(lane == 0.0, i1, jnp.where(lane == 1.0, i2,
                                                          jnp.where(lane == 2.0, w1, jnp.where(lane == 3.0, w2, 0.0))))


def moe_router(x, npre, mods, w_router, geom, rows):
    d = D_MODEL
    tm = geom["tm_merge"]
    wr = jnp.pad(w_router, ((0, 0), (0, LANES - w_router.shape[1])))
    return pl.pallas_call(
        _router_body,
        grid=(rows // tm,),
        in_specs=[pl.BlockSpec((tm, d), lambda i: (i, 0)),
                  pl.BlockSpec((1, d), lambda i: (0, 0)),
                  pl.BlockSpec((1, 6, d), _row_tile_mod_idx(geom, tm)),
                  pl.BlockSpec((d, LANES), lambda i: (0, 0))],
        out_specs=[pl.BlockSpec((tm, d), lambda i: (i, 0)), pl.BlockSpec((tm, LANES), lambda i: (i, 0))],
        out_shape=[jax.ShapeDtypeStruct((rows, d), f32), jax.ShapeDtypeStruct((rows, LANES), f32)],
        compiler_params=_cp(("arbitrary",)),
        name="moe_router",
    )(x, npre, mods, wr)


def _row_copy(src, src_row, dst, dst_row, sem):
    return pltpu.make_async_copy(src.at[pl.ds(src_row, 1)], dst.at[pl.ds(dst_row, 1)], sem)


def _dispatch_body(slot_ref, h_ref, xb_in, xb_hbm, sem, *, rows):
    del xb_in

    def start(r, c):
        _row_copy(h_ref, r, xb_hbm, slot_ref[0, 0, 2 * r], sem).start()
        _row_copy(h_ref, r, xb_hbm, slot_ref[0, 0, 2 * r + 1], sem).start()
        return c

    lax.fori_loop(0, rows, start, 0, unroll=DMA_ISSUE_UNROLL)
    for _ in range(2):
        pltpu.make_async_copy(h_ref, xb_hbm.at[pl.ds(0, rows)], sem).wait()


def moe_dispatch(h, slots, n_slots):
    n_tok, d = h.shape
    rows = MOE_DISPATCH_ROWS
    steps = n_tok // rows
    return pl.pallas_call(
        functools.partial(_dispatch_body, rows=rows),
        grid=(steps,),
        in_specs=[pl.BlockSpec((1, 1, 2 * rows), lambda i: (i, 0, 0), memory_space=pltpu.SMEM),
                  pl.BlockSpec((rows, d), lambda i: (i, 0)),
                  pl.BlockSpec(memory_space=pl.ANY)],
        out_specs=pl.BlockSpec(memory_space=pl.ANY),
        out_shape=jax.ShapeDtypeStruct((n_slots, d), f32),
        scratch_shapes=[pltpu.SemaphoreType.DMA(())],
        input_output_aliases={2: 0},
        compiler_params=_cp(("arbitrary",)),
        name="moe_dispatch",
    )(slots.reshape(steps, 1, 2 * rows), h, jnp.zeros((n_slots, d), f32))


def _expert_body(be_ref, cnt_ref, used_ref, x_ref, w1_ref, w3_ref, w2_ref, o_ref):
    del be_ref, used_ref
    i = pl.program_id(0)

    @pl.when(cnt_ref[i] > 0)
    def _():
        o_ref[...] = _swiglu(x_ref[...].astype(bf16), w1_ref.at[0], w3_ref.at[0], w2_ref.at[0], FFN_SPLIT)

    @pl.when(cnt_ref[i] == 0)
    def _():
        o_ref[...] = jnp.zeros_like(o_ref)


def moe_experts(xb, blk_e, blk_cnt, n_used, w1, w3, w2):
    n_slots, d = xb.shape
    ff = w1.shape[2]
    tm = MOE_BLOCK
    grid_spec = pltpu.PrefetchScalarGridSpec(
        num_scalar_prefetch=3,
        grid=(n_slots // tm,),
        in_specs=[pl.BlockSpec((tm, d), lambda i, be, cnt, used: (jnp.minimum(i, used[0] - 1), 0)),
                  pl.BlockSpec((1, d, ff), lambda i, be, cnt, used: (be[i], 0, 0)),
                  pl.BlockSpec((1, d, ff), lambda i, be, cnt, used: (be[i], 0, 0)),
                  pl.BlockSpec((1, ff, d), lambda i, be, cnt, used: (be[i], 0, 0))],
        out_specs=pl.BlockSpec((tm, d), lambda i, be, cnt, used: (i, 0)),
    )
    return pl.pallas_call(
        _expert_body,
        grid_spec=grid_spec,
        out_shape=jax.ShapeDtypeStruct((n_slots, d), f32),
        compiler_params=_cp(("arbitrary",), VMEM_LIMIT),
        name="moe_experts",
    )(blk_e, blk_cnt, n_used, xb, w1, w3, w2)


def _combine_body(slot_ref, route_ref, x_ref, npost_ref, mod_ref, yb_hbm, o_ref, y0_scr, y1_scr, sem, *, rows):
    def start(r, c):
        _row_copy(yb_hbm, slot_ref[0, 0, 2 * r], y0_scr, r, sem).start()
        _row_copy(yb_hbm, slot_ref[0, 0, 2 * r + 1], y1_scr, r, sem).start()
        return c

    lax.fori_loop(0, rows, start, 0, unroll=DMA_ISSUE_UNROLL)
    for buf in (y0_scr, y1_scr):
        pltpu.make_async_copy(yb_hbm.at[pl.ds(0, rows)], buf, sem).wait()
    y = y0_scr[...] * route_ref[:, 2:3] + y1_scr[...] * route_ref[:, 3:4]
    o_ref[...] = x_ref[...] + mod_ref[0][5:6] * _rms(y, npost_ref[...])


def moe_combine(slots, route, x, npost, mods, yb, geom, rows_total):
    d = D_MODEL
    rows = MOE_COMBINE_ROWS
    steps = rows_total // rows
    return pl.pallas_call(
        functools.partial(_combine_body, rows=rows),
        grid=(steps,),
        in_specs=[pl.BlockSpec((1, 1, 2 * rows), lambda i: (i, 0, 0), memory_space=pltpu.SMEM),
                  pl.BlockSpec((rows, LANES), lambda i: (i, 0)),
                  pl.BlockSpec((rows, d), lambda i: (i, 0)),
                  pl.BlockSpec((1, d), lambda i: (0, 0)),
                  pl.BlockSpec((1, 6, d), _row_tile_mod_idx(geom, rows)),
                  pl.BlockSpec(memory_space=pl.ANY)],
        out_specs=pl.BlockSpec((rows, d), lambda i: (i, 0)),
        out_shape=jax.ShapeDtypeStruct((rows_total, d), f32),
        scratch_shapes=[pltpu.VMEM((rows, d), f32), pltpu.VMEM((rows, d), f32), pltpu.SemaphoreType.DMA(())],
        compiler_params=_cp(("arbitrary",)),
        name="moe_combine",
    )(slots.reshape(steps, 1, 2 * rows), route, x, npost, mods, yb)


def _moe_slots(route, n_tok):
    tm = MOE_BLOCK
    flat_e = route[:, 0:2].astype(jnp.int32).reshape(2 * n_tok)
    onehot = (flat_e[:, None] == jnp.arange(N_EXPERTS, dtype=jnp.int32)[None, :]).astype(jnp.int32)
    csum = jnp.cumsum(onehot, axis=0)
    counts = csum[-1]
    rank = jnp.sum(csum * onehot, axis=1) - 1
    padded = (counts + tm - 1) // tm * tm
    pend = jnp.cumsum(padded)
    pstart = pend - padded
    slots = jnp.sum(onehot * pstart[None, :], axis=1) + rank
    n_blocks = 2 * n_tok // tm + N_EXPERTS
    blk = jnp.arange(n_blocks, dtype=jnp.int32)
    blk_e = jnp.minimum(jnp.sum((blk[:, None] >= (pend // tm)[None, :]).astype(jnp.int32), axis=1), N_EXPERTS - 1)
    n_used = (pend[-1] // tm).astype(jnp.int32)
    cnt = jnp.clip(counts[blk_e] - (blk - pstart[blk_e] // tm) * tm, 0, tm)
    blk_cnt = jnp.where(blk < n_used, cnt, 0).astype(jnp.int32)
    return slots.astype(jnp.int32), blk_e, blk_cnt, n_used.reshape(1), n_blocks * tm


def moe_ffn(x, npre, npost, mods, w_router, w1, w3, w2, geom, rows):
    h, route = moe_router(x, npre, mods, w_router, geom, rows)
    slots, blk_e, blk_cnt, n_used, n_slots = _moe_slots(route, rows)
    xb = moe_dispatch(h, slots, n_slots)
    yb = moe_experts(xb, blk_e, blk_cnt, n_used, w1, w3, w2)
    return moe_combine(slots, route, x, npost, mods, yb, geom, rows)


def _rope_tables(geom):
    s, tm = geom["S"], geom["tm"]
    pos = jnp.arange(s)

    def angles(p, dim):
        inv = ROPE_BASE ** (-jnp.arange(0, dim, 2, dtype=f32) / dim)
        return p.astype(f32)[:, None] * inv[None, :]

    def with_identity(t, one):
        return jnp.concatenate([t, jnp.full((tm, LANES), one, f32)], axis=0)

    a = angles(pos, RET_DK)
    ret_cos = jnp.concatenate([jnp.cos(a), jnp.cos(a)], axis=1)
    ret_sin = jnp.concatenate([-jnp.sin(a), jnp.sin(a)], axis=1)
    half = ATT_HEAD_DIM // 2
    ar = angles(pos // GRID_W, half)
    ac = angles(pos % GRID_W, half)
    z = jnp.zeros_like(ar)
    att_cos = jnp.concatenate([jnp.cos(ar), jnp.cos(ar), jnp.cos(ac), jnp.cos(ac)], axis=1)
    att_s_up = jnp.concatenate([-jnp.sin(ar), z, -jnp.sin(ac), z], axis=1)
    att_s_dn = jnp.concatenate([z, jnp.sin(ar), z, jnp.sin(ac)], axis=1)
    ret = ((RET_DK // 2,), with_identity(ret_cos, 1.0), (with_identity(ret_sin, 0.0),))
    att = ((LANES - half // 2, half // 2), with_identity(att_cos, 1.0),
           (with_identity(att_s_up, 0.0), with_identity(att_s_dn, 0.0)))
    return ret, att


def _pack_w_in(w):
    rq, rk, rv, rg, sz, sxbc, sdt, aq, ak, av, bg = jnp.split(w, np.cumsum(IN_SPLITS)[:-1].tolist(), axis=1)
    pad = lambda t, n: jnp.pad(t, ((0, 0), (0, n - t.shape[1])))
    packed = jnp.concatenate([rq, rk, aq, ak, rv, av, rg, sz, sxbc,
                              pad(sdt[:, :SSM_HG], LANES), pad(sdt[:, SSM_HG:], 3 * LANES)], axis=1)
    return packed.astype(bf16), bg.astype(bf16)


def _pad_heads(v):
    return jnp.pad(v.astype(f32).reshape(2, SSM_GROUPS, SSM_HG), ((0, 0), (0, 0), (0, LANES - SSM_HG)))


def _geometry(batch, seq, ctx_len):
    tm = 1024
    while seq % tm or (batch * ctx_len) % tm:
        tm //= 2
    tm_merge = min(tm, 512)
    scan_tile = batch * ctx_len
    assert seq % scan_tile == 0 and scan_tile % (2 * CHUNK) == 0 and ctx_len % CHUNK == 0 and seq % GRID_W == 0
    return {"B": batch, "S": seq, "L": ctx_len, "nx": batch * seq, "tm": tm, "tm_merge": tm_merge, "T": scan_tile}


def kernel(x, c, ctx, c_ctx, ada_w, ada_b, mix_norm_pre, mix_norm_post, ffn_norm_pre, ffn_norm_post, w_in, ret_decay, ssm_conv_w, ssm_conv_b, ssm_dt_bias, ssm_a_log, ssm_d, ssm_norm_w, att_sink, w_branch, w_out, ffn_w1, ffn_w3, ffn_w2, moe_router, moe_w1, moe_w3, moe_w2):
    batch, seq, d = x.shape
    ctx_len = ctx.shape[1]
    depth = ada_w.shape[0]
    geom = _geometry(batch, seq, ctx_len)
    nx = geom["nx"]
    ntok = nx + batch * ctx_len

    stream = jnp.concatenate([x.reshape(nx, d), ctx.reshape(batch * ctx_len, d)], axis=0)
    cond = jnp.concatenate([c, c_ctx[None, :], jnp.zeros((SUBLANES - 1 - batch % SUBLANES, d), f32)], axis=0)
    mods_all = adaln_all(cond, ada_w, ada_b).reshape(depth, cond.shape[0], 6, d)
    rope_ret, rope_att = _rope_tables(geom)
    row = lambda v: v.reshape(1, -1)
    dt_cb0 = (2 * D_MODEL + IN_SPLITS[5]) // LANES
    z_cb0 = D_MODEL // SSM_GW
    conv_col0 = 2 * D_MODEL

    for i in range(depth):
        last = i == depth - 1
        rows = nx if last else ntok
        mods = mods_all[i]
        w_packed, w_gate = _pack_w_in(w_in[i])
        npre = row(mix_norm_pre[i])
        ret_qk, att_qk, val, d1 = in_proj(stream, npre, mods, w_packed, rope_ret, rope_att, geom)

        lg = jax.nn.log_sigmoid(ret_decay[i].astype(f32))
        ret_f = ret_scan(lg, ret_qk, val, geom, bwd=False)
        ret_o = ret_scan(lg, ret_qk, val, geom, bwd=True, d1=d1, prev=ret_f)

        u = ssd_conv(d1, ssm_conv_w[i], ssm_conv_b[i], geom, conv_col0)
        par = jnp.stack([_pad_heads(ssm_dt_bias[i]), _pad_heads(-jnp.exp(ssm_a_log[i].astype(f32)))], axis=2)
        skip = row(jnp.repeat(ssm_d[i, 0].astype(f32) + ssm_d[i, 1].astype(f32), SSM_HEAD_DIM))
        ssd_f = ssd_scan(u, d1, par[0], geom, dt_cb0, bwd=False)
        ssd_o = ssd_scan(u, d1, par[1], geom, dt_cb0, bwd=True, z_cb0=z_cb0, skip=skip,
                         norm_w=row(ssm_norm_w[i]), prev=ssd_f)

        att_o = window_attention(att_sink[i].astype(f32), att_qk, val, geom)

        stream = merge(ret_o, ssd_o, att_o, stream, npre, w_gate, w_branch[i].astype(bf16), w_out[i].astype(bf16),
                       row(mix_norm_post[i]), mods, geom, rows)

        j = i // 2
        fpre, fpost = row(ffn_norm_pre[i]), row(ffn_norm_post[i])
        if i % 2 == 0:
            stream = dense_ffn(stream, fpre, fpost, mods, ffn_w1[j].astype(bf16), ffn_w3[j].astype(bf16),
                               ffn_w2[j].astype(bf16), geom, rows)
        else:
            stream = moe_ffn(stream, fpre, fpost, mods, moe_router[j], moe_w1[j].astype(bf16),
                             moe_w3[j].astype(bf16), moe_w2[j].astype(bf16), geom, rows)
    return stream[:nx].reshape(batch, seq, d)
```

```python
import functools
import math

import jax
import jax.numpy as jnp
import numpy as np
from jax import lax
from jax.experimental import pallas as pl
from jax.experimental.pallas import tpu as pltpu

f32 = jnp.float32
bf16 = jnp.bfloat16

D_MODEL = 1024
GRID_W = 64
CHUNK = 128
NORM_EPS = 1e-6
ROPE_BASE = 10000.0
NEG_INF = -1e30
RET_HEADS, RET_DK, RET_DV = 4, 128, 256
SSM_HEADS, SSM_HEAD_DIM, SSM_GROUPS, SSM_STATE = 16, 64, 2, 128
SSM_HG = SSM_HEADS // SSM_GROUPS
SSM_GW = SSM_HG * SSM_HEAD_DIM
ATT_HEADS, ATT_KV_HEADS, ATT_HEAD_DIM = 8, 2, 128
ATT_G = ATT_HEADS // ATT_KV_HEADS
D_FF = 2816
N_EXPERTS = 8
IN_SPLITS = (512, 512, 1024, 1024, 1024, 1536, 16, 1024, 256, 256, 3072)

LANES = 128
SUBLANES = 8
MOE_BLOCK = 256
MOE_DISPATCH_ROWS = 256
MOE_COMBINE_ROWS = 256
VMEM_LIMIT = 56 * 2 ** 20
DMA_ISSUE_UNROLL = 8
ATT_STACK = ATT_G
SSD_CHUNK_GROUP = 4
RET_CHUNK = 256
RET_CHUNK_GROUP = 2
FFN_SPLIT = 2


def _cp(sem, vmem=None):
    return pltpu.CompilerParams(dimension_semantics=sem, vmem_limit_bytes=vmem)


def _silu(v):
    return v / (1.0 + jnp.exp(-v))


def _sigmoid(v):
    return 1.0 / (1.0 + jnp.exp(-v))


def _rms(v, w):
    return v * lax.rsqrt(jnp.mean(v * v, axis=-1, keepdims=True) + NORM_EPS) * w


def _norm_mod(x, nw, m, shift_row, scale_row):
    return _rms(x, nw) * (1.0 + m[scale_row:scale_row + 1]) + m[shift_row:shift_row + 1]


def _dot(a, b):
    return jnp.dot(a, b, preferred_element_type=f32)


def _dot_nt(a, b):
    return lax.dot_general(a, b, (((1,), (1,)), ((), ())), preferred_element_type=f32)


def _split3(a):
    hi = a.astype(bf16)
    r = a - hi.astype(f32)
    mid = r.astype(bf16)
    lo = (r - mid.astype(f32)).astype(bf16)
    return hi, mid, lo


def _row_tile_mod_idx(geom, tm):
    nxt = geom["nx"] // tm
    per_b = geom["S"] // tm
    return lambda i: (jnp.where(i < nxt, i // per_b, geom["B"]), 0, 0)


def _resident(shape):
    return pl.BlockSpec(shape, lambda i: (0,) * len(shape), pipeline_mode=pl.Buffered(1))


def _adaln_body(c_ref, w_ref, b_ref, o_ref):
    s = _silu(c_ref[...])
    o_ref[0] = _dot(s.astype(bf16), w_ref[0].astype(bf16)) + b_ref[0]


def adaln_all(cond, ada_w, ada_b):
    depth, d, n = ada_w.shape
    rows = cond.shape[0]
    tn = 1024
    return pl.pallas_call(
        _adaln_body,
        grid=(depth, n // tn),
        in_specs=[pl.BlockSpec((rows, d), lambda l, j: (0, 0)),
                  pl.BlockSpec((1, d, tn), lambda l, j: (l, 0, j)),
                  pl.BlockSpec((1, 1, tn), lambda l, j: (l, 0, j))],
        out_specs=pl.BlockSpec((1, rows, tn), lambda l, j: (l, 0, j)),
        out_shape=jax.ShapeDtypeStruct((depth, rows, n), f32),
        compiler_params=_cp(("arbitrary", "arbitrary")),
        name="adaln",
    )(cond, ada_w, ada_b.reshape(depth, 1, n))


PROJ_CHUNK = 512


def _proj_plan():
    rq, rk = RET_HEADS * RET_DK, RET_HEADS * RET_DK
    aq, ak = ATT_HEADS * ATT_HEAD_DIM, ATT_KV_HEADS * ATT_HEAD_DIM
    groups = [(0, rq, "ret", 1.0), (0, rk, "ret", RET_DK ** -0.5),
              (1, aq, "att", ATT_HEAD_DIM ** -0.5), (1, ak, "att", 1.0),
              (2, RET_HEADS * RET_DV + ak, None, 1.0),
              (3, 4 * D_MODEL, None, 1.0)]
    plan, wcol, ocol = [], 0, {}
    for out, width, kind, scale in groups:
        done = 0
        while done < width:
            step = min(PROJ_CHUNK, width - done)
            plan.append((wcol, step, out, ocol.get(out, 0), kind, scale))
            wcol += step
            ocol[out] = ocol.get(out, 0) + step
            done += step
    return plan, wcol, [ocol[o] for o in range(4)]


def _proj_body(x_ref, nw_ref, mod_ref, w_ref, rcos_ref, rsin_ref, acos_ref, aup_ref, adn_ref,
               o_ret, o_att, o_val, o_d1, *, plan, ret_shift, att_shifts):
    outs = (o_ret, o_att, o_val, o_d1)
    h = _norm_mod(x_ref[...], nw_ref[...], mod_ref[0], 0, 1).astype(bf16)
    for wcol, width, out, ocol, kind, scale in plan:
        acc = _dot(h, w_ref[:, wcol:wcol + width])
        o_ref = outs[out]
        if kind is None:
            o_ref[:, ocol:ocol + width] = acc.astype(o_ref.dtype)
            continue
        for c in range(width // LANES):
            t = acc[:, c * LANES:(c + 1) * LANES]
            if kind == "ret":
                r = t * rcos_ref[...] + pltpu.roll(t, ret_shift, 1) * rsin_ref[...]
            else:
                r = (t * acos_ref[...] + pltpu.roll(t, att_shifts[0], 1) * aup_ref[...]
                     + pltpu.roll(t, att_shifts[1], 1) * adn_ref[...])
            o_ref[:, ocol + c * LANES:ocol + (c + 1) * LANES] = (r * scale).astype(o_ref.dtype)


def in_proj(x, nw, mods, w, rope_ret, rope_att, geom):
    ntok, d = x.shape
    tm = geom["tm_merge"]
    plan, wcols, widths = _proj_plan()
    assert w.shape == (d, wcols)
    nxt = geom["nx"] // tm
    per_b = geom["S"] // tm
    pos_idx = lambda i: (jnp.where(i < nxt, i % per_b, per_b), 0)
    table = pl.BlockSpec((tm, LANES), pos_idx)
    dtypes = (bf16, bf16, bf16, f32)
    return pl.pallas_call(
        functools.partial(_proj_body, plan=plan, ret_shift=rope_ret[0][0], att_shifts=rope_att[0]),
        grid=(ntok // tm,),
        in_specs=[pl.BlockSpec((tm, d), lambda i: (i, 0)),
                  pl.BlockSpec((1, d), lambda i: (0, 0)),
                  pl.BlockSpec((1, 6, d), _row_tile_mod_idx(geom, tm)),
                  _resident((d, wcols)),
                  table, table, table, table, table],
        out_specs=[pl.BlockSpec((tm, n), lambda i: (i, 0)) for n in widths],
        out_shape=[jax.ShapeDtypeStruct((ntok, n), dt) for n, dt in zip(widths, dtypes)],
        compiler_params=_cp(("arbitrary",), VMEM_LIMIT),
        name="in_proj",
    )(x, nw, mods, w, rope_ret[1], rope_ret[2][0], rope_att[1], rope_att[2][0], rope_att[2][1])


def _conv_body(x_ref, prev_ref, next_ref, w_ref, b_ref, o_ref, *, tiles_per_seq, nx_tiles, ctx_len):
    i = pl.program_id(0)
    x = x_ref[...]
    rows = x.shape[0]
    t_in = i % tiles_per_seq
    is_x = i < nx_tiles
    use_prev = jnp.logical_and(is_x, t_in > 0)
    use_next = jnp.logical_and(is_x, t_in < tiles_per_seq - 1)
    prow = jnp.where(use_prev, prev_ref[SUBLANES - 1:SUBLANES, :], 0.0)
    nrow = jnp.where(use_next, next_ref[0:1, :], 0.0)
    rid = lax.broadcasted_iota(jnp.int32, x.shape, 0)
    xm = jnp.where(rid == 0, prow, pltpu.roll(x, 1, 0))
    xp = jnp.where(rid == rows - 1, nrow, pltpu.roll(x, rows - 1, 0))
    seg = lax.rem(rid, ctx_len)
    is_ctx = jnp.logical_not(is_x)
    xm = jnp.where(jnp.logical_and(is_ctx, seg == 0), 0.0, xm)
    xp = jnp.where(jnp.logical_and(is_ctx, seg == ctx_len - 1), 0.0, xp)
    w = w_ref[...]
    o_ref[...] = _silu(xm * w[0:1] + x * w[1:2] + xp * w[2:3] + b_ref[...])


def ssd_conv(d1, conv_w, conv_b, geom, col0):
    ntok = d1.shape[0]
    width = conv_w.shape[1]
    tc = 512
    cb0 = col0 // tc
    tr = geom["T"]
    per8 = tr // SUBLANES
    last8 = ntok // SUBLANES - 1
    return pl.pallas_call(
        functools.partial(_conv_body, tiles_per_seq=geom["S"] // tr, nx_tiles=geom["nx"] // tr, ctx_len=geom["L"]),
        grid=(ntok // tr, width // tc),
        in_specs=[pl.BlockSpec((tr, tc), lambda i, j: (i, cb0 + j)),
                  pl.BlockSpec((SUBLANES, tc), lambda i, j: (jnp.maximum(i * per8 - 1, 0), cb0 + j)),
                  pl.BlockSpec((SUBLANES, tc), lambda i, j: (jnp.minimum((i + 1) * per8, last8), cb0 + j)),
                  pl.BlockSpec((3, tc), lambda i, j: (0, j)),
                  pl.BlockSpec((1, tc), lambda i, j: (0, j))],
        out_specs=pl.BlockSpec((tr, tc), lambda i, j: (i, j)),
        out_shape=jax.ShapeDtypeStruct((ntok, width), f32),
        compiler_params=_cp(("arbitrary", "arbitrary")),
        name="ssd_conv",
    )(d1, d1, d1, conv_w, conv_b.reshape(1, width))


def _scan_row_index(geom, bwd):
    tile = geom["T"]
    per_b = geom["S"] // tile
    ctx_tile = geom["nx"] // tile

    def idx(s):
        t = s - 1
        xt = (t // per_b) * per_b + (per_b - 1 - t % per_b) if bwd else t
        return jnp.where(s == 0, ctx_tile, xt)

    return idx, ctx_tile + 1


def _scan_driver(chunk, s_all, *, bwd, geom, group, rows=CHUNK):
    tile, ctx_len, nbatch = geom["T"], geom["L"], geom["B"]
    per_b = geom["S"] // tile
    s = pl.program_id(1)

    @pl.when(s == 0)
    def _():
        for b in range(nbatch):
            state = jnp.zeros(s_all.shape[1:], f32)
            cs = range(ctx_len // rows)
            for ci in (reversed(cs) if bwd else cs):
                state = chunk(b * ctx_len + ci * rows, state)
            s_all[b] = state

    @pl.when(s > 0)
    def _():
        st = s_all.at[(s - 1) // per_b]
        n_groups = tile // (group * rows)
        order = tuple(reversed(range(group))) if bwd else tuple(range(group))

        def body(p, state):
            base = ((n_groups - 1 - p) if bwd else p) * (group * rows)
            if not isinstance(base, int):
                base = pl.multiple_of(base, group * rows)
            for ci in order:
                state = chunk(base + ci * rows, state)
            return state

        st[...] = body(0, st[...]) if n_groups == 1 else lax.fori_loop(0, n_groups, body, st[...])


def _expand_heads(a, lane):
    rows = a.shape[0]
    lane = lane[:rows]
    cols = []
    for v in range(SSM_HG // 2):
        left = jnp.broadcast_to(a[:, 2 * v:2 * v + 1], (rows, LANES))
        right = jnp.broadcast_to(a[:, 2 * v + 1:2 * v + 2], (rows, LANES))
        cols.append(jnp.where(lane < SSM_HEAD_DIM, left, right))
    return jnp.concatenate(cols, axis=1)


def _ssd_body(*refs, bwd, geom):
    if bwd:
        xs_ref, b_ref, c_ref, dt_ref, par_ref, z_ref, skip_ref, nw_ref, prev_ref, o_ref, s_all = refs
    else:
        xs_ref, b_ref, c_ref, dt_ref, par_ref, o_ref, s_all = refs

    par = par_ref[0]
    ii = lax.broadcasted_iota(jnp.int32, (CHUNK, CHUNK), 0)
    jj = lax.broadcasted_iota(jnp.int32, (CHUNK, CHUNK), 1)
    tri_mask = (ii <= jj) if bwd else (ii >= jj)
    tri = jnp.where(tri_mask, 1.0, 0.0).astype(bf16)
    lane = jj
    last = 0 if bwd else CHUNK - 1
    pair_keep = [jnp.where((lane < SSM_HEAD_DIM) == (a == 0), 1.0, 0.0).astype(bf16) for a in range(2)]

    def chunk(r0, state):
        sl = pl.ds(r0, CHUNK)
        pre = dt_ref[sl, :] + par[0:1]
        dt = jnp.maximum(pre, 0.0) + jnp.log(1.0 + jnp.exp(-jnp.abs(pre)))
        hi, mid, lo = _split3(dt * par[1:2])
        cum = _dot(tri, hi) + _dot(tri, mid) + _dot(tri, lo)
        tot = cum[last:last + 1]
        cum_t = cum.T[0:SUBLANES]
        dt_t = dt.T[0:SUBLANES]
        dtws_t = dt_t * jnp.exp(cum_t[:, last:last + 1] - cum_t)
        dec_x = _expand_heads(jnp.broadcast_to(jnp.exp(tot), (SUBLANES, LANES)), lane)[0:1]

        xs = xs_ref[sl, :]
        xs_b = xs.astype(bf16)
        bm = b_ref[sl, :]
        cm = c_ref[sl, :]
        cb = _dot_nt(cm.astype(bf16), bm.astype(bf16))
        bm_t = bm.T
        state_b = state.astype(bf16)
        inter_all = _dot(cm.astype(bf16), state_b)
        y_parts, upd_parts = [], []
        for v in range(SSM_HG // 2):
            ps = slice(v * LANES, (v + 1) * LANES)
            y_p = upd_p = None
            row_scale = []
            for a in range(2):
                h = 2 * v + a
                xs_h = xs_b[:, ps] * pair_keep[a]
                cbc = jnp.broadcast_to(cum[:, h:h + 1], (CHUNK, CHUNK))
                lmat = jnp.exp(jnp.where(tri_mask, cbc - cum_t[h:h + 1, :], NEG_INF))
                row_scale.append(jnp.exp(cbc))
                t = _dot((cb * lmat * dt_t[h:h + 1, :]).astype(bf16), xs_h)
                y_p = t if y_p is None else y_p + t
                t = _dot((bm_t * dtws_t[h:h + 1, :]).astype(bf16), xs_h)
                upd_p = t if upd_p is None else upd_p + t
            y_parts.append(y_p + inter_all[:, ps] * jnp.where(lane < SSM_HEAD_DIM, row_scale[0], row_scale[1]))
            upd_parts.append(upd_p)
        y = jnp.concatenate(y_parts, axis=1)
        if bwd:
            y = y + prev_ref[sl, :] + xs * skip_ref[...]
            gz = y * _silu(z_ref[sl, :])
            o_ref[sl, :] = _rms(gz, nw_ref[...]).astype(o_ref.dtype)
        else:
            o_ref[sl, :] = y
        return dec_x * state + jnp.concatenate(upd_parts, axis=1)

    _scan_driver(chunk, s_all, bwd=bwd, geom=geom, group=SSD_CHUNK_GROUP)


def ssd_scan(u, d1, par, geom, dt_cb0, bwd, z_cb0=None, skip=None, norm_w=None, prev=None):
    ntok = u.shape[0]
    idx, steps = _scan_row_index(geom, bwd)
    gw = SSM_GW
    nb_blk = SSM_HEADS * SSM_HEAD_DIM // SSM_STATE
    tile = lambda w, cb: pl.BlockSpec((geom["T"], w), lambda g, s, cb=cb: (idx(s), cb(g)))
    in_specs = [tile(gw, lambda g: g),
                tile(SSM_STATE, lambda g: nb_blk + g),
                tile(SSM_STATE, lambda g: nb_blk + SSM_GROUPS + g),
                tile(LANES, lambda g: dt_cb0 + g),
                pl.BlockSpec((1, 2, LANES), lambda g, s: (g, 0, 0))]
    args = [u, u, u, d1, par]
    if bwd:
        in_specs += [tile(gw, lambda g: z_cb0 + g),
                     pl.BlockSpec((1, gw), lambda g, s: (0, g)),
                     pl.BlockSpec((1, gw), lambda g, s: (0, g)),
                     tile(gw, lambda g: g)]
        args += [d1, skip, norm_w, prev]
    return pl.pallas_call(
        functools.partial(_ssd_body, bwd=bwd, geom=geom),
        grid=(SSM_GROUPS, steps),
        in_specs=in_specs,
        out_specs=tile(gw, lambda g: g),
        out_shape=jax.ShapeDtypeStruct((ntok, SSM_GROUPS * gw), bf16 if bwd else f32),
        scratch_shapes=[pltpu.VMEM((geom["B"], SSM_STATE, gw), f32)],
        compiler_params=_cp(("arbitrary", "arbitrary"), VMEM_LIMIT),
        name="ssd_bwd" if bwd else "ssd_fwd",
    )(*args)


def _ret_body(*refs, bwd, geom):
    if bwd:
        lg_ref, q_ref, k_ref, v_ref, g_ref, prev_ref, o_ref, s_all = refs
    else:
        lg_ref, q_ref, k_ref, v_ref, o_ref, s_all = refs

    ii = lax.broadcasted_iota(jnp.int32, (RET_CHUNK, RET_CHUNK), 0).astype(f32)
    jj = lax.broadcasted_iota(jnp.int32, (RET_CHUNK, RET_CHUNK), 1).astype(f32)
    diff = (jj - ii) if bwd else (ii - jj)
    icol = ii[:, 0:1]
    jrow = jj[0:1, :]
    consts = []
    for h in range(RET_HEADS):
        lg = lg_ref[1 if bwd else 0, h]
        dmat = jnp.where(diff >= 0, jnp.exp(jnp.maximum(diff, 0.0) * lg), 0.0)
        if bwd:
            wq = jnp.exp((RET_CHUNK - icol) * lg)
            ws = jnp.exp(jrow * lg)
        else:
            wq = jnp.exp((icol + 1.0) * lg)
            ws = jnp.exp((RET_CHUNK - 1.0 - jrow) * lg)
        consts.append((dmat, wq, ws, jnp.exp(jnp.full((1, 1), float(RET_CHUNK), f32) * lg)))

    def chunk(r0, state):
        sl = pl.ds(r0, RET_CHUNK)
        new_state = []
        for h, (dmat, wq, ws, decay) in enumerate(consts):
            ks = slice(h * RET_DK, (h + 1) * RET_DK)
            vs = slice(h * RET_DV, (h + 1) * RET_DV)
            q = q_ref[sl, ks]
            k = k_ref[sl, ks]
            v = v_ref[sl, vs]
            sc = _dot_nt(q, k) * dmat
            o = _dot(sc.astype(bf16), v) + _dot(q, state[h].astype(bf16)) * wq
            kt = (k.astype(f32).T * ws).astype(bf16)
            if bwd:
                o = o + prev_ref[sl, vs]
                o = o * lax.rsqrt(jnp.mean(o * o, axis=-1, keepdims=True) + NORM_EPS)
                o_ref[sl, vs] = (_silu(g_ref[sl, vs]) * o).astype(o_ref.dtype)
            else:
                o_ref[sl, vs] = o
            new_state.append(decay * state[h] + _dot(kt, v))
        return jnp.stack(new_state)

    _scan_driver(chunk, s_all, bwd=bwd, geom=geom, group=RET_CHUNK_GROUP, rows=RET_CHUNK)


def ret_scan(lg, qk, vb, geom, bwd, d1=None, prev=None):
    ntok = qk.shape[0]
    idx, steps = _scan_row_index(geom, bwd)
    kw, vw = RET_HEADS * RET_DK, RET_HEADS * RET_DV
    tile = lambda w, cb: pl.BlockSpec((geom["T"], w), lambda z, s: (idx(s), cb))
    in_specs = [pl.BlockSpec(memory_space=pltpu.SMEM), tile(kw, 0), tile(kw, 1), tile(vw, 0)]
    args = [lg, qk, qk, vb]
    if bwd:
        in_specs += [tile(vw, 0), tile(vw, 0)]
        args += [d1, prev]
    return pl.pallas_call(
        functools.partial(_ret_body, bwd=bwd, geom=geom),
        grid=(1, steps),
        in_specs=in_specs,
        out_specs=tile(vw, 0),
        out_shape=jax.ShapeDtypeStruct((ntok, vw), bf16 if bwd else f32),
        scratch_shapes=[pltpu.VMEM((geom["B"], RET_HEADS, RET_DK, RET_DV), f32)],
        compiler_params=_cp(("arbitrary", "arbitrary"), VMEM_LIMIT),
        name="ret_bwd" if bwd else "ret_fwd",
    )(*args)


def _att_body(sink_ref, q_ref, kc_ref, vc_ref, kp_ref, ko_ref, kn_ref, vp_ref, vo_ref, vn_ref, o_ref, *, nblk, ctx_blk):
    n = pl.program_id(1) - ctx_blk
    own_ok = n >= 0
    prev_ok = n >= 1
    next_ok = jnp.logical_and(n >= 0, n <= nblk - 2)
    rows = ATT_STACK * CHUNK
    qi = lax.rem(lax.broadcasted_iota(jnp.int32, (rows, CHUNK), 0), CHUNK)
    kj = lax.broadcasted_iota(jnp.int32, (rows, CHUNK), 1)
    m_prev = jnp.logical_and(kj >= qi, prev_ok)
    m_next = jnp.logical_and(kj <= qi, next_ok)
    hd = ATT_HEAD_DIM
    for h0 in range(0, ATT_HEADS, ATT_STACK):
        kv = h0 // ATT_G
        ks = slice(kv * hd, (kv + 1) * hd)
        kc, vc = kc_ref[:, ks], vc_ref[:, ks]
        heads = list(range(h0, h0 + ATT_STACK))
        q = jnp.concatenate([q_ref[:, h * hd:(h + 1) * hd] for h in heads], axis=0)
        sk = jnp.concatenate([jnp.full((CHUNK, 1), sink_ref[h], f32) for h in heads], axis=0)
        s_c = _dot_nt(q, kc)
        s_p = jnp.where(m_prev, _dot_nt(q, kp_ref[:, ks]), NEG_INF)
        s_o = jnp.where(own_ok, _dot_nt(q, ko_ref[:, ks]), NEG_INF)
        s_n = jnp.where(m_next, _dot_nt(q, kn_ref[:, ks]), NEG_INF)
        parts = [s_c[:, c * CHUNK:(c + 1) * CHUNK] for c in range(s_c.shape[1] // CHUNK)] + [s_p, s_o, s_n]
        mx = jnp.maximum(jnp.max(functools.reduce(jnp.maximum, parts), axis=-1, keepdims=True), sk)
        probs = [jnp.exp(t - mx) for t in parts]
        den = jnp.exp(sk - mx) + jnp.sum(functools.reduce(jnp.add, probs), axis=-1, keepdims=True)
        n_c = len(parts) - 3
        p_c = jnp.concatenate(probs[:n_c], axis=1)
        o = (_dot(p_c.astype(bf16), vc) + _dot(probs[n_c].astype(bf16), vp_ref[:, ks])
             + _dot(probs[n_c + 1].astype(bf16), vo_ref[:, ks]) + _dot(probs[n_c + 2].astype(bf16), vn_ref[:, ks]))
        o = o / den
        for g, h in enumerate(heads):
            o_ref[:, h * hd:(h + 1) * hd] = o[g * CHUNK:(g + 1) * CHUNK].astype(o_ref.dtype)


def window_attention(sink, qk, vb, geom):
    ntok = qk.shape[0]
    blk = CHUNK
    nblk = geom["S"] // blk
    ctx_blk = geom["L"] // blk
    ctx0 = geom["nx"] // blk
    kvw = ATT_KV_HEADS * ATT_HEAD_DIM
    qw = ATT_HEADS * ATT_HEAD_DIM
    kcol = qw // kvw

    def q_idx(b, i):
        return jnp.where(i < ctx_blk, ctx0 + b * ctx_blk + i, b * nblk + i - ctx_blk)

    def win(off):
        def f(b, i):
            n = jnp.clip(i - ctx_blk + off, 0, nblk - 1)
            return (b * nblk + n, kcol)
        return f

    ctx_spec = pl.BlockSpec((geom["L"], kvw), lambda b, i: (geom["nx"] // geom["L"] + b, kcol))
    wspec = lambda off: pl.BlockSpec((blk, kvw), win(off))
    return pl.pallas_call(
        functools.partial(_att_body, nblk=nblk, ctx_blk=ctx_blk),
        grid=(geom["B"], ctx_blk + nblk),
        in_specs=[pl.BlockSpec(memory_space=pltpu.SMEM),
                  pl.BlockSpec((blk, qw), lambda b, i: (q_idx(b, i), 0)),
                  ctx_spec, ctx_spec,
                  wspec(-1), wspec(0), wspec(1), wspec(-1), wspec(0), wspec(1)],
        out_specs=pl.BlockSpec((blk, qw), lambda b, i: (q_idx(b, i), 0)),
        out_shape=jax.ShapeDtypeStruct((ntok, qw), bf16),
        compiler_params=_cp(("arbitrary", "arbitrary")),
        name="window_attention",
    )(sink, qk, qk, vb, qk, qk, qk, vb, vb, vb)


def _merge_body(r_ref, s_ref, a_ref, x_ref, npre_ref, wg_ref, wb_ref, wo_ref, nw_ref, mod_ref, o_ref):
    x = x_ref[...]
    m = mod_ref[0]
    h = _norm_mod(x, npre_ref[...], m, 0, 1).astype(bf16)
    acc = None
    for n, ref in enumerate((r_ref, s_ref, a_ref)):
        gate = _dot(h, wg_ref[:, n * D_MODEL:(n + 1) * D_MODEL])
        t = _sigmoid(gate) * _dot(ref[...], wb_ref[n])
        acc = t if acc is None else acc + t
    y = _dot(acc.astype(bf16), wo_ref[...])
    o_ref[...] = x + m[2:3] * _rms(y, nw_ref[...])


def merge(ret_o, ssd_o, att_o, x, npre, wg, wb, wo, nw, mods, geom, rows):
    d = D_MODEL
    tm = geom["tm_merge"]
    row = lambda w: pl.BlockSpec((tm, w), lambda i: (i, 0))
    return pl.pallas_call(
        _merge_body,
        grid=(rows // tm,),
        in_specs=[row(d), row(d), row(d), row(d),
                  pl.BlockSpec((1, d), lambda i: (0, 0)),
                  _resident((d, 3 * d)), _resident((3, d, d)), _resident((d, d)),
                  pl.BlockSpec((1, d), lambda i: (0, 0)),
                  pl.BlockSpec((1, 6, d), _row_tile_mod_idx(geom, tm))],
        out_specs=row(d),
        out_shape=jax.ShapeDtypeStruct((rows, d), f32),
        compiler_params=_cp(("arbitrary",), VMEM_LIMIT),
        name="merge",
    )(ret_o, ssd_o, att_o, x, npre, wg, wb, wo, nw, mods)


def _swiglu(h, w1, w3, w2, f_split):
    step = w1.shape[1] // f_split
    acc = None
    for f in range(f_split):
        fs = slice(f * step, (f + 1) * step)
        act = _silu(_dot(h, w1[:, fs])) * _dot(h, w3[:, fs])
        t = _dot(act.astype(bf16), w2[fs, :])
        acc = t if acc is None else acc + t
    return acc


def _ffn_body(x_ref, npre_ref, npost_ref, mod_ref, w1_ref, w3_ref, w2_ref, o_ref):
    x = x_ref[...]
    m = mod_ref[0]
    h = _norm_mod(x, npre_ref[...], m, 3, 4).astype(bf16)
    y = _swiglu(h, w1_ref, w3_ref, w2_ref, FFN_SPLIT)
    o_ref[...] = x + m[5:6] * _rms(y, npost_ref[...])


def dense_ffn(x, npre, npost, mods, w1, w3, w2, geom, rows):
    d = D_MODEL
    ff = w1.shape[1]
    tm = geom["tm_merge"]
    return pl.pallas_call(
        _ffn_body,
        grid=(rows // tm,),
        in_specs=[pl.BlockSpec((tm, d), lambda i: (i, 0)),
                  pl.BlockSpec((1, d), lambda i: (0, 0)),
                  pl.BlockSpec((1, d), lambda i: (0, 0)),
                  pl.BlockSpec((1, 6, d), _row_tile_mod_idx(geom, tm)),
                  _resident((d, ff)), _resident((d, ff)), _resident((ff, d))],
        out_specs=pl.BlockSpec((tm, d), lambda i: (i, 0)),
        out_shape=jax.ShapeDtypeStruct((rows, d), f32),
        compiler_params=_cp(("arbitrary",), VMEM_LIMIT),
        name="dense_ffn",
    )(x, npre, npost, mods, w1, w3, w2)


def _router_body(x_ref, npre_ref, mod_ref, wr_ref, h_ref, route_ref):
    h = _norm_mod(x_ref[...], npre_ref[...], mod_ref[0], 3, 4)
    h_ref[...] = h
    h_hi = h.astype(bf16)
    h_lo = (h - h_hi.astype(f32)).astype(bf16)
    wr = wr_ref[...]
    w_hi = wr.astype(bf16)
    w_lo = (wr - w_hi.astype(f32)).astype(bf16)
    logits = _dot(h_hi, w_hi) + _dot(h_hi, w_lo) + _dot(h_lo, w_hi) + _dot(h_lo, w_lo)
    lane = lax.broadcasted_iota(jnp.int32, logits.shape, 1).astype(f32)
    l1 = jnp.where(lane < N_EXPERTS, logits, -jnp.inf)
    m1 = jnp.max(l1, axis=-1, keepdims=True)
    i1 = jnp.min(jnp.where(l1 == m1, lane, float(LANES)), axis=-1, keepdims=True)
    l2 = jnp.where(lane == i1, -jnp.inf, l1)
    m2 = jnp.max(l2, axis=-1, keepdims=True)
    i2 = jnp.min(jnp.where(l2 == m2, lane, float(LANES)), axis=-1, keepdims=True)
    e = jnp.exp(m2 - m1)
    w1 = 1.0 / (1.0 + e)
    w2 = e / (1.0 + e)
    route_ref[...] = jnp.where(lane == 0.0, i1, jnp.where(lane == 1.0, i2,
                                                          jnp.where(lane == 2.0, w1, jnp.where(lane == 3.0, w2, 0.0))))


def moe_router(x, npre, mods, w_router, geom, rows):
    d = D_MODEL
    tm = geom["tm_merge"]
    wr = jnp.pad(w_router, ((0, 0), (0, LANES - w_router.shape[1])))
    return pl.pallas_call(
        _router_body,
        grid=(rows // tm,),
        in_specs=[pl.BlockSpec((tm, d), lambda i: (i, 0)),
                  pl.BlockSpec((1, d), lambda i: (0, 0)),
                  pl.BlockSpec((1, 6, d), _row_tile_mod_idx(geom, tm)),
                  pl.BlockSpec((d, LANES), lambda i: (0, 0))],
        out_specs=[pl.BlockSpec((tm, d), lambda i: (i, 0)), pl.BlockSpec((tm, LANES), lambda i: (i, 0))],
        out_shape=[jax.ShapeDtypeStruct((rows, d), f32), jax.ShapeDtypeStruct((rows, LANES), f32)],
        compiler_params=_cp(("arbitrary",)),
        name="moe_router",
    )(x, npre, mods, wr)


def _row_copy(src, src_row, dst, dst_row, sem):
    return pltpu.make_async_copy(src.at[pl.ds(src_row, 1)], dst.at[pl.ds(dst_row, 1)], sem)


def _dispatch_body(slot_ref, h_ref, xb_in, xb_hbm, sem, *, rows):
    del xb_in

    def start(r, c):
        _row_copy(h_ref, r, xb_hbm, slot_ref[0, 0, 2 * r], sem).start()
        _row_copy(h_ref, r, xb_hbm, slot_ref[0, 0, 2 * r + 1], sem).start()
        return c

    lax.fori_loop(0, rows, start, 0, unroll=DMA_ISSUE_UNROLL)
    for _ in range(2):
        pltpu.make_async_copy(h_ref, xb_hbm.at[pl.ds(0, rows)], sem).wait()


def moe_dispatch(h, slots, n_slots):
    n_tok, d = h.shape
    rows = MOE_DISPATCH_ROWS
    steps = n_tok // rows
    return pl.pallas_call(
        functools.partial(_dispatch_body, rows=rows),
        grid=(steps,),
        in_specs=[pl.BlockSpec((1, 1, 2 * rows), lambda i: (i, 0, 0), memory_space=pltpu.SMEM),
                  pl.BlockSpec((rows, d), lambda i: (i, 0)),
                  pl.BlockSpec(memory_space=pl.ANY)],
        out_specs=pl.BlockSpec(memory_space=pl.ANY),
        out_shape=jax.ShapeDtypeStruct((n_slots, d), f32),
        scratch_shapes=[pltpu.SemaphoreType.DMA(())],
        input_output_aliases={2: 0},
        compiler_params=_cp(("arbitrary",)),
        name="moe_dispatch",
    )(slots.reshape(steps, 1, 2 * rows), h, jnp.zeros((n_slots, d), f32))


def _expert_body(be_ref, cnt_ref, used_ref, x_ref, w1_ref, w3_ref, w2_ref, o_ref):
    del be_ref, used_ref
    i = pl.program_id(0)

    @pl.when(cnt_ref[i] > 0)
    def _():
        o_ref[...] = _swiglu(x_ref[...].astype(bf16), w1_ref.at[0], w3_ref.at[0], w2_ref.at[0], FFN_SPLIT)

    @pl.when(cnt_ref[i] == 0)
    def _():
        o_ref[...] = jnp.zeros_like(o_ref)


def moe_experts(xb, blk_e, blk_cnt, n_used, w1, w3, w2):
    n_slots, d = xb.shape
    ff = w1.shape[2]
    tm = MOE_BLOCK
    grid_spec = pltpu.PrefetchScalarGridSpec(
        num_scalar_prefetch=3,
        grid=(n_slots // tm,),
        in_specs=[pl.BlockSpec((tm, d), lambda i, be, cnt, used: (jnp.minimum(i, used[0] - 1), 0)),
                  pl.BlockSpec((1, d, ff), lambda i, be, cnt, used: (be[i], 0, 0)),
                  pl.BlockSpec((1, d, ff), lambda i, be, cnt, used: (be[i], 0, 0)),
                  pl.BlockSpec((1, ff, d), lambda i, be, cnt, used: (be[i], 0, 0))],
        out_specs=pl.BlockSpec((tm, d), lambda i, be, cnt, used: (i, 0)),
    )
    return pl.pallas_call(
        _expert_body,
        grid_spec=grid_spec,
        out_shape=jax.ShapeDtypeStruct((n_slots, d), f32),
        compiler_params=_cp(("arbitrary",), VMEM_LIMIT),
        name="moe_experts",
    )(blk_e, blk_cnt, n_used, xb, w1, w3, w2)


def _combine_body(slot_ref, route_ref, x_ref, npost_ref, mod_ref, yb_hbm, o_ref, y0_scr, y1_scr, sem, *, rows):
    def start(r, c):
        _row_copy(yb_hbm, slot_ref[0, 0, 2 * r], y0_scr, r, sem).start()
        _row_copy(yb_hbm, slot_ref[0, 0, 2 * r + 1], y1_scr, r, sem).start()
        return c

    lax.fori_loop(0, rows, start, 0, unroll=DMA_ISSUE_UNROLL)
    for buf in (y0_scr, y1_scr):
        pltpu.make_async_copy(yb_hbm.at[pl.ds(0, rows)], buf, sem).wait()
    y = y0_scr[...] * route_ref[:, 2:3] + y1_scr[...] * route_ref[:, 3:4]
    o_ref[...] = x_ref[...] + mod_ref[0][5:6] * _rms(y, npost_ref[...])


def moe_combine(slots, route, x, npost, mods, yb, geom, rows_total):
    d = D_MODEL
    rows = MOE_COMBINE_ROWS
    steps = rows_total // rows
    return pl.pallas_call(
        functools.partial(_combine_body, rows=rows),
        grid=(steps,),
        in_specs=[pl.BlockSpec((1, 1, 2 * rows), lambda i: (i, 0, 0), memory_space=pltpu.SMEM),
                  pl.BlockSpec((rows, LANES), lambda i: (i, 0)),
                  pl.BlockSpec((rows, d), lambda i: (i, 0)),
                  pl.BlockSpec((1, d), lambda i: (0, 0)),
                  pl.BlockSpec((1, 6, d), _row_tile_mod_idx(geom, rows)),
                  pl.BlockSpec(memory_space=pl.ANY)],
        out_specs=pl.BlockSpec((rows, d), lambda i: (i, 0)),
        out_shape=jax.ShapeDtypeStruct((rows_total, d), f32),
        scratch_shapes=[pltpu.VMEM((rows, d), f32), pltpu.VMEM((rows, d), f32), pltpu.SemaphoreType.DMA(())],
        compiler_params=_cp(("arbitrary",)),
        name="moe_combine",
    )(slots.reshape(steps, 1, 2 * rows), route, x, npost, mods, yb)


def _moe_slots(route, n_tok):
    tm = MOE_BLOCK
    flat_e = route[:, 0:2].astype(jnp.int32).reshape(2 * n_tok)
    onehot = (flat_e[:, None] == jnp.arange(N_EXPERTS, dtype=jnp.int32)[None, :]).astype(jnp.int32)
    csum = jnp.cumsum(onehot, axis=0)
    counts = csum[-1]
    rank = jnp.sum(csum * onehot, axis=1) - 1
    padded = (counts + tm - 1) // tm * tm
    pend = jnp.cumsum(padded)
    pstart = pend - padded
    slots = jnp.sum(onehot * pstart[None, :], axis=1) + rank
    n_blocks = 2 * n_tok // tm + N_EXPERTS
    blk = jnp.arange(n_blocks, dtype=jnp.int32)
    blk_e = jnp.minimum(jnp.sum((blk[:, None] >= (pend // tm)[None, :]).astype(jnp.int32), axis=1), N_EXPERTS - 1)
    n_used = (pend[-1] // tm).astype(jnp.int32)
    cnt = jnp.clip(counts[blk_e] - (blk - pstart[blk_e] // tm) * tm, 0, tm)
    blk_cnt = jnp.where(blk < n_used, cnt, 0).astype(jnp.int32)
    return slots.astype(jnp.int32), blk_e, blk_cnt, n_used.reshape(1), n_blocks * tm


def moe_ffn(x, npre, npost, mods, w_router, w1, w3, w2, geom, rows):
    h, route = moe_router(x, npre, mods, w_router, geom, rows)
    slots, blk_e, blk_cnt, n_used, n_slots = _moe_slots(route, rows)
    xb = moe_dispatch(h, slots, n_slots)
    yb = moe_experts(xb, blk_e, blk_cnt, n_used, w1, w3, w2)
    return moe_combine(slots, route, x, npost, mods, yb, geom, rows)


def _rope_tables(geom):
    s, tm = geom["S"], geom["tm"]
    pos = jnp.arange(s)

    def angles(p, dim):
        inv = ROPE_BASE ** (-jnp.arange(0, dim, 2, dtype=f32) / dim)
        return p.astype(f32)[:, None] * inv[None, :]

    def with_identity(t, one):
        return jnp.concatenate([t, jnp.full((tm, LANES), one, f32)], axis=0)

    a = angles(pos, RET_DK)
    ret_cos = jnp.concatenate([jnp.cos(a), jnp.cos(a)], axis=1)
    ret_sin = jnp.concatenate([-jnp.sin(a), jnp.sin(a)], axis=1)
    half = ATT_HEAD_DIM // 2
    ar = angles(pos // GRID_W, half)
    ac = angles(pos % GRID_W, half)
    z = jnp.zeros_like(ar)
    att_cos = jnp.concatenate([jnp.cos(ar), jnp.cos(ar), jnp.cos(ac), jnp.cos(ac)], axis=1)
    att_s_up = jnp.concatenate([-jnp.sin(ar), z, -jnp.sin(ac), z], axis=1)
    att_s_dn = jnp.concatenate([z, jnp.sin(ar), z, jnp.sin(ac)], axis=1)
    ret = ((RET_DK // 2,), with_identity(ret_cos, 1.0), (with_identity(ret_sin, 0.0),))
    att = ((LANES - half // 2, half // 2), with_identity(att_cos, 1.0),
           (with_identity(att_s_up, 0.0), with_identity(att_s_dn, 0.0)))
    return ret, att


def _pack_w_in(w):
    rq, rk, rv, rg, sz, sxbc, sdt, aq, ak, av, bg = jnp.split(w, np.cumsum(IN_SPLITS)[:-1].tolist(), axis=1)
    pad = lambda t, n: jnp.pad(t, ((0, 0), (0, n - t.shape[1])))
    packed = jnp.concatenate([rq, rk, aq, ak, rv, av, rg, sz, sxbc,
                              pad(sdt[:, :SSM_HG], LANES), pad(sdt[:, SSM_HG:], 3 * LANES)], axis=1)
    return packed.astype(bf16), bg.astype(bf16)


def _pad_heads(v):
    return jnp.pad(v.astype(f32).reshape(2, SSM_GROUPS, SSM_HG), ((0, 0), (0, 0), (0, LANES - SSM_HG)))


def _geometry(batch, seq, ctx_len):
    tm = 1024
    while seq % tm or (batch * ctx_len) % tm:
        tm //= 2
    tm_merge = min(tm, 512)
    scan_tile = batch * ctx_len
    assert seq % scan_tile == 0 and scan_tile % (SSD_CHUNK_GROUP * CHUNK) == 0 and scan_tile % (RET_CHUNK_GROUP * RET_CHUNK) == 0 and ctx_len % RET_CHUNK == 0 and seq % GRID_W == 0
    return {"B": batch, "S": seq, "L": ctx_len, "nx": batch * seq, "tm": tm, "tm_merge": tm_merge, "T": scan_tile}


def kernel(x, c, ctx, c_ctx, ada_w, ada_b, mix_norm_pre, mix_norm_post, ffn_norm_pre, ffn_norm_post, w_in, ret_decay, ssm_conv_w, ssm_conv_b, ssm_dt_bias, ssm_a_log, ssm_d, ssm_norm_w, att_sink, w_branch, w_out, ffn_w1, ffn_w3, ffn_w2, moe_router, moe_w1, moe_w3, moe_w2):
    batch, seq, d = x.shape
    ctx_len = ctx.shape[1]
    depth = ada_w.shape[0]
    geom = _geometry(batch, seq, ctx_len)
    nx = geom["nx"]
    ntok = nx + batch * ctx_len

    stream = jnp.concatenate([x.reshape(nx, d), ctx.reshape(batch * ctx_len, d)], axis=0)
    cond = jnp.concatenate([c, c_ctx[None, :], jnp.zeros((SUBLANES - 1 - batch % SUBLANES, d), f32)], axis=0)
    mods_all = adaln_all(cond, ada_w, ada_b).reshape(depth, cond.shape[0], 6, d)
    rope_ret, rope_att = _rope_tables(geom)
    row = lambda v: v.reshape(1, -1)
    dt_cb0 = (2 * D_MODEL + IN_SPLITS[5]) // LANES
    z_cb0 = D_MODEL // SSM_GW
    conv_col0 = 2 * D_MODEL

    for i in range(depth):
        last = i == depth - 1
        rows = nx if last else ntok
        mods = mods_all[i]
        w_packed, w_gate = _pack_w_in(w_in[i])
        npre = row(mix_norm_pre[i])
        ret_qk, att_qk, val, d1 = in_proj(stream, npre, mods, w_packed, rope_ret, rope_att, geom)

        lg = jax.nn.log_sigmoid(ret_decay[i].astype(f32))
        ret_f = ret_scan(lg, ret_qk, val, geom, bwd=False)
        ret_o = ret_scan(lg, ret_qk, val, geom, bwd=True, d1=d1, prev=ret_f)

        u = ssd_conv(d1, ssm_conv_w[i], ssm_conv_b[i], geom, conv_col0)
        par = jnp.stack([_pad_heads(ssm_dt_bias[i]), _pad_heads(-jnp.exp(ssm_a_log[i].astype(f32)))], axis=2)
        skip = row(jnp.repeat(ssm_d[i, 0].astype(f32) + ssm_d[i, 1].astype(f32), SSM_HEAD_DIM))
        ssd_f = ssd_scan(u, d1, par[0], geom, dt_cb0, bwd=False)
        ssd_o = ssd_scan(u, d1, par[1], geom, dt_cb0, bwd=True, z_cb0=z_cb0, skip=skip,
                         norm_w=row(ssm_norm_w[i]), prev=ssd_f)

        att_o = window_attention(att_sink[i].astype(f32), att_qk, val, geom)

        stream = merge(ret_o, ssd_o, att_o, stream, npre, w_gate, w_branch[i].astype(bf16), w_out[i].astype(bf16),
                       row(mix_norm_post[i]), mods, geom, rows)

        j = i // 2
        fpre, fpost = row(ffn_norm_pre[i]), row(ffn_norm_post[i])
        if i % 2 == 0:
            stream = dense_ffn(stream, fpre, fpost, mods, ffn_w1[j].astype(bf16), ffn_w3[j].astype(bf16),
                               ffn_w2[j].astype(bf16), geom, rows)
        else:
            stream = moe_ffn(stream, fpre, fpost, mods, moe_router[j], moe_w1[j].astype(bf16),
                             moe_w3[j].astype(bf16), moe_w2[j].astype(bf16), geom, rows)
    return stream[:nx].reshape(batch, seq, d)
```

```python
import functools
import math

import jax
import jax.numpy as jnp
import numpy as np
from jax import lax
from jax.experimental import pallas as pl
from jax.experimental.pallas import tpu as pltpu

f32 = jnp.float32
bf16 = jnp.bfloat16

D_MODEL = 1024
GRID_W = 64
CHUNK = 128
NORM_EPS = 1e-6
ROPE_BASE = 10000.0
NEG_INF = -1e30
RET_HEADS, RET_DK, RET_DV = 4, 128, 256
SSM_HEADS, SSM_HEAD_DIM, SSM_GROUPS, SSM_STATE = 16, 64, 2, 128
SSM_HG = SSM_HEADS // SSM_GROUPS
SSM_GW = SSM_HG * SSM_HEAD_DIM
ATT_HEADS, ATT_KV_HEADS, ATT_HEAD_DIM = 8, 2, 128
ATT_G = ATT_HEADS // ATT_KV_HEADS
D_FF = 2816
N_EXPERTS = 8
IN_SPLITS = (512, 512, 1024, 1024, 1024, 1536, 16, 1024, 256, 256, 3072)

LANES = 128
SUBLANES = 8
MOE_BLOCK = 256
MOE_DISPATCH_ROWS = 512
MOE_COMBINE_ROWS = 512
VMEM_LIMIT = 56 * 2 ** 20
DMA_ISSUE_UNROLL = 8
ATT_STACK = ATT_G
SSD_CHUNK_GROUP = 4
RET_CHUNK = 256
RET_CHUNK_GROUP = 2
FFN_SPLIT = 2


def _cp(sem, vmem=None):
    return pltpu.CompilerParams(dimension_semantics=sem, vmem_limit_bytes=vmem)


def _silu(v):
    return v / (1.0 + jnp.exp(-v))


def _sigmoid(v):
    return 1.0 / (1.0 + jnp.exp(-v))


def _rms(v, w):
    return v * lax.rsqrt(jnp.mean(v * v, axis=-1, keepdims=True) + NORM_EPS) * w


def _norm_mod(x, nw, m, shift_row, scale_row):
    return _rms(x, nw) * (1.0 + m[scale_row:scale_row + 1]) + m[shift_row:shift_row + 1]


def _dot(a, b):
    return jnp.dot(a, b, preferred_element_type=f32)


def _dot_nt(a, b):
    return lax.dot_general(a, b, (((1,), (1,)), ((), ())), preferred_element_type=f32)


def _split3(a):
    hi = a.astype(bf16)
    r = a - hi.astype(f32)
    mid = r.astype(bf16)
    lo = (r - mid.astype(f32)).astype(bf16)
    return hi, mid, lo


def _row_tile_mod_idx(geom, tm):
    nxt = geom["nx"] // tm
    per_b = geom["S"] // tm
    return lambda i: (jnp.where(i < nxt, i // per_b, geom["B"]), 0, 0)


def _resident(shape):
    return pl.BlockSpec(shape, lambda i: (0,) * len(shape), pipeline_mode=pl.Buffered(1))


def _adaln_body(c_ref, w_ref, b_ref, o_ref):
    s = _silu(c_ref[...])
    o_ref[0] = _dot(s.astype(bf16), w_ref[0].astype(bf16)) + b_ref[0]


def adaln_all(cond, ada_w, ada_b):
    depth, d, n = ada_w.shape
    rows = cond.shape[0]
    tn = 1024
    return pl.pallas_call(
        _adaln_body,
        grid=(depth, n // tn),
        in_specs=[pl.BlockSpec((rows, d), lambda l, j: (0, 0)),
                  pl.BlockSpec((1, d, tn), lambda l, j: (l, 0, j)),
                  pl.BlockSpec((1, 1, tn), lambda l, j: (l, 0, j))],
        out_specs=pl.BlockSpec((1, rows, tn), lambda l, j: (l, 0, j)),
        out_shape=jax.ShapeDtypeStruct((depth, rows, n), f32),
        compiler_params=_cp(("arbitrary", "arbitrary")),
        name="adaln",
    )(cond, ada_w, ada_b.reshape(depth, 1, n))


PROJ_CHUNK = 512
D1_DT_COL = 2 * D_MODEL
D1_XBC_COL = D1_DT_COL + SSM_GROUPS * LANES
D1_WIDTH = D1_XBC_COL + IN_SPLITS[5]


def _proj_plan():
    rq, rk = RET_HEADS * RET_DK, RET_HEADS * RET_DK
    aq, ak = ATT_HEADS * ATT_HEAD_DIM, ATT_KV_HEADS * ATT_HEAD_DIM
    groups = [(0, rq, "ret", 1.0), (0, rk, "ret", RET_DK ** -0.5),
              (1, aq, "att", ATT_HEAD_DIM ** -0.5), (1, ak, "att", 1.0),
              (2, RET_HEADS * RET_DV + ak, None, 1.0),
              (3, D1_WIDTH, None, 1.0)]
    plan, wcol, ocol = [], 0, {}
    for out, width, kind, scale in groups:
        done = 0
        while done < width:
            step = min(PROJ_CHUNK, width - done)
            plan.append((wcol, step, out, ocol.get(out, 0), kind, scale))
            wcol += step
            ocol[out] = ocol.get(out, 0) + step
            done += step
    return plan, wcol, [ocol[o] for o in range(4)]


def _proj_body(x_ref, nw_ref, mod_ref, w_ref, rcos_ref, rsin_ref, acos_ref, aup_ref, adn_ref,
               o_ret, o_att, o_val, o_d1, *, plan, ret_shift, att_shifts):
    outs = (o_ret, o_att, o_val, o_d1)
    h = _norm_mod(x_ref[...], nw_ref[...], mod_ref[0], 0, 1).astype(bf16)
    for wcol, width, out, ocol, kind, scale in plan:
        acc = _dot(h, w_ref[:, wcol:wcol + width])
        o_ref = outs[out]
        if kind is None:
            o_ref[:, ocol:ocol + width] = acc.astype(o_ref.dtype)
            continue
        for c in range(width // LANES):
            t = acc[:, c * LANES:(c + 1) * LANES]
            if kind == "ret":
                r = t * rcos_ref[...] + pltpu.roll(t, ret_shift, 1) * rsin_ref[...]
            else:
                r = (t * acos_ref[...] + pltpu.roll(t, att_shifts[0], 1) * aup_ref[...]
                     + pltpu.roll(t, att_shifts[1], 1) * adn_ref[...])
            o_ref[:, ocol + c * LANES:ocol + (c + 1) * LANES] = (r * scale).astype(o_ref.dtype)


def in_proj(x, nw, mods, w, rope_ret, rope_att, geom):
    ntok, d = x.shape
    tm = geom["tm_merge"]
    plan, wcols, widths = _proj_plan()
    assert w.shape == (d, wcols)
    nxt = geom["nx"] // tm
    per_b = geom["S"] // tm
    pos_idx = lambda i: (jnp.where(i < nxt, i % per_b, per_b), 0)
    table = pl.BlockSpec((tm, LANES), pos_idx)
    dtypes = (bf16, bf16, bf16, f32)
    return pl.pallas_call(
        functools.partial(_proj_body, plan=plan, ret_shift=rope_ret[0][0], att_shifts=rope_att[0]),
        grid=(ntok // tm,),
        in_specs=[pl.BlockSpec((tm, d), lambda i: (i, 0)),
                  pl.BlockSpec((1, d), lambda i: (0, 0)),
                  pl.BlockSpec((1, 6, d), _row_tile_mod_idx(geom, tm)),
                  _resident((d, wcols)),
                  table, table, table, table, table],
        out_specs=[pl.BlockSpec((tm, n), lambda i: (i, 0)) for n in widths],
        out_shape=[jax.ShapeDtypeStruct((ntok, n), dt) for n, dt in zip(widths, dtypes)],
        compiler_params=_cp(("arbitrary",), VMEM_LIMIT),
        name="in_proj",
    )(x, nw, mods, w, rope_ret[1], rope_ret[2][0], rope_att[1], rope_att[2][0], rope_att[2][1])


def _conv_body(x_ref, prev_ref, next_ref, w_ref, b_ref, o_ref, *, tiles_per_seq, nx_tiles, ctx_len):
    i = pl.program_id(0)
    x = x_ref[...]
    rows = x.shape[0]
    t_in = i % tiles_per_seq
    is_x = i < nx_tiles
    use_prev = jnp.logical_and(is_x, t_in > 0)
    use_next = jnp.logical_and(is_x, t_in < tiles_per_seq - 1)
    prow = jnp.where(use_prev, prev_ref[SUBLANES - 1:SUBLANES, :], 0.0)
    nrow = jnp.where(use_next, next_ref[0:1, :], 0.0)
    rid = lax.broadcasted_iota(jnp.int32, x.shape, 0)
    xm = jnp.where(rid == 0, prow, pltpu.roll(x, 1, 0))
    xp = jnp.where(rid == rows - 1, nrow, pltpu.roll(x, rows - 1, 0))
    seg = lax.rem(rid, ctx_len)
    is_ctx = jnp.logical_not(is_x)
    xm = jnp.where(jnp.logical_and(is_ctx, seg == 0), 0.0, xm)
    xp = jnp.where(jnp.logical_and(is_ctx, seg == ctx_len - 1), 0.0, xp)
    w = w_ref[...]
    o_ref[...] = _silu(xm * w[0:1] + x * w[1:2] + xp * w[2:3] + b_ref[...])


def ssd_conv(d1, conv_w, conv_b, geom, col0):
    ntok = d1.shape[0]
    width = conv_w.shape[1]
    tc = width // 2
    cb0 = col0 // tc
    tr = geom["T"]
    per8 = tr // SUBLANES
    last8 = ntok // SUBLANES - 1
    return pl.pallas_call(
        functools.partial(_conv_body, tiles_per_seq=geom["S"] // tr, nx_tiles=geom["nx"] // tr, ctx_len=geom["L"]),
        grid=(ntok // tr, width // tc),
        in_specs=[pl.BlockSpec((tr, tc), lambda i, j: (i, cb0 + j)),
                  pl.BlockSpec((SUBLANES, tc), lambda i, j: (jnp.maximum(i * per8 - 1, 0), cb0 + j)),
                  pl.BlockSpec((SUBLANES, tc), lambda i, j: (jnp.minimum((i + 1) * per8, last8), cb0 + j)),
                  pl.BlockSpec((3, tc), lambda i, j: (0, j)),
                  pl.BlockSpec((1, tc), lambda i, j: (0, j))],
        out_specs=pl.BlockSpec((tr, tc), lambda i, j: (i, j)),
        out_shape=jax.ShapeDtypeStruct((ntok, width), f32),
        compiler_params=_cp(("arbitrary", "arbitrary")),
        name="ssd_conv",
    )(d1, d1, d1, conv_w, conv_b.reshape(1, width))


def _scan_row_index(geom, bwd):
    tile = geom["T"]
    per_b = geom["S"] // tile
    ctx_tile = geom["nx"] // tile

    def idx(s):
        t = s - 1
        xt = (t // per_b) * per_b + (per_b - 1 - t % per_b) if bwd else t
        return jnp.where(s == 0, ctx_tile, xt)

    return idx, ctx_tile + 1


def _scan_driver(chunk, s_all, *, bwd, geom, group, rows=CHUNK):
    tile, ctx_len, nbatch = geom["T"], geom["L"], geom["B"]
    per_b = geom["S"] // tile
    s = pl.program_id(1)

    @pl.when(s == 0)
    def _():
        for b in range(nbatch):
            state = jnp.zeros(s_all.shape[1:], f32)
            cs = range(ctx_len // rows)
            for ci in (reversed(cs) if bwd else cs):
                state = chunk(b * ctx_len + ci * rows, state)
            s_all[b] = state

    @pl.when(s > 0)
    def _():
        st = s_all.at[(s - 1) // per_b]
        n_groups = tile // (group * rows)
        order = tuple(reversed(range(group))) if bwd else tuple(range(group))

        def body(p, state):
            base = ((n_groups - 1 - p) if bwd else p) * (group * rows)
            if not isinstance(base, int):
                base = pl.multiple_of(base, group * rows)
            for ci in order:
                state = chunk(base + ci * rows, state)
            return state

        st[...] = body(0, st[...]) if n_groups == 1 else lax.fori_loop(0, n_groups, body, st[...])


def _expand_heads(a, lane):
    rows = a.shape[0]
    lane = lane[:rows]
    cols = []
    for v in range(SSM_HG // 2):
        left = jnp.broadcast_to(a[:, 2 * v:2 * v + 1], (rows, LANES))
        right = jnp.broadcast_to(a[:, 2 * v + 1:2 * v + 2], (rows, LANES))
        cols.append(jnp.where(lane < SSM_HEAD_DIM, left, right))
    return jnp.concatenate(cols, axis=1)


def _ssd_body(*refs, bwd, geom):
    if bwd:
        xs_ref, b_ref, c_ref, dt_ref, par_ref, z_ref, skip_ref, nw_ref, prev_ref, o_ref, s_all = refs
    else:
        xs_ref, b_ref, c_ref, dt_ref, par_ref, o_ref, s_all = refs

    par = par_ref[0]
    ii = lax.broadcasted_iota(jnp.int32, (CHUNK, CHUNK), 0)
    jj = lax.broadcasted_iota(jnp.int32, (CHUNK, CHUNK), 1)
    tri_mask = (ii <= jj) if bwd else (ii >= jj)
    tri = jnp.where(tri_mask, 1.0, 0.0).astype(bf16)
    lane = jj
    last = 0 if bwd else CHUNK - 1
    pair_keep = [jnp.where((lane < SSM_HEAD_DIM) == (a == 0), 1.0, 0.0).astype(bf16) for a in range(2)]

    def chunk(r0, state):
        sl = pl.ds(r0, CHUNK)
        pre = dt_ref[sl, :] + par[0:1]
        dt = jnp.maximum(pre, 0.0) + jnp.log(1.0 + jnp.exp(-jnp.abs(pre)))
        hi, mid, lo = _split3(dt * par[1:2])
        cum = _dot(tri, hi) + _dot(tri, mid) + _dot(tri, lo)
        tot = cum[last:last + 1]
        cum_t = cum.T[0:SUBLANES]
        dt_t = dt.T[0:SUBLANES]
        dtws_t = dt_t * jnp.exp(cum_t[:, last:last + 1] - cum_t)
        dec_x = _expand_heads(jnp.broadcast_to(jnp.exp(tot), (SUBLANES, LANES)), lane)[0:1]

        xs = xs_ref[sl, :]
        xs_b = xs.astype(bf16)
        bm = b_ref[sl, :]
        cm = c_ref[sl, :]
        cb = _dot_nt(cm.astype(bf16), bm.astype(bf16))
        bm_t = bm.T
        state_b = state.astype(bf16)
        inter_all = _dot(cm.astype(bf16), state_b)
        y_parts, upd_parts = [], []
        for v in range(SSM_HG // 2):
            ps = slice(v * LANES, (v + 1) * LANES)
            y_p = upd_p = None
            row_scale = []
            for a in range(2):
                h = 2 * v + a
                xs_h = xs_b[:, ps] * pair_keep[a]
                cbc = jnp.broadcast_to(cum[:, h:h + 1], (CHUNK, CHUNK))
                lmat = jnp.exp(jnp.where(tri_mask, cbc - cum_t[h:h + 1, :], NEG_INF))
                row_scale.append(jnp.exp(cbc))
                t = _dot((cb * lmat * dt_t[h:h + 1, :]).astype(bf16), xs_h)
                y_p = t if y_p is None else y_p + t
                t = _dot((bm_t * dtws_t[h:h + 1, :]).astype(bf16), xs_h)
                upd_p = t if upd_p is None else upd_p + t
            y_parts.append(y_p + inter_all[:, ps] * jnp.where(lane < SSM_HEAD_DIM, row_scale[0], row_scale[1]))
            upd_parts.append(upd_p)
        y = jnp.concatenate(y_parts, axis=1)
        if bwd:
            y = y + prev_ref[sl, :] + xs * skip_ref[...]
            gz = y * _silu(z_ref[sl, :])
            o_ref[sl, :] = _rms(gz, nw_ref[...]).astype(o_ref.dtype)
        else:
            o_ref[sl, :] = y
        return dec_x * state + jnp.concatenate(upd_parts, axis=1)

    _scan_driver(chunk, s_all, bwd=bwd, geom=geom, group=SSD_CHUNK_GROUP)


def ssd_scan(u, d1, par, geom, dt_cb0, bwd, z_cb0=None, skip=None, norm_w=None, prev=None):
    ntok = u.shape[0]
    idx, steps = _scan_row_index(geom, bwd)
    gw = SSM_GW
    nb_blk = SSM_HEADS * SSM_HEAD_DIM // SSM_STATE
    tile = lambda w, cb: pl.BlockSpec((geom["T"], w), lambda g, s, cb=cb: (idx(s), cb(g)))
    in_specs = [tile(gw, lambda g: g),
                tile(SSM_STATE, lambda g: nb_blk + g),
                tile(SSM_STATE, lambda g: nb_blk + SSM_GROUPS + g),
                tile(LANES, lambda g: dt_cb0 + g),
                pl.BlockSpec((1, 2, LANES), lambda g, s: (g, 0, 0))]
    args = [u, u, u, d1, par]
    if bwd:
        in_specs += [tile(gw, lambda g: z_cb0 + g),
                     pl.BlockSpec((1, gw), lambda g, s: (0, g)),
                     pl.BlockSpec((1, gw), lambda g, s: (0, g)),
                     tile(gw, lambda g: g)]
        args += [d1, skip, norm_w, prev]
    return pl.pallas_call(
        functools.partial(_ssd_body, bwd=bwd, geom=geom),
        grid=(SSM_GROUPS, steps),
        in_specs=in_specs,
        out_specs=tile(gw, lambda g: g),
        out_shape=jax.ShapeDtypeStruct((ntok, SSM_GROUPS * gw), bf16 if bwd else f32),
        scratch_shapes=[pltpu.VMEM((geom["B"], SSM_STATE, gw), f32)],
        compiler_params=_cp(("arbitrary", "arbitrary"), VMEM_LIMIT),
        name="ssd_bwd" if bwd else "ssd_fwd",
    )(*args)


def _ret_body(*refs, bwd, geom):
    if bwd:
        lg_ref, q_ref, k_ref, v_ref, g_ref, prev_ref, o_ref, s_all = refs
    else:
        lg_ref, q_ref, k_ref, v_ref, o_ref, s_all = refs

    ii = lax.broadcasted_iota(jnp.int32, (RET_CHUNK, RET_CHUNK), 0).astype(f32)
    jj = lax.broadcasted_iota(jnp.int32, (RET_CHUNK, RET_CHUNK), 1).astype(f32)
    diff = (jj - ii) if bwd else (ii - jj)
    icol = ii[:, 0:1]
    jrow = jj[0:1, :]
    consts = []
    for h in range(RET_HEADS):
        lg = lg_ref[1 if bwd else 0, h]
        dmat = jnp.where(diff >= 0, jnp.exp(jnp.maximum(diff, 0.0) * lg), 0.0)
        if bwd:
            wq = jnp.exp((RET_CHUNK - icol) * lg)
            ws = jnp.exp(jrow * lg)
        else:
            wq = jnp.exp((icol + 1.0) * lg)
            ws = jnp.exp((RET_CHUNK - 1.0 - jrow) * lg)
        consts.append((dmat, wq, ws, jnp.exp(jnp.full((1, 1), float(RET_CHUNK), f32) * lg)))

    def chunk(r0, state):
        sl = pl.ds(r0, RET_CHUNK)
        new_state = []
        for h, (dmat, wq, ws, decay) in enumerate(consts):
            ks = slice(h * RET_DK, (h + 1) * RET_DK)
            vs = slice(h * RET_DV, (h + 1) * RET_DV)
            q = q_ref[sl, ks]
            k = k_ref[sl, ks]
            v = v_ref[sl, vs]
            sc = _dot_nt(q, k) * dmat
            o = _dot(sc.astype(bf16), v) + _dot(q, state[h].astype(bf16)) * wq
            kt = (k.astype(f32).T * ws).astype(bf16)
            if bwd:
                o = o + prev_ref[sl, vs]
                o = o * lax.rsqrt(jnp.mean(o * o, axis=-1, keepdims=True) + NORM_EPS)
                o_ref[sl, vs] = (_silu(g_ref[sl, vs]) * o).astype(o_ref.dtype)
            else:
                o_ref[sl, vs] = o
            new_state.append(decay * state[h] + _dot(kt, v))
        return jnp.stack(new_state)

    _scan_driver(chunk, s_all, bwd=bwd, geom=geom, group=RET_CHUNK_GROUP, rows=RET_CHUNK)


def ret_scan(lg, qk, vb, geom, bwd, d1=None, prev=None):
    ntok = qk.shape[0]
    idx, steps = _scan_row_index(geom, bwd)
    kw, vw = RET_HEADS * RET_DK, RET_HEADS * RET_DV
    tile = lambda w, cb: pl.BlockSpec((geom["T"], w), lambda z, s: (idx(s), cb))
    in_specs = [pl.BlockSpec(memory_space=pltpu.SMEM), tile(kw, 0), tile(kw, 1), tile(vw, 0)]
    args = [lg, qk, qk, vb]
    if bwd:
        in_specs += [tile(vw, 0), tile(vw, 0)]
        args += [d1, prev]
    return pl.pallas_call(
        functools.partial(_ret_body, bwd=bwd, geom=geom),
        grid=(1, steps),
        in_specs=in_specs,
        out_specs=tile(vw, 0),
        out_shape=jax.ShapeDtypeStruct((ntok, vw), bf16 if bwd else f32),
        scratch_shapes=[pltpu.VMEM((geom["B"], RET_HEADS, RET_DK, RET_DV), f32)],
        compiler_params=_cp(("arbitrary", "arbitrary"), VMEM_LIMIT),
        name="ret_bwd" if bwd else "ret_fwd",
    )(*args)


def _att_body(sink_ref, q_ref, kc_ref, vc_ref, kp_ref, ko_ref, kn_ref, vp_ref, vo_ref, vn_ref, o_ref, *, nblk, ctx_blk):
    n = pl.program_id(1) - ctx_blk
    own_ok = n >= 0
    prev_ok = n >= 1
    next_ok = jnp.logical_and(n >= 0, n <= nblk - 2)
    rows = ATT_STACK * CHUNK
    qi = lax.rem(lax.broadcasted_iota(jnp.int32, (rows, CHUNK), 0), CHUNK)
    kj = lax.broadcasted_iota(jnp.int32, (rows, CHUNK), 1)
    m_prev = jnp.logical_and(kj >= qi, prev_ok)
    m_next = jnp.logical_and(kj <= qi, next_ok)
    hd = ATT_HEAD_DIM
    for h0 in range(0, ATT_HEADS, ATT_STACK):
        kv = h0 // ATT_G
        ks = slice(kv * hd, (kv + 1) * hd)
        kc, vc = kc_ref[:, ks], vc_ref[:, ks]
        heads = list(range(h0, h0 + ATT_STACK))
        q = jnp.concatenate([q_ref[:, h * hd:(h + 1) * hd] for h in heads], axis=0)
        sk = jnp.concatenate([jnp.full((CHUNK, 1), sink_ref[h], f32) for h in heads], axis=0)
        s_c = _dot_nt(q, kc)
        s_p = jnp.where(m_prev, _dot_nt(q, kp_ref[:, ks]), NEG_INF)
        s_o = jnp.where(own_ok, _dot_nt(q, ko_ref[:, ks]), NEG_INF)
        s_n = jnp.where(m_next, _dot_nt(q, kn_ref[:, ks]), NEG_INF)
        parts = [s_c[:, c * CHUNK:(c + 1) * CHUNK] for c in range(s_c.shape[1] // CHUNK)] + [s_p, s_o, s_n]
        mx = jnp.maximum(jnp.max(functools.reduce(jnp.maximum, parts), axis=-1, keepdims=True), sk)
        probs = [jnp.exp(t - mx) for t in parts]
        den = jnp.exp(sk - mx) + jnp.sum(functools.reduce(jnp.add, probs), axis=-1, keepdims=True)
        n_c = len(parts) - 3
        p_c = jnp.concatenate(probs[:n_c], axis=1)
        o = (_dot(p_c.astype(bf16), vc) + _dot(probs[n_c].astype(bf16), vp_ref[:, ks])
             + _dot(probs[n_c + 1].astype(bf16), vo_ref[:, ks]) + _dot(probs[n_c + 2].astype(bf16), vn_ref[:, ks]))
        o = o / den
        for g, h in enumerate(heads):
            o_ref[:, h * hd:(h + 1) * hd] = o[g * CHUNK:(g + 1) * CHUNK].astype(o_ref.dtype)


def window_attention(sink, qk, vb, geom):
    ntok = qk.shape[0]
    blk = CHUNK
    nblk = geom["S"] // blk
    ctx_blk = geom["L"] // blk
    ctx0 = geom["nx"] // blk
    kvw = ATT_KV_HEADS * ATT_HEAD_DIM
    qw = ATT_HEADS * ATT_HEAD_DIM
    kcol = qw // kvw

    def q_idx(b, i):
        return jnp.where(i < ctx_blk, ctx0 + b * ctx_blk + i, b * nblk + i - ctx_blk)

    def win(off):
        def f(b, i):
            n = jnp.clip(i - ctx_blk + off, 0, nblk - 1)
            return (b * nblk + n, kcol)
        return f

    ctx_spec = pl.BlockSpec((geom["L"], kvw), lambda b, i: (geom["nx"] // geom["L"] + b, kcol))
    wspec = lambda off: pl.BlockSpec((blk, kvw), win(off))
    return pl.pallas_call(
        functools.partial(_att_body, nblk=nblk, ctx_blk=ctx_blk),
        grid=(geom["B"], ctx_blk + nblk),
        in_specs=[pl.BlockSpec(memory_space=pltpu.SMEM),
                  pl.BlockSpec((blk, qw), lambda b, i: (q_idx(b, i), 0)),
                  ctx_spec, ctx_spec,
                  wspec(-1), wspec(0), wspec(1), wspec(-1), wspec(0), wspec(1)],
        out_specs=pl.BlockSpec((blk, qw), lambda b, i: (q_idx(b, i), 0)),
        out_shape=jax.ShapeDtypeStruct((ntok, qw), bf16),
        compiler_params=_cp(("arbitrary", "arbitrary")),
        name="window_attention",
    )(sink, qk, qk, vb, qk, qk, qk, vb, vb, vb)


def _merge_body(r_ref, s_ref, a_ref, x_ref, npre_ref, wg_ref, wb_ref, wo_ref, nw_ref, mod_ref, o_ref):
    x = x_ref[...]
    m = mod_ref[0]
    h = _norm_mod(x, npre_ref[...], m, 0, 1).astype(bf16)
    acc = None
    for n, ref in enumerate((r_ref, s_ref, a_ref)):
        gate = _dot(h, wg_ref[:, n * D_MODEL:(n + 1) * D_MODEL])
        t = _sigmoid(gate) * _dot(ref[...], wb_ref[n])
        acc = t if acc is None else acc + t
    y = _dot(acc.astype(bf16), wo_ref[...])
    o_ref[...] = x + m[2:3] * _rms(y, nw_ref[...])


def merge(ret_o, ssd_o, att_o, x, npre, wg, wb, wo, nw, mods, geom, rows):
    d = D_MODEL
    tm = geom["tm_merge"]
    row = lambda w: pl.BlockSpec((tm, w), lambda i: (i, 0))
    return pl.pallas_call(
        _merge_body,
        grid=(rows // tm,),
        in_specs=[row(d), row(d), row(d), row(d),
                  pl.BlockSpec((1, d), lambda i: (0, 0)),
                  _resident((d, 3 * d)), _resident((3, d, d)), _resident((d, d)),
                  pl.BlockSpec((1, d), lambda i: (0, 0)),
                  pl.BlockSpec((1, 6, d), _row_tile_mod_idx(geom, tm))],
        out_specs=row(d),
        out_shape=jax.ShapeDtypeStruct((rows, d), f32),
        compiler_params=_cp(("arbitrary",), VMEM_LIMIT),
        name="merge",
    )(ret_o, ssd_o, att_o, x, npre, wg, wb, wo, nw, mods)


def _swiglu(h, w1, w3, w2, f_split):
    step = w1.shape[1] // f_split
    acc = None
    for f in range(f_split):
        fs = slice(f * step, (f + 1) * step)
        act = _silu(_dot(h, w1[:, fs])) * _dot(h, w3[:, fs])
        t = _dot(act.astype(bf16), w2[fs, :])
        acc = t if acc is None else acc + t
    return acc


def _ffn_body(x_ref, npre_ref, npost_ref, mod_ref, w1_ref, w3_ref, w2_ref, o_ref):
    x = x_ref[...]
    m = mod_ref[0]
    h = _norm_mod(x, npre_ref[...], m, 3, 4).astype(bf16)
    y = _swiglu(h, w1_ref, w3_ref, w2_ref, FFN_SPLIT)
    o_ref[...] = x + m[5:6] * _rms(y, npost_ref[...])


def dense_ffn(x, npre, npost, mods, w1, w3, w2, geom, rows):
    d = D_MODEL
    ff = w1.shape[1]
    tm = geom["tm_merge"]
    return pl.pallas_call(
        _ffn_body,
        grid=(rows // tm,),
        in_specs=[pl.BlockSpec((tm, d), lambda i: (i, 0)),
                  pl.BlockSpec((1, d), lambda i: (0, 0)),
                  pl.BlockSpec((1, d), lambda i: (0, 0)),
                  pl.BlockSpec((1, 6, d), _row_tile_mod_idx(geom, tm)),
                  _resident((d, ff)), _resident((d, ff)), _resident((ff, d))],
        out_specs=pl.BlockSpec((tm, d), lambda i: (i, 0)),
        out_shape=jax.ShapeDtypeStruct((rows, d), f32),
        compiler_params=_cp(("arbitrary",), VMEM_LIMIT),
        name="dense_ffn",
    )(x, npre, npost, mods, w1, w3, w2)


def _router_body(x_ref, npre_ref, mod_ref, wr_ref, h_ref, route_ref):
    h = _norm_mod(x_ref[...], npre_ref[...], mod_ref[0], 3, 4)
    h_ref[...] = h
    h_hi = h.astype(bf16)
    h_lo = (h - h_hi.astype(f32)).astype(bf16)
    wr = wr_ref[...]
    w_hi = wr.astype(bf16)
    w_lo = (wr - w_hi.astype(f32)).astype(bf16)
    logits = _dot(h_hi, w_hi) + _dot(h_hi, w_lo) + _dot(h_lo, w_hi) + _dot(h_lo, w_lo)
    lane = lax.broadcasted_iota(jnp.int32, logits.shape, 1).astype(f32)
    l1 = jnp.where(lane < N_EXPERTS, logits, -jnp.inf)
    m1 = jnp.max(l1, axis=-1, keepdims=True)
    i1 = jnp.min(jnp.where(l1 == m1, lane, float(LANES)), axis=-1, keepdims=True)
    l2 = jnp.where(lane == i1, -jnp.inf, l1)
    m2 = jnp.max(l2, axis=-1, keepdims=True)
    i2 = jnp.min(jnp.where(l2 == m2, lane, float(LANES)), axis=-1, keepdims=True)
    e = jnp.exp(m2 - m1)
    w1 = 1.0 / (1.0 + e)
    w2 = e / (1.0 + e)
    route_ref[...] = jnp.where(lane == 0.0, i1, jnp.where(lane == 1.0, i2,
                                                          jnp.where(lane == 2.0, w1, jnp.where(lane == 3.0, w2, 0.0))))


def moe_router(x, npre, mods, w_router, geom, rows):
    d = D_MODEL
    tm = geom["tm_merge"]
    wr = jnp.pad(w_router, ((0, 0), (0, LANES - w_router.shape[1])))
    return pl.pallas_call(
        _router_body,
        grid=(rows // tm,),
        in_specs=[pl.BlockSpec((tm, d), lambda i: (i, 0)),
                  pl.BlockSpec((1, d), lambda i: (0, 0)),
                  pl.BlockSpec((1, 6, d), _row_tile_mod_idx(geom, tm)),
                  pl.BlockSpec((d, LANES), lambda i: (0, 0))],
        out_specs=[pl.BlockSpec((tm, d), lambda i: (i, 0)), pl.BlockSpec((tm, LANES), lambda i: (i, 0))],
        out_shape=[jax.ShapeDtypeStruct((rows, d), f32), jax.ShapeDtypeStruct((rows, LANES), f32)],
        compiler_params=_cp(("arbitrary",)),
        name="moe_router",
    )(x, npre, mods, wr)


def _row_copy(src, src_row, dst, dst_row, sem):
    return pltpu.make_async_copy(src.at[pl.ds(src_row, 1)], dst.at[pl.ds(dst_row, 1)], sem)


def _dispatch_body(slot_ref, h_ref, xb_in, xb_hbm, sem, *, rows):
    del xb_in

    def start(r, c):
        _row_copy(h_ref, r, xb_hbm, slot_ref[0, 0, 2 * r], sem).start()
        _row_copy(h_ref, r, xb_hbm, slot_ref[0, 0, 2 * r + 1], sem).start()
        return c

    lax.fori_loop(0, rows, start, 0, unroll=DMA_ISSUE_UNROLL)
    for _ in range(2):
        pltpu.make_async_copy(h_ref, xb_hbm.at[pl.ds(0, rows)], sem).wait()


def moe_dispatch(h, slots, n_slots):
    n_tok, d = h.shape
    rows = MOE_DISPATCH_ROWS
    steps = n_tok // rows
    return pl.pallas_call(
        functools.partial(_dispatch_body, rows=rows),
        grid=(steps,),
        in_specs=[pl.BlockSpec((1, 1, 2 * rows), lambda i: (i, 0, 0), memory_space=pltpu.SMEM),
                  pl.BlockSpec((rows, d), lambda i: (i, 0)),
                  pl.BlockSpec(memory_space=pl.ANY)],
        out_specs=pl.BlockSpec(memory_space=pl.ANY),
        out_shape=jax.ShapeDtypeStruct((n_slots, d), f32),
        scratch_shapes=[pltpu.SemaphoreType.DMA(())],
        input_output_aliases={2: 0},
        compiler_params=_cp(("arbitrary",)),
        name="moe_dispatch",
    )(slots.reshape(steps, 1, 2 * rows), h, jnp.zeros((n_slots, d), f32))


def _expert_body(be_ref, cnt_ref, used_ref, x_ref, w1_ref, w3_ref, w2_ref, o_ref):
    del be_ref, used_ref
    i = pl.program_id(0)

    @pl.when(cnt_ref[i] > 0)
    def _():
        o_ref[...] = _swiglu(x_ref[...].astype(bf16), w1_ref.at[0], w3_ref.at[0], w2_ref.at[0], FFN_SPLIT)

    @pl.when(cnt_ref[i] == 0)
    def _():
        o_ref[...] = jnp.zeros_like(o_ref)


def moe_experts(xb, blk_e, blk_cnt, n_used, w1, w3, w2):
    n_slots, d = xb.shape
    ff = w1.shape[2]
    tm = MOE_BLOCK
    grid_spec = pltpu.PrefetchScalarGridSpec(
        num_scalar_prefetch=3,
        grid=(n_slots // tm,),
        in_specs=[pl.BlockSpec((tm, d), lambda i, be, cnt, used: (jnp.minimum(i, used[0] - 1), 0)),
                  pl.BlockSpec((1, d, ff), lambda i, be, cnt, used: (be[i], 0, 0)),
                  pl.BlockSpec((1, d, ff), lambda i, be, cnt, used: (be[i], 0, 0)),
                  pl.BlockSpec((1, ff, d), lambda i, be, cnt, used: (be[i], 0, 0))],
        out_specs=pl.BlockSpec((tm, d), lambda i, be, cnt, used: (i, 0)),
    )
    return pl.pallas_call(
        _expert_body,
        grid_spec=grid_spec,
        out_shape=jax.ShapeDtypeStruct((n_slots, d), f32),
        compiler_params=_cp(("arbitrary",), VMEM_LIMIT),
        name="moe_experts",
    )(blk_e, blk_cnt, n_used, xb, w1, w3, w2)


def _combine_body(slot_ref, route_ref, x_ref, npost_ref, mod_ref, yb_hbm, o_ref, y0_scr, y1_scr, sem, *, rows):
    def start(r, c):
        _row_copy(yb_hbm, slot_ref[0, 0, 2 * r], y0_scr, r, sem).start()
        _row_copy(yb_hbm, slot_ref[0, 0, 2 * r + 1], y1_scr, r, sem).start()
        return c

    lax.fori_loop(0, rows, start, 0, unroll=DMA_ISSUE_UNROLL)
    for buf in (y0_scr, y1_scr):
        pltpu.make_async_copy(yb_hbm.at[pl.ds(0, rows)], buf, sem).wait()
    y = y0_scr[...] * route_ref[:, 2:3] + y1_scr[...] * route_ref[:, 3:4]
    o_ref[...] = x_ref[...] + mod_ref[0][5:6] * _rms(y, npost_ref[...])


def moe_combine(slots, route, x, npost, mods, yb, geom, rows_total):
    d = D_MODEL
    rows = MOE_COMBINE_ROWS
    steps = rows_total // rows
    return pl.pallas_call(
        functools.partial(_combine_body, rows=rows),
        grid=(steps,),
        in_specs=[pl.BlockSpec((1, 1, 2 * rows), lambda i: (i, 0, 0), memory_space=pltpu.SMEM),
                  pl.BlockSpec((rows, LANES), lambda i: (i, 0)),
                  pl.BlockSpec((rows, d), lambda i: (i, 0)),
                  pl.BlockSpec((1, d), lambda i: (0, 0)),
                  pl.BlockSpec((1, 6, d), _row_tile_mod_idx(geom, rows)),
                  pl.BlockSpec(memory_space=pl.ANY)],
        out_specs=pl.BlockSpec((rows, d), lambda i: (i, 0)),
        out_shape=jax.ShapeDtypeStruct((rows_total, d), f32),
        scratch_shapes=[pltpu.VMEM((rows, d), f32), pltpu.VMEM((rows, d), f32), pltpu.SemaphoreType.DMA(())],
        compiler_params=_cp(("arbitrary",)),
        name="moe_combine",
    )(slots.reshape(steps, 1, 2 * rows), route, x, npost, mods, yb)


def _moe_slots(route, n_tok):
    tm = MOE_BLOCK
    flat_e = route[:, 0:2].astype(jnp.int32).reshape(2 * n_tok)
    onehot = (flat_e[:, None] == jnp.arange(N_EXPERTS, dtype=jnp.int32)[None, :]).astype(jnp.int32)
    csum = jnp.cumsum(onehot, axis=0)
    counts = csum[-1]
    rank = jnp.sum(csum * onehot, axis=1) - 1
    padded = (counts + tm - 1) // tm * tm
    pend = jnp.cumsum(padded)
    pstart = pend - padded
    slots = jnp.sum(onehot * pstart[None, :], axis=1) + rank
    n_blocks = 2 * n_tok // tm + N_EXPERTS
    blk = jnp.arange(n_blocks, dtype=jnp.int32)
    blk_e = jnp.minimum(jnp.sum((blk[:, None] >= (pend // tm)[None, :]).astype(jnp.int32), axis=1), N_EXPERTS - 1)
    n_used = (pend[-1] // tm).astype(jnp.int32)
    cnt = jnp.clip(counts[blk_e] - (blk - pstart[blk_e] // tm) * tm, 0, tm)
    blk_cnt = jnp.where(blk < n_used, cnt, 0).astype(jnp.int32)
    return slots.astype(jnp.int32), blk_e, blk_cnt, n_used.reshape(1), n_blocks * tm


def moe_ffn(x, npre, npost, mods, w_router, w1, w3, w2, geom, rows):
    h, route = moe_router(x, npre, mods, w_router, geom, rows)
    slots, blk_e, blk_cnt, n_used, n_slots = _moe_slots(route, rows)
    xb = moe_dispatch(h, slots, n_slots)
    yb = moe_experts(xb, blk_e, blk_cnt, n_used, w1, w3, w2)
    return moe_combine(slots, route, x, npost, mods, yb, geom, rows)


def _rope_tables(geom):
    s, tm = geom["S"], geom["tm"]
    pos = jnp.arange(s)

    def angles(p, dim):
        inv = ROPE_BASE ** (-jnp.arange(0, dim, 2, dtype=f32) / dim)
        return p.astype(f32)[:, None] * inv[None, :]

    def with_identity(t, one):
        return jnp.concatenate([t, jnp.full((tm, LANES), one, f32)], axis=0)

    a = angles(pos, RET_DK)
    ret_cos = jnp.concatenate([jnp.cos(a), jnp.cos(a)], axis=1)
    ret_sin = jnp.concatenate([-jnp.sin(a), jnp.sin(a)], axis=1)
    half = ATT_HEAD_DIM // 2
    ar = angles(pos // GRID_W, half)
    ac = angles(pos % GRID_W, half)
    z = jnp.zeros_like(ar)
    att_cos = jnp.concatenate([jnp.cos(ar), jnp.cos(ar), jnp.cos(ac), jnp.cos(ac)], axis=1)
    att_s_up = jnp.concatenate([-jnp.sin(ar), z, -jnp.sin(ac), z], axis=1)
    att_s_dn = jnp.concatenate([z, jnp.sin(ar), z, jnp.sin(ac)], axis=1)
    ret = ((RET_DK // 2,), with_identity(ret_cos, 1.0), (with_identity(ret_sin, 0.0),))
    att = ((LANES - half // 2, half // 2), with_identity(att_cos, 1.0),
           (with_identity(att_s_up, 0.0), with_identity(att_s_dn, 0.0)))
    return ret, att


def _pack_w_in(w):
    rq, rk, rv, rg, sz, sxbc, sdt, aq, ak, av, bg = jnp.split(w, np.cumsum(IN_SPLITS)[:-1].tolist(), axis=1)
    pad = lambda t, n: jnp.pad(t, ((0, 0), (0, n - t.shape[1])))
    packed = jnp.concatenate([rq, rk, aq, ak, rv, av, rg, sz,
                              pad(sdt[:, :SSM_HG], LANES), pad(sdt[:, SSM_HG:], LANES), sxbc], axis=1)
    return packed.astype(bf16), bg.astype(bf16)


def _pad_heads(v):
    return jnp.pad(v.astype(f32).reshape(2, SSM_GROUPS, SSM_HG), ((0, 0), (0, 0), (0, LANES - SSM_HG)))


def _geometry(batch, seq, ctx_len):
    tm = 1024
    while seq % tm or (batch * ctx_len) % tm:
        tm //= 2
    tm_merge = min(tm, 512)
    scan_tile = batch * ctx_len
    assert seq % scan_tile == 0 and scan_tile % (SSD_CHUNK_GROUP * CHUNK) == 0 and scan_tile % (RET_CHUNK_GROUP * RET_CHUNK) == 0 and ctx_len % RET_CHUNK == 0 and seq % GRID_W == 0
    return {"B": batch, "S": seq, "L": ctx_len, "nx": batch * seq, "tm": tm, "tm_merge": tm_merge, "T": scan_tile}


def kernel(x, c, ctx, c_ctx, ada_w, ada_b, mix_norm_pre, mix_norm_post, ffn_norm_pre, ffn_norm_post, w_in, ret_decay, ssm_conv_w, ssm_conv_b, ssm_dt_bias, ssm_a_log, ssm_d, ssm_norm_w, att_sink, w_branch, w_out, ffn_w1, ffn_w3, ffn_w2, moe_router, moe_w1, moe_w3, moe_w2):
    batch, seq, d = x.shape
    ctx_len = ctx.shape[1]
    depth = ada_w.shape[0]
    geom = _geometry(batch, seq, ctx_len)
    nx = geom["nx"]
    ntok = nx + batch * ctx_len

    stream = jnp.concatenate([x.reshape(nx, d), ctx.reshape(batch * ctx_len, d)], axis=0)
    cond = jnp.concatenate([c, c_ctx[None, :], jnp.zeros((SUBLANES - 1 - batch % SUBLANES, d), f32)], axis=0)
    mods_all = adaln_all(cond, ada_w, ada_b).reshape(depth, cond.shape[0], 6, d)
    rope_ret, rope_att = _rope_tables(geom)
    row = lambda v: v.reshape(1, -1)
    dt_cb0 = D1_DT_COL // LANES
    z_cb0 = D_MODEL // SSM_GW
    conv_col0 = D1_XBC_COL

    for i in range(depth):
        last = i == depth - 1
        rows = nx if last else ntok
        mods = mods_all[i]
        w_packed, w_gate = _pack_w_in(w_in[i])
        npre = row(mix_norm_pre[i])
        ret_qk, att_qk, val, d1 = in_proj(stream, npre, mods, w_packed, rope_ret, rope_att, geom)

        lg = jax.nn.log_sigmoid(ret_decay[i].astype(f32))
        ret_f = ret_scan(lg, ret_qk, val, geom, bwd=False)
        ret_o = ret_scan(lg, ret_qk, val, geom, bwd=True, d1=d1, prev=ret_f)

        u = ssd_conv(d1, ssm_conv_w[i], ssm_conv_b[i], geom, conv_col0)
        par = jnp.stack([_pad_heads(ssm_dt_bias[i]), _pad_heads(-jnp.exp(ssm_a_log[i].astype(f32)))], axis=2)
        skip = row(jnp.repeat(ssm_d[i, 0].astype(f32) + ssm_d[i, 1].astype(f32), SSM_HEAD_DIM))
        ssd_f = ssd_scan(u, d1, par[0], geom, dt_cb0, bwd=False)
        ssd_o = ssd_scan(u, d1, par[1], geom, dt_cb0, bwd=True, z_cb0=z_cb0, skip=skip,
                         norm_w=row(ssm_norm_w[i]), prev=ssd_f)

        att_o = window_attention(att_sink[i].astype(f32), att_qk, val, geom)

        stream = merge(ret_o, ssd_o, att_o, stream, npre, w_gate, w_branch[i].astype(bf16), w_out[i].astype(bf16),
                       row(mix_norm_post[i]), mods, geom, rows)

        j = i // 2
        fpre, fpost = row(ffn_norm_pre[i]), row(ffn_norm_post[i])
        if i % 2 == 0:
            stream = dense_ffn(stream, fpre, fpost, mods, ffn_w1[j].astype(bf16), ffn_w3[j].astype(bf16),
                               ffn_w2[j].astype(bf16), geom, rows)
        else:
            stream = moe_ffn(stream, fpre, fpost, mods, moe_router[j], moe_w1[j].astype(bf16),
                             moe_w3[j].astype(bf16), moe_w2[j].astype(bf16), geom, rows)
    return stream[:nx].reshape(batch, seq, d)
```

```python
import functools
import math

import jax
import jax.numpy as jnp
import numpy as np
from jax import lax
from jax.experimental import pallas as pl
from jax.experimental.pallas import tpu as pltpu

f32 = jnp.float32
bf16 = jnp.bfloat16

D_MODEL = 1024
GRID_W = 64
CHUNK = 128
NORM_EPS = 1e-6
ROPE_BASE = 10000.0
NEG_INF = -1e30
RET_HEADS, RET_DK, RET_DV = 4, 128, 256
SSM_HEADS, SSM_HEAD_DIM, SSM_GROUPS, SSM_STATE = 16, 64, 2, 128
SSM_HG = SSM_HEADS // SSM_GROUPS
SSM_GW = SSM_HG * SSM_HEAD_DIM
ATT_HEADS, ATT_KV_HEADS, ATT_HEAD_DIM = 8, 2, 128
ATT_G = ATT_HEADS // ATT_KV_HEADS
D_FF = 2816
N_EXPERTS = 8
IN_SPLITS = (512, 512, 1024, 1024, 1024, 1536, 16, 1024, 256, 256, 3072)

LANES = 128
SUBLANES = 8
MOE_BLOCK = 256
MOE_DISPATCH_ROWS = 1024
MOE_COMBINE_ROWS = 1024
VMEM_LIMIT = 56 * 2 ** 20
DMA_ISSUE_UNROLL = 8
ATT_STACK = ATT_G
SSD_CHUNK_GROUP = 8
RET_CHUNK = 256
RET_CHUNK_GROUP = 4
FFN_SPLIT = 2


def _cp(sem, vmem=None):
    return pltpu.CompilerParams(dimension_semantics=sem, vmem_limit_bytes=vmem)


def _silu(v):
    return v / (1.0 + jnp.exp(-v))


def _sigmoid(v):
    return 1.0 / (1.0 + jnp.exp(-v))


def _rms(v, w):
    return v * lax.rsqrt(jnp.mean(v * v, axis=-1, keepdims=True) + NORM_EPS) * w


def _norm_mod(x, nw, m, shift_row, scale_row):
    return _rms(x, nw) * (1.0 + m[scale_row:scale_row + 1]) + m[shift_row:shift_row + 1]


def _dot(a, b):
    return jnp.dot(a, b, preferred_element_type=f32)


def _dot_nt(a, b):
    return lax.dot_general(a, b, (((1,), (1,)), ((), ())), preferred_element_type=f32)


def _split3(a):
    hi = a.astype(bf16)
    r = a - hi.astype(f32)
    mid = r.astype(bf16)
    lo = (r - mid.astype(f32)).astype(bf16)
    return hi, mid, lo


def _row_tile_mod_idx(geom, tm):
    nxt = geom["nx"] // tm
    per_b = geom["S"] // tm
    return lambda i: (jnp.where(i < nxt, i // per_b, geom["B"]), 0, 0)


def _resident(shape):
    return pl.BlockSpec(shape, lambda i: (0,) * len(shape), pipeline_mode=pl.Buffered(1))


def _adaln_body(c_ref, w_ref, b_ref, o_ref):
    s = _silu(c_ref[...])
    o_ref[0] = _dot(s.astype(bf16), w_ref[0].astype(bf16)) + b_ref[0]


def adaln_all(cond, ada_w, ada_b):
    depth, d, n = ada_w.shape
    rows = cond.shape[0]
    tn = 1024
    return pl.pallas_call(
        _adaln_body,
        grid=(depth, n // tn),
        in_specs=[pl.BlockSpec((rows, d), lambda l, j: (0, 0)),
                  pl.BlockSpec((1, d, tn), lambda l, j: (l, 0, j)),
                  pl.BlockSpec((1, 1, tn), lambda l, j: (l, 0, j))],
        out_specs=pl.BlockSpec((1, rows, tn), lambda l, j: (l, 0, j)),
        out_shape=jax.ShapeDtypeStruct((depth, rows, n), f32),
        compiler_params=_cp(("arbitrary", "arbitrary")),
        name="adaln",
    )(cond, ada_w, ada_b.reshape(depth, 1, n))


PROJ_CHUNK = 512
D1_DT_COL = 2 * D_MODEL
D1_XBC_COL = D1_DT_COL + SSM_GROUPS * LANES
D1_WIDTH = D1_XBC_COL + IN_SPLITS[5]


def _proj_plan():
    rq, rk = RET_HEADS * RET_DK, RET_HEADS * RET_DK
    aq, ak = ATT_HEADS * ATT_HEAD_DIM, ATT_KV_HEADS * ATT_HEAD_DIM
    groups = [(0, rq, "ret", 1.0), (0, rk, "ret", RET_DK ** -0.5),
              (1, aq, "att", ATT_HEAD_DIM ** -0.5), (1, ak, "att", 1.0),
              (2, RET_HEADS * RET_DV + ak, None, 1.0),
              (3, D1_WIDTH, None, 1.0)]
    plan, wcol, ocol = [], 0, {}
    for out, width, kind, scale in groups:
        done = 0
        while done < width:
            step = min(PROJ_CHUNK, width - done)
            plan.append((wcol, step, out, ocol.get(out, 0), kind, scale))
            wcol += step
            ocol[out] = ocol.get(out, 0) + step
            done += step
    return plan, wcol, [ocol[o] for o in range(4)]


def _proj_body(x_ref, nw_ref, mod_ref, w_ref, rcos_ref, rsin_ref, acos_ref, aup_ref, adn_ref,
               o_ret, o_att, o_val, o_d1, *, plan, ret_shift, att_shifts):
    outs = (o_ret, o_att, o_val, o_d1)
    h = _norm_mod(x_ref[...], nw_ref[...], mod_ref[0], 0, 1).astype(bf16)
    for wcol, width, out, ocol, kind, scale in plan:
        acc = _dot(h, w_ref[:, wcol:wcol + width])
        o_ref = outs[out]
        if kind is None:
            o_ref[:, ocol:ocol + width] = acc.astype(o_ref.dtype)
            continue
        for c in range(width // LANES):
            t = acc[:, c * LANES:(c + 1) * LANES]
            if kind == "ret":
                r = t * rcos_ref[...] + pltpu.roll(t, ret_shift, 1) * rsin_ref[...]
            else:
                r = (t * acos_ref[...] + pltpu.roll(t, att_shifts[0], 1) * aup_ref[...]
                     + pltpu.roll(t, att_shifts[1], 1) * adn_ref[...])
            o_ref[:, ocol + c * LANES:ocol + (c + 1) * LANES] = (r * scale).astype(o_ref.dtype)


def in_proj(x, nw, mods, w, rope_ret, rope_att, geom):
    ntok, d = x.shape
    tm = geom["tm_merge"]
    plan, wcols, widths = _proj_plan()
    assert w.shape == (d, wcols)
    nxt = geom["nx"] // tm
    per_b = geom["S"] // tm
    pos_idx = lambda i: (jnp.where(i < nxt, i % per_b, per_b), 0)
    table = pl.BlockSpec((tm, LANES), pos_idx)
    dtypes = (bf16, bf16, bf16, f32)
    return pl.pallas_call(
        functools.partial(_proj_body, plan=plan, ret_shift=rope_ret[0][0], att_shifts=rope_att[0]),
        grid=(ntok // tm,),
        in_specs=[pl.BlockSpec((tm, d), lambda i: (i, 0)),
                  pl.BlockSpec((1, d), lambda i: (0, 0)),
                  pl.BlockSpec((1, 6, d), _row_tile_mod_idx(geom, tm)),
                  _resident((d, wcols)),
                  table, table, table, table, table],
        out_specs=[pl.BlockSpec((tm, n), lambda i: (i, 0)) for n in widths],
        out_shape=[jax.ShapeDtypeStruct((ntok, n), dt) for n, dt in zip(widths, dtypes)],
        compiler_params=_cp(("arbitrary",), VMEM_LIMIT),
        name="in_proj",
    )(x, nw, mods, w, rope_ret[1], rope_ret[2][0], rope_att[1], rope_att[2][0], rope_att[2][1])


def _conv_body(x_ref, prev_ref, next_ref, w_ref, b_ref, o_ref, *, tiles_per_seq, nx_tiles, ctx_len):
    i = pl.program_id(0)
    x = x_ref[...]
    rows = x.shape[0]
    t_in = i % tiles_per_seq
    is_x = i < nx_tiles
    use_prev = jnp.logical_and(is_x, t_in > 0)
    use_next = jnp.logical_and(is_x, t_in < tiles_per_seq - 1)
    prow = jnp.where(use_prev, prev_ref[SUBLANES - 1:SUBLANES, :], 0.0)
    nrow = jnp.where(use_next, next_ref[0:1, :], 0.0)
    rid = lax.broadcasted_iota(jnp.int32, x.shape, 0)
    xm = jnp.where(rid == 0, prow, pltpu.roll(x, 1, 0))
    xp = jnp.where(rid == rows - 1, nrow, pltpu.roll(x, rows - 1, 0))
    seg = lax.rem(rid, ctx_len)
    is_ctx = jnp.logical_not(is_x)
    xm = jnp.where(jnp.logical_and(is_ctx, seg == 0), 0.0, xm)
    xp = jnp.where(jnp.logical_and(is_ctx, seg == ctx_len - 1), 0.0, xp)
    w = w_ref[...]
    o_ref[...] = _silu(xm * w[0:1] + x * w[1:2] + xp * w[2:3] + b_ref[...])


def ssd_conv(d1, conv_w, conv_b, geom, col0):
    ntok = d1.shape[0]
    width = conv_w.shape[1]
    tc = width // 2
    cb0 = col0 // tc
    tr = geom["T"]
    per8 = tr // SUBLANES
    last8 = ntok // SUBLANES - 1
    return pl.pallas_call(
        functools.partial(_conv_body, tiles_per_seq=geom["S"] // tr, nx_tiles=geom["nx"] // tr, ctx_len=geom["L"]),
        grid=(ntok // tr, width // tc),
        in_specs=[pl.BlockSpec((tr, tc), lambda i, j: (i, cb0 + j)),
                  pl.BlockSpec((SUBLANES, tc), lambda i, j: (jnp.maximum(i * per8 - 1, 0), cb0 + j)),
                  pl.BlockSpec((SUBLANES, tc), lambda i, j: (jnp.minimum((i + 1) * per8, last8), cb0 + j)),
                  pl.BlockSpec((3, tc), lambda i, j: (0, j)),
                  pl.BlockSpec((1, tc), lambda i, j: (0, j))],
        out_specs=pl.BlockSpec((tr, tc), lambda i, j: (i, j)),
        out_shape=jax.ShapeDtypeStruct((ntok, width), f32),
        compiler_params=_cp(("arbitrary", "arbitrary")),
        name="ssd_conv",
    )(d1, d1, d1, conv_w, conv_b.reshape(1, width))


def _scan_row_index(geom, bwd):
    tile = geom["T"]
    per_b = geom["S"] // tile
    ctx_tile = geom["nx"] // tile

    def idx(s):
        t = s - 1
        xt = (t // per_b) * per_b + (per_b - 1 - t % per_b) if bwd else t
        return jnp.where(s == 0, ctx_tile, xt)

    return idx, ctx_tile + 1


def _scan_driver(chunk, s_all, *, bwd, geom, group, rows=CHUNK):
    tile, ctx_len, nbatch = geom["T"], geom["L"], geom["B"]
    per_b = geom["S"] // tile
    s = pl.program_id(1)

    @pl.when(s == 0)
    def _():
        for b in range(nbatch):
            state = jnp.zeros(s_all.shape[1:], f32)
            cs = range(ctx_len // rows)
            for ci in (reversed(cs) if bwd else cs):
                state = chunk(b * ctx_len + ci * rows, state)
            s_all[b] = state

    @pl.when(s > 0)
    def _():
        st = s_all.at[(s - 1) // per_b]
        n_groups = tile // (group * rows)
        order = tuple(reversed(range(group))) if bwd else tuple(range(group))

        def body(p, state):
            base = ((n_groups - 1 - p) if bwd else p) * (group * rows)
            if not isinstance(base, int):
                base = pl.multiple_of(base, group * rows)
            for ci in order:
                state = chunk(base + ci * rows, state)
            return state

        st[...] = body(0, st[...]) if n_groups == 1 else lax.fori_loop(0, n_groups, body, st[...])


def _expand_heads(a, lane):
    rows = a.shape[0]
    lane = lane[:rows]
    cols = []
    for v in range(SSM_HG // 2):
        left = jnp.broadcast_to(a[:, 2 * v:2 * v + 1], (rows, LANES))
        right = jnp.broadcast_to(a[:, 2 * v + 1:2 * v + 2], (rows, LANES))
        cols.append(jnp.where(lane < SSM_HEAD_DIM, left, right))
    return jnp.concatenate(cols, axis=1)


def _ssd_body(*refs, bwd, geom):
    if bwd:
        xs_ref, b_ref, c_ref, dt_ref, par_ref, z_ref, skip_ref, nw_ref, prev_ref, o_ref, s_all = refs
    else:
        xs_ref, b_ref, c_ref, dt_ref, par_ref, o_ref, s_all = refs

    par = par_ref[0]
    ii = lax.broadcasted_iota(jnp.int32, (CHUNK, CHUNK), 0)
    jj = lax.broadcasted_iota(jnp.int32, (CHUNK, CHUNK), 1)
    tri_mask = (ii <= jj) if bwd else (ii >= jj)
    tri = jnp.where(tri_mask, 1.0, 0.0).astype(bf16)
    lane = jj
    last = 0 if bwd else CHUNK - 1
    pair_keep = [jnp.where((lane < SSM_HEAD_DIM) == (a == 0), 1.0, 0.0).astype(bf16) for a in range(2)]

    def chunk(r0, state):
        sl = pl.ds(r0, CHUNK)
        pre = dt_ref[sl, :] + par[0:1]
        dt = jnp.maximum(pre, 0.0) + jnp.log(1.0 + jnp.exp(-jnp.abs(pre)))
        hi, mid, lo = _split3(dt * par[1:2])
        cum = _dot(tri, hi) + _dot(tri, mid) + _dot(tri, lo)
        tot = cum[last:last + 1]
        cum_t = cum.T[0:SUBLANES]
        dt_t = dt.T[0:SUBLANES]
        dtws_t = dt_t * jnp.exp(cum_t[:, last:last + 1] - cum_t)
        dec_x = _expand_heads(jnp.broadcast_to(jnp.exp(tot), (SUBLANES, LANES)), lane)[0:1]

        xs = xs_ref[sl, :]
        xs_b = xs.astype(bf16)
        bm = b_ref[sl, :]
        cm = c_ref[sl, :]
        cb = _dot_nt(cm.astype(bf16), bm.astype(bf16))
        bm_t = bm.T
        state_b = state.astype(bf16)
        inter_all = _dot(cm.astype(bf16), state_b)
        y_parts, upd_parts = [], []
        for v in range(SSM_HG // 2):
            ps = slice(v * LANES, (v + 1) * LANES)
            y_p = upd_p = None
            row_scale = []
            for a in range(2):
                h = 2 * v + a
                xs_h = xs_b[:, ps] * pair_keep[a]
                cbc = jnp.broadcast_to(cum[:, h:h + 1], (CHUNK, CHUNK))
                lmat = jnp.exp(jnp.where(tri_mask, cbc - cum_t[h:h + 1, :], NEG_INF))
                row_scale.append(jnp.exp(cbc))
                t = _dot((cb * lmat * dt_t[h:h + 1, :]).astype(bf16), xs_h)
                y_p = t if y_p is None else y_p + t
                t = _dot((bm_t * dtws_t[h:h + 1, :]).astype(bf16), xs_h)
                upd_p = t if upd_p is None else upd_p + t
            y_parts.append(y_p + inter_all[:, ps] * jnp.where(lane < SSM_HEAD_DIM, row_scale[0], row_scale[1]))
            upd_parts.append(upd_p)
        y = jnp.concatenate(y_parts, axis=1)
        if bwd:
            y = y + prev_ref[sl, :] + xs * skip_ref[...]
            gz = y * _silu(z_ref[sl, :])
            o_ref[sl, :] = _rms(gz, nw_ref[...]).astype(o_ref.dtype)
        else:
            o_ref[sl, :] = y
        return dec_x * state + jnp.concatenate(upd_parts, axis=1)

    _scan_driver(chunk, s_all, bwd=bwd, geom=geom, group=SSD_CHUNK_GROUP)


def ssd_scan(u, d1, par, geom, dt_cb0, bwd, z_cb0=None, skip=None, norm_w=None, prev=None):
    ntok = u.shape[0]
    idx, steps = _scan_row_index(geom, bwd)
    gw = SSM_GW
    nb_blk = SSM_HEADS * SSM_HEAD_DIM // SSM_STATE
    tile = lambda w, cb: pl.BlockSpec((geom["T"], w), lambda g, s, cb=cb: (idx(s), cb(g)))
    in_specs = [tile(gw, lambda g: g),
                tile(SSM_STATE, lambda g: nb_blk + g),
                tile(SSM_STATE, lambda g: nb_blk + SSM_GROUPS + g),
                tile(LANES, lambda g: dt_cb0 + g),
                pl.BlockSpec((1, 2, LANES), lambda g, s: (g, 0, 0))]
    args = [u, u, u, d1, par]
    if bwd:
        in_specs += [tile(gw, lambda g: z_cb0 + g),
                     pl.BlockSpec((1, gw), lambda g, s: (0, g)),
                     pl.BlockSpec((1, gw), lambda g, s: (0, g)),
                     tile(gw, lambda g: g)]
        args += [d1, skip, norm_w, prev]
    return pl.pallas_call(
        functools.partial(_ssd_body, bwd=bwd, geom=geom),
        grid=(SSM_GROUPS, steps),
        in_specs=in_specs,
        out_specs=tile(gw, lambda g: g),
        out_shape=jax.ShapeDtypeStruct((ntok, SSM_GROUPS * gw), bf16 if bwd else f32),
        scratch_shapes=[pltpu.VMEM((geom["B"], SSM_STATE, gw), f32)],
        compiler_params=_cp(("arbitrary", "arbitrary"), VMEM_LIMIT),
        name="ssd_bwd" if bwd else "ssd_fwd",
    )(*args)


def _ret_body(*refs, bwd, geom):
    if bwd:
        lg_ref, q_ref, k_ref, v_ref, g_ref, prev_ref, o_ref, s_all = refs
    else:
        lg_ref, q_ref, k_ref, v_ref, o_ref, s_all = refs

    ii = lax.broadcasted_iota(jnp.int32, (RET_CHUNK, RET_CHUNK), 0).astype(f32)
    jj = lax.broadcasted_iota(jnp.int32, (RET_CHUNK, RET_CHUNK), 1).astype(f32)
    diff = (jj - ii) if bwd else (ii - jj)
    icol = ii[:, 0:1]
    jrow = jj[0:1, :]
    consts = []
    for h in range(RET_HEADS):
        lg = lg_ref[1 if bwd else 0, h]
        dmat = jnp.where(diff >= 0, jnp.exp(jnp.maximum(diff, 0.0) * lg), 0.0)
        if bwd:
            wq = jnp.exp((RET_CHUNK - icol) * lg)
            ws = jnp.exp(jrow * lg)
        else:
            wq = jnp.exp((icol + 1.0) * lg)
            ws = jnp.exp((RET_CHUNK - 1.0 - jrow) * lg)
        consts.append((dmat, wq, ws, jnp.exp(jnp.full((1, 1), float(RET_CHUNK), f32) * lg)))

    def chunk(r0, state):
        sl = pl.ds(r0, RET_CHUNK)
        new_state = []
        for h, (dmat, wq, ws, decay) in enumerate(consts):
            ks = slice(h * RET_DK, (h + 1) * RET_DK)
            vs = slice(h * RET_DV, (h + 1) * RET_DV)
            q = q_ref[sl, ks]
            k = k_ref[sl, ks]
            v = v_ref[sl, vs]
            sc = _dot_nt(q, k) * dmat
            o = _dot(sc.astype(bf16), v) + _dot(q, state[h].astype(bf16)) * wq
            kt = (k.astype(f32).T * ws).astype(bf16)
            if bwd:
                o = o + prev_ref[sl, vs]
                o = o * lax.rsqrt(jnp.mean(o * o, axis=-1, keepdims=True) + NORM_EPS)
                o_ref[sl, vs] = (_silu(g_ref[sl, vs]) * o).astype(o_ref.dtype)
            else:
                o_ref[sl, vs] = o
            new_state.append(decay * state[h] + _dot(kt, v))
        return jnp.stack(new_state)

    _scan_driver(chunk, s_all, bwd=bwd, geom=geom, group=RET_CHUNK_GROUP, rows=RET_CHUNK)


def ret_scan(lg, qk, vb, geom, bwd, d1=None, prev=None):
    ntok = qk.shape[0]
    idx, steps = _scan_row_index(geom, bwd)
    kw, vw = RET_HEADS * RET_DK, RET_HEADS * RET_DV
    tile = lambda w, cb: pl.BlockSpec((geom["T"], w), lambda z, s: (idx(s), cb))
    in_specs = [pl.BlockSpec(memory_space=pltpu.SMEM), tile(kw, 0), tile(kw, 1), tile(vw, 0)]
    args = [lg, qk, qk, vb]
    if bwd:
        in_specs += [tile(vw, 0), tile(vw, 0)]
        args += [d1, prev]
    return pl.pallas_call(
        functools.partial(_ret_body, bwd=bwd, geom=geom),
        grid=(1, steps),
        in_specs=in_specs,
        out_specs=tile(vw, 0),
        out_shape=jax.ShapeDtypeStruct((ntok, vw), bf16 if bwd else f32),
        scratch_shapes=[pltpu.VMEM((geom["B"], RET_HEADS, RET_DK, RET_DV), f32)],
        compiler_params=_cp(("arbitrary", "arbitrary"), VMEM_LIMIT),
        name="ret_bwd" if bwd else "ret_fwd",
    )(*args)


def _att_body(sink_ref, q_ref, kc_ref, vc_ref, kp_ref, ko_ref, kn_ref, vp_ref, vo_ref, vn_ref, o_ref, *, nblk, ctx_blk):
    n = pl.program_id(1) - ctx_blk
    own_ok = n >= 0
    prev_ok = n >= 1
    next_ok = jnp.logical_and(n >= 0, n <= nblk - 2)
    rows = ATT_STACK * CHUNK
    qi = lax.rem(lax.broadcasted_iota(jnp.int32, (rows, CHUNK), 0), CHUNK)
    kj = lax.broadcasted_iota(jnp.int32, (rows, CHUNK), 1)
    m_prev = jnp.logical_and(kj >= qi, prev_ok)
    m_next = jnp.logical_and(kj <= qi, next_ok)
    hd = ATT_HEAD_DIM
    for h0 in range(0, ATT_HEADS, ATT_STACK):
        kv = h0 // ATT_G
        ks = slice(kv * hd, (kv + 1) * hd)
        kc, vc = kc_ref[:, ks], vc_ref[:, ks]
        heads = list(range(h0, h0 + ATT_STACK))
        q = jnp.concatenate([q_ref[:, h * hd:(h + 1) * hd] for h in heads], axis=0)
        sk = jnp.concatenate([jnp.full((CHUNK, 1), sink_ref[h], f32) for h in heads], axis=0)
        s_c = _dot_nt(q, kc)
        s_p = jnp.where(m_prev, _dot_nt(q, kp_ref[:, ks]), NEG_INF)
        s_o = jnp.where(own_ok, _dot_nt(q, ko_ref[:, ks]), NEG_INF)
        s_n = jnp.where(m_next, _dot_nt(q, kn_ref[:, ks]), NEG_INF)
        parts = [s_c[:, c * CHUNK:(c + 1) * CHUNK] for c in range(s_c.shape[1] // CHUNK)] + [s_p, s_o, s_n]
        mx = jnp.maximum(jnp.max(functools.reduce(jnp.maximum, parts), axis=-1, keepdims=True), sk)
        probs = [jnp.exp(t - mx) for t in parts]
        den = jnp.exp(sk - mx) + jnp.sum(functools.reduce(jnp.add, probs), axis=-1, keepdims=True)
        n_c = len(parts) - 3
        p_c = jnp.concatenate(probs[:n_c], axis=1)
        o = (_dot(p_c.astype(bf16), vc) + _dot(probs[n_c].astype(bf16), vp_ref[:, ks])
             + _dot(probs[n_c + 1].astype(bf16), vo_ref[:, ks]) + _dot(probs[n_c + 2].astype(bf16), vn_ref[:, ks]))
        o = o / den
        for g, h in enumerate(heads):
            o_ref[:, h * hd:(h + 1) * hd] = o[g * CHUNK:(g + 1) * CHUNK].astype(o_ref.dtype)


def window_attention(sink, qk, vb, geom):
    ntok = qk.shape[0]
    blk = CHUNK
    nblk = geom["S"] // blk
    ctx_blk = geom["L"] // blk
    ctx0 = geom["nx"] // blk
    kvw = ATT_KV_HEADS * ATT_HEAD_DIM
    qw = ATT_HEADS * ATT_HEAD_DIM
    kcol = qw // kvw

    def q_idx(b, i):
        return jnp.where(i < ctx_blk, ctx0 + b * ctx_blk + i, b * nblk + i - ctx_blk)

    def win(off):
        def f(b, i):
            n = jnp.clip(i - ctx_blk + off, 0, nblk - 1)
            return (b * nblk + n, kcol)
        return f

    ctx_spec = pl.BlockSpec((geom["L"], kvw), lambda b, i: (geom["nx"] // geom["L"] + b, kcol))
    wspec = lambda off: pl.BlockSpec((blk, kvw), win(off))
    return pl.pallas_call(
        functools.partial(_att_body, nblk=nblk, ctx_blk=ctx_blk),
        grid=(geom["B"], ctx_blk + nblk),
        in_specs=[pl.BlockSpec(memory_space=pltpu.SMEM),
                  pl.BlockSpec((blk, qw), lambda b, i: (q_idx(b, i), 0)),
                  ctx_spec, ctx_spec,
                  wspec(-1), wspec(0), wspec(1), wspec(-1), wspec(0), wspec(1)],
        out_specs=pl.BlockSpec((blk, qw), lambda b, i: (q_idx(b, i), 0)),
        out_shape=jax.ShapeDtypeStruct((ntok, qw), bf16),
        compiler_params=_cp(("arbitrary", "arbitrary")),
        name="window_attention",
    )(sink, qk, qk, vb, qk, qk, qk, vb, vb, vb)


def _merge_body(r_ref, s_ref, a_ref, x_ref, npre_ref, wg_ref, wb_ref, wo_ref, nw_ref, mod_ref, o_ref):
    x = x_ref[...]
    m = mod_ref[0]
    h = _norm_mod(x, npre_ref[...], m, 0, 1).astype(bf16)
    acc = None
    for n, ref in enumerate((r_ref, s_ref, a_ref)):
        gate = _dot(h, wg_ref[:, n * D_MODEL:(n + 1) * D_MODEL])
        t = _sigmoid(gate) * _dot(ref[...], wb_ref[n])
        acc = t if acc is None else acc + t
    y = _dot(acc.astype(bf16), wo_ref[...])
    o_ref[...] = x + m[2:3] * _rms(y, nw_ref[...])


def merge(ret_o, ssd_o, att_o, x, npre, wg, wb, wo, nw, mods, geom, rows):
    d = D_MODEL
    tm = geom["tm_merge"]
    row = lambda w: pl.BlockSpec((tm, w), lambda i: (i, 0))
    return pl.pallas_call(
        _merge_body,
        grid=(rows // tm,),
        in_specs=[row(d), row(d), row(d), row(d),
                  pl.BlockSpec((1, d), lambda i: (0, 0)),
                  _resident((d, 3 * d)), _resident((3, d, d)), _resident((d, d)),
                  pl.BlockSpec((1, d), lambda i: (0, 0)),
                  pl.BlockSpec((1, 6, d), _row_tile_mod_idx(geom, tm))],
        out_specs=row(d),
        out_shape=jax.ShapeDtypeStruct((rows, d), f32),
        compiler_params=_cp(("arbitrary",), VMEM_LIMIT),
        name="merge",
    )(ret_o, ssd_o, att_o, x, npre, wg, wb, wo, nw, mods)


def _swiglu(h, w1, w3, w2, f_split):
    step = w1.shape[1] // f_split
    acc = None
    for f in range(f_split):
        fs = slice(f * step, (f + 1) * step)
        act = _silu(_dot(h, w1[:, fs])) * _dot(h, w3[:, fs])
        t = _dot(act.astype(bf16), w2[fs, :])
        acc = t if acc is None else acc + t
    return acc


def _ffn_body(x_ref, npre_ref, npost_ref, mod_ref, w1_ref, w3_ref, w2_ref, o_ref):
    x = x_ref[...]
    m = mod_ref[0]
    h = _norm_mod(x, npre_ref[...], m, 3, 4).astype(bf16)
    y = _swiglu(h, w1_ref, w3_ref, w2_ref, FFN_SPLIT)
    o_ref[...] = x + m[5:6] * _rms(y, npost_ref[...])


def dense_ffn(x, npre, npost, mods, w1, w3, w2, geom, rows):
    d = D_MODEL
    ff = w1.shape[1]
    tm = geom["tm_merge"]
    return pl.pallas_call(
        _ffn_body,
        grid=(rows // tm,),
        in_specs=[pl.BlockSpec((tm, d), lambda i: (i, 0)),
                  pl.BlockSpec((1, d), lambda i: (0, 0)),
                  pl.BlockSpec((1, d), lambda i: (0, 0)),
                  pl.BlockSpec((1, 6, d), _row_tile_mod_idx(geom, tm)),
                  _resident((d, ff)), _resident((d, ff)), _resident((ff, d))],
        out_specs=pl.BlockSpec((tm, d), lambda i: (i, 0)),
        out_shape=jax.ShapeDtypeStruct((rows, d), f32),
        compiler_params=_cp(("arbitrary",), VMEM_LIMIT),
        name="dense_ffn",
    )(x, npre, npost, mods, w1, w3, w2)


def _router_body(x_ref, npre_ref, mod_ref, wr_ref, h_ref, route_ref):
    h = _norm_mod(x_ref[...], npre_ref[...], mod_ref[0], 3, 4)
    h_ref[...] = h
    h_hi = h.astype(bf16)
    h_lo = (h - h_hi.astype(f32)).astype(bf16)
    wr = wr_ref[...]
    w_hi = wr.astype(bf16)
    w_lo = (wr - w_hi.astype(f32)).astype(bf16)
    logits = _dot(h_hi, w_hi) + _dot(h_hi, w_lo) + _dot(h_lo, w_hi) + _dot(h_lo, w_lo)
    lane = lax.broadcasted_iota(jnp.int32, logits.shape, 1).astype(f32)
    l1 = jnp.where(lane < N_EXPERTS, logits, -jnp.inf)
    m1 = jnp.max(l1, axis=-1, keepdims=True)
    i1 = jnp.min(jnp.where(l1 == m1, lane, float(LANES)), axis=-1, keepdims=True)
    l2 = jnp.where(lane == i1, -jnp.inf, l1)
    m2 = jnp.max(l2, axis=-1, keepdims=True)
    i2 = jnp.min(jnp.where(l2 == m2, lane, float(LANES)), axis=-1, keepdims=True)
    e = jnp.exp(m2 - m1)
    w1 = 1.0 / (1.0 + e)
    w2 = e / (1.0 + e)
    route_ref[...] = jnp.where(lane == 0.0, i1, jnp.where(lane == 1.0, i2,
                                                          jnp.where(lane == 2.0, w1, jnp.where(lane == 3.0, w2, 0.0))))


def moe_router(x, npre, mods, w_router, geom, rows):
    d = D_MODEL
    tm = geom["tm_merge"]
    wr = jnp.pad(w_router, ((0, 0), (0, LANES - w_router.shape[1])))
    return pl.pallas_call(
        _router_body,
        grid=(rows // tm,),
        in_specs=[pl.BlockSpec((tm, d), lambda i: (i, 0)),
                  pl.BlockSpec((1, d), lambda i: (0, 0)),
                  pl.BlockSpec((1, 6, d), _row_tile_mod_idx(geom, tm)),
                  pl.BlockSpec((d, LANES), lambda i: (0, 0))],
        out_specs=[pl.BlockSpec((tm, d), lambda i: (i, 0)), pl.BlockSpec((tm, LANES), lambda i: (i, 0))],
        out_shape=[jax.ShapeDtypeStruct((rows, d), f32), jax.ShapeDtypeStruct((rows, LANES), f32)],
        compiler_params=_cp(("arbitrary",)),
        name="moe_router",
    )(x, npre, mods, wr)


def _row_copy(src, src_row, dst, dst_row, sem):
    return pltpu.make_async_copy(src.at[pl.ds(src_row, 1)], dst.at[pl.ds(dst_row, 1)], sem)


def _dispatch_body(slot_ref, h_ref, xb_in, xb_hbm, sem, *, rows):
    del xb_in

    def start(r, c):
        _row_copy(h_ref, r, xb_hbm, slot_ref[0, 0, 2 * r], sem).start()
        _row_copy(h_ref, r, xb_hbm, slot_ref[0, 0, 2 * r + 1], sem).start()
        return c

    lax.fori_loop(0, rows, start, 0, unroll=DMA_ISSUE_UNROLL)
    for _ in range(2):
        pltpu.make_async_copy(h_ref, xb_hbm.at[pl.ds(0, rows)], sem).wait()


def moe_dispatch(h, slots, n_slots):
    n_tok, d = h.shape
    rows = MOE_DISPATCH_ROWS
    steps = n_tok // rows
    return pl.pallas_call(
        functools.partial(_dispatch_body, rows=rows),
        grid=(steps,),
        in_specs=[pl.BlockSpec((1, 1, 2 * rows), lambda i: (i, 0, 0), memory_space=pltpu.SMEM),
                  pl.BlockSpec((rows, d), lambda i: (i, 0)),
                  pl.BlockSpec(memory_space=pl.ANY)],
        out_specs=pl.BlockSpec(memory_space=pl.ANY),
        out_shape=jax.ShapeDtypeStruct((n_slots, d), f32),
        scratch_shapes=[pltpu.SemaphoreType.DMA(())],
        input_output_aliases={2: 0},
        compiler_params=_cp(("arbitrary",), VMEM_LIMIT),
        name="moe_dispatch",
    )(slots.reshape(steps, 1, 2 * rows), h, jnp.zeros((n_slots, d), f32))


def _expert_body(be_ref, cnt_ref, used_ref, x_ref, w1_ref, w3_ref, w2_ref, o_ref):
    del be_ref, used_ref
    i = pl.program_id(0)

    @pl.when(cnt_ref[i] > 0)
    def _():
        o_ref[...] = _swiglu(x_ref[...].astype(bf16), w1_ref.at[0], w3_ref.at[0], w2_ref.at[0], FFN_SPLIT)

    @pl.when(cnt_ref[i] == 0)
    def _():
        o_ref[...] = jnp.zeros_like(o_ref)


def moe_experts(xb, blk_e, blk_cnt, n_used, w1, w3, w2):
    n_slots, d = xb.shape
    ff = w1.shape[2]
    tm = MOE_BLOCK
    grid_spec = pltpu.PrefetchScalarGridSpec(
        num_scalar_prefetch=3,
        grid=(n_slots // tm,),
        in_specs=[pl.BlockSpec((tm, d), lambda i, be, cnt, used: (jnp.minimum(i, used[0] - 1), 0)),
                  pl.BlockSpec((1, d, ff), lambda i, be, cnt, used: (be[i], 0, 0)),
                  pl.BlockSpec((1, d, ff), lambda i, be, cnt, used: (be[i], 0, 0)),
                  pl.BlockSpec((1, ff, d), lambda i, be, cnt, used: (be[i], 0, 0))],
        out_specs=pl.BlockSpec((tm, d), lambda i, be, cnt, used: (i, 0)),
    )
    return pl.pallas_call(
        _expert_body,
        grid_spec=grid_spec,
        out_shape=jax.ShapeDtypeStruct((n_slots, d), f32),
        compiler_params=_cp(("arbitrary",), VMEM_LIMIT),
        name="moe_experts",
    )(blk_e, blk_cnt, n_used, xb, w1, w3, w2)


def _combine_body(slot_ref, route_ref, x_ref, npost_ref, mod_ref, yb_hbm, o_ref, y0_scr, y1_scr, sem, *, rows):
    def start(r, c):
        _row_copy(yb_hbm, slot_ref[0, 0, 2 * r], y0_scr, r, sem).start()
        _row_copy(yb_hbm, slot_ref[0, 0, 2 * r + 1], y1_scr, r, sem).start()
        return c

    lax.fori_loop(0, rows, start, 0, unroll=DMA_ISSUE_UNROLL)
    for buf in (y0_scr, y1_scr):
        pltpu.make_async_copy(yb_hbm.at[pl.ds(0, rows)], buf, sem).wait()
    y = y0_scr[...] * route_ref[:, 2:3] + y1_scr[...] * route_ref[:, 3:4]
    o_ref[...] = x_ref[...] + mod_ref[0][5:6] * _rms(y, npost_ref[...])


def moe_combine(slots, route, x, npost, mods, yb, geom, rows_total):
    d = D_MODEL
    rows = MOE_COMBINE_ROWS
    steps = rows_total // rows
    return pl.pallas_call(
        functools.partial(_combine_body, rows=rows),
        grid=(steps,),
        in_specs=[pl.BlockSpec((1, 1, 2 * rows), lambda i: (i, 0, 0), memory_space=pltpu.SMEM),
                  pl.BlockSpec((rows, LANES), lambda i: (i, 0)),
                  pl.BlockSpec((rows, d), lambda i: (i, 0)),
                  pl.BlockSpec((1, d), lambda i: (0, 0)),
                  pl.BlockSpec((1, 6, d), _row_tile_mod_idx(geom, rows)),
                  pl.BlockSpec(memory_space=pl.ANY)],
        out_specs=pl.BlockSpec((rows, d), lambda i: (i, 0)),
        out_shape=jax.ShapeDtypeStruct((rows_total, d), f32),
        scratch_shapes=[pltpu.VMEM((rows, d), f32), pltpu.VMEM((rows, d), f32), pltpu.SemaphoreType.DMA(())],
        compiler_params=_cp(("arbitrary",), VMEM_LIMIT),
        name="moe_combine",
    )(slots.reshape(steps, 1, 2 * rows), route, x, npost, mods, yb)


def _moe_slots(route, n_tok):
    tm = MOE_BLOCK
    flat_e = route[:, 0:2].astype(jnp.int32).reshape(2 * n_tok)
    onehot = (flat_e[:, None] == jnp.arange(N_EXPERTS, dtype=jnp.int32)[None, :]).astype(jnp.int32)
    csum = jnp.cumsum(onehot, axis=0)
    counts = csum[-1]
    rank = jnp.sum(csum * onehot, axis=1) - 1
    padded = (counts + tm - 1) // tm * tm
    pend = jnp.cumsum(padded)
    pstart = pend - padded
    slots = jnp.sum(onehot * pstart[None, :], axis=1) + rank
    n_blocks = 2 * n_tok // tm + N_EXPERTS
    blk = jnp.arange(n_blocks, dtype=jnp.int32)
    blk_e = jnp.minimum(jnp.sum((blk[:, None] >= (pend // tm)[None, :]).astype(jnp.int32), axis=1), N_EXPERTS - 1)
    n_used = (pend[-1] // tm).astype(jnp.int32)
    cnt = jnp.clip(counts[blk_e] - (blk - pstart[blk_e] // tm) * tm, 0, tm)
    blk_cnt = jnp.where(blk < n_used, cnt, 0).astype(jnp.int32)
    return slots.astype(jnp.int32), blk_e, blk_cnt, n_used.reshape(1), n_blocks * tm


def moe_ffn(x, npre, npost, mods, w_router, w1, w3, w2, geom, rows):
    h, route = moe_router(x, npre, mods, w_router, geom, rows)
    slots, blk_e, blk_cnt, n_used, n_slots = _moe_slots(route, rows)
    xb = moe_dispatch(h, slots, n_slots)
    yb = moe_experts(xb, blk_e, blk_cnt, n_used, w1, w3, w2)
    return moe_combine(slots, route, x, npost, mods, yb, geom, rows)


def _rope_tables(geom):
    s, tm = geom["S"], geom["tm"]
    pos = jnp.arange(s)

    def angles(p, dim):
        inv = ROPE_BASE ** (-jnp.arange(0, dim, 2, dtype=f32) / dim)
        return p.astype(f32)[:, None] * inv[None, :]

    def with_identity(t, one):
        return jnp.concatenate([t, jnp.full((tm, LANES), one, f32)], axis=0)

    a = angles(pos, RET_DK)
    ret_cos = jnp.concatenate([jnp.cos(a), jnp.cos(a)], axis=1)
    ret_sin = jnp.concatenate([-jnp.sin(a), jnp.sin(a)], axis=1)
    half = ATT_HEAD_DIM // 2
    ar = angles(pos // GRID_W, half)
    ac = angles(pos % GRID_W, half)
    z = jnp.zeros_like(ar)
    att_cos = jnp.concatenate([jnp.cos(ar), jnp.cos(ar), jnp.cos(ac), jnp.cos(ac)], axis=1)
    att_s_up = jnp.concatenate([-jnp.sin(ar), z, -jnp.sin(ac), z], axis=1)
    att_s_dn = jnp.concatenate([z, jnp.sin(ar), z, jnp.sin(ac)], axis=1)
    ret = ((RET_DK // 2,), with_identity(ret_cos, 1.0), (with_identity(ret_sin, 0.0),))
    att = ((LANES - half // 2, half // 2), with_identity(att_cos, 1.0),
           (with_identity(att_s_up, 0.0), with_identity(att_s_dn, 0.0)))
    return ret, att


def _pack_w_in(w):
    rq, rk, rv, rg, sz, sxbc, sdt, aq, ak, av, bg = jnp.split(w, np.cumsum(IN_SPLITS)[:-1].tolist(), axis=1)
    pad = lambda t, n: jnp.pad(t, ((0, 0), (0, n - t.shape[1])))
    packed = jnp.concatenate([rq, rk, aq, ak, rv, av, rg, sz,
                              pad(sdt[:, :SSM_HG], LANES), pad(sdt[:, SSM_HG:], LANES), sxbc], axis=1)
    return packed.astype(bf16), bg.astype(bf16)


def _pad_heads(v):
    return jnp.pad(v.astype(f32).reshape(2, SSM_GROUPS, SSM_HG), ((0, 0), (0, 0), (0, LANES - SSM_HG)))


def _geometry(batch, seq, ctx_len):
    tm = 1024
    while seq % tm or (batch * ctx_len) % tm:
        tm //= 2
    tm_merge = min(tm, 512)
    scan_tile = batch * ctx_len
    assert seq % scan_tile == 0 and scan_tile % (SSD_CHUNK_GROUP * CHUNK) == 0 and scan_tile % (RET_CHUNK_GROUP * RET_CHUNK) == 0 and ctx_len % RET_CHUNK == 0 and seq % GRID_W == 0
    return {"B": batch, "S": seq, "L": ctx_len, "nx": batch * seq, "tm": tm, "tm_merge": tm_merge, "T": scan_tile}


def kernel(x, c, ctx, c_ctx, ada_w, ada_b, mix_norm_pre, mix_norm_post, ffn_norm_pre, ffn_norm_post, w_in, ret_decay, ssm_conv_w, ssm_conv_b, ssm_dt_bias, ssm_a_log, ssm_d, ssm_norm_w, att_sink, w_branch, w_out, ffn_w1, ffn_w3, ffn_w2, moe_router, moe_w1, moe_w3, moe_w2):
    batch, seq, d = x.shape
    ctx_len = ctx.shape[1]
    depth = ada_w.shape[0]
    geom = _geometry(batch, seq, ctx_len)
    nx = geom["nx"]
    ntok = nx + batch * ctx_len

    stream = jnp.concatenate([x.reshape(nx, d), ctx.reshape(batch * ctx_len, d)], axis=0)
    cond = jnp.concatenate([c, c_ctx[None, :], jnp.zeros((SUBLANES - 1 - batch % SUBLANES, d), f32)], axis=0)
    mods_all = adaln_all(cond, ada_w, ada_b).reshape(depth, cond.shape[0], 6, d)
    rope_ret, rope_att = _rope_tables(geom)
    row = lambda v: v.reshape(1, -1)
    dt_cb0 = D1_DT_COL // LANES
    z_cb0 = D_MODEL // SSM_GW
    conv_col0 = D1_XBC_COL

    for i in range(depth):
        last = i == depth - 1
        rows = nx if last else ntok
        mods = mods_all[i]
        w_packed, w_gate = _pack_w_in(w_in[i])
        npre = row(mix_norm_pre[i])
        ret_qk, att_qk, val, d1 = in_proj(stream, npre, mods, w_packed, rope_ret, rope_att, geom)

        lg = jax.nn.log_sigmoid(ret_decay[i].astype(f32))
        ret_f = ret_scan(lg, ret_qk, val, geom, bwd=False)
        ret_o = ret_scan(lg, ret_qk, val, geom, bwd=True, d1=d1, prev=ret_f)

        u = ssd_conv(d1, ssm_conv_w[i], ssm_conv_b[i], geom, conv_col0)
        par = jnp.stack([_pad_heads(ssm_dt_bias[i]), _pad_heads(-jnp.exp(ssm_a_log[i].astype(f32)))], axis=2)
        skip = row(jnp.repeat(ssm_d[i, 0].astype(f32) + ssm_d[i, 1].astype(f32), SSM_HEAD_DIM))
        ssd_f = ssd_scan(u, d1, par[0], geom, dt_cb0, bwd=False)
        ssd_o = ssd_scan(u, d1, par[1], geom, dt_cb0, bwd=True, z_cb0=z_cb0, skip=skip,
                         norm_w=row(ssm_norm_w[i]), prev=ssd_f)

        att_o = window_attention(att_sink[i].astype(f32), att_qk, val, geom)

        stream = merge(ret_o, ssd_o, att_o, stream, npre, w_gate, w_branch[i].astype(bf16), w_out[i].astype(bf16),
                       row(mix_norm_post[i]), mods, geom, rows)

        j = i // 2
        fpre, fpost = row(ffn_norm_pre[i]), row(ffn_norm_post[i])
        if i % 2 == 0:
            stream = dense_ffn(stream, fpre, fpost, mods, ffn_w1[j].astype(bf16), ffn_w3[j].astype(bf16),
                               ffn_w2[j].astype(bf16), geom, rows)
        else:
            stream = moe_ffn(stream, fpre, fpost, mods, moe_router[j], moe_w1[j].astype(bf16),
                             moe_w3[j].astype(bf16), moe_w2[j].astype(bf16), geom, rows)
    return stream[:nx].reshape(batch, seq, d)
```

```python
import functools
import math

import jax
import jax.numpy as jnp
import numpy as np
from jax import lax
from jax.experimental import pallas as pl
from jax.experimental.pallas import tpu as pltpu

f32 = jnp.float32
bf16 = jnp.bfloat16

D_MODEL = 1024
GRID_W = 64
CHUNK = 128
NORM_EPS = 1e-6
ROPE_BASE = 10000.0
NEG_INF = -1e30
RET_HEADS, RET_DK, RET_DV = 4, 128, 256
SSM_HEADS, SSM_HEAD_DIM, SSM_GROUPS, SSM_STATE = 16, 64, 2, 128
SSM_HG = SSM_HEADS // SSM_GROUPS
SSM_GW = SSM_HG * SSM_HEAD_DIM
ATT_HEADS, ATT_KV_HEADS, ATT_HEAD_DIM = 8, 2, 128
ATT_G = ATT_HEADS // ATT_KV_HEADS
D_FF = 2816
N_EXPERTS = 8
IN_SPLITS = (512, 512, 1024, 1024, 1024, 1536, 16, 1024, 256, 256, 3072)

LANES = 128
SUBLANES = 8
MOE_BLOCK = 256
MOE_DISPATCH_ROWS = 1024
MOE_COMBINE_ROWS = 1024
VMEM_LIMIT = 56 * 2 ** 20
DMA_ISSUE_UNROLL = 8
ATT_STACK = ATT_G
SSD_CHUNK_GROUP = 8
RET_CHUNK = 256
RET_CHUNK_GROUP = 4
FFN_SPLIT = 2


def _cp(sem, vmem=None):
    return pltpu.CompilerParams(dimension_semantics=sem, vmem_limit_bytes=vmem)


def _silu(v):
    return v / (1.0 + jnp.exp(-v))


def _sigmoid(v):
    return 1.0 / (1.0 + jnp.exp(-v))


def _rms(v, w):
    return v * lax.rsqrt(jnp.mean(v * v, axis=-1, keepdims=True) + NORM_EPS) * w


def _norm_mod(x, nw, m, shift_row, scale_row):
    return _rms(x, nw) * (1.0 + m[scale_row:scale_row + 1]) + m[shift_row:shift_row + 1]


def _dot(a, b):
    return jnp.dot(a, b, preferred_element_type=f32)


def _dot_nt(a, b):
    return lax.dot_general(a, b, (((1,), (1,)), ((), ())), preferred_element_type=f32)


def _split3(a):
    hi = a.astype(bf16)
    r = a - hi.astype(f32)
    mid = r.astype(bf16)
    lo = (r - mid.astype(f32)).astype(bf16)
    return hi, mid, lo


def _row_tile_mod_idx(geom, tm):
    nxt = geom["nx"] // tm
    per_b = geom["S"] // tm
    return lambda i: (jnp.where(i < nxt, i // per_b, geom["B"]), 0, 0)


def _resident(shape):
    return pl.BlockSpec(shape, lambda i: (0,) * len(shape), pipeline_mode=pl.Buffered(1))


def _adaln_body(c_ref, w_ref, b_ref, o_ref):
    s = _silu(c_ref[...])
    o_ref[0] = _dot(s.astype(bf16), w_ref[0].astype(bf16)) + b_ref[0]


def adaln_all(cond, ada_w, ada_b):
    depth, d, n = ada_w.shape
    rows = cond.shape[0]
    tn = 1024
    return pl.pallas_call(
        _adaln_body,
        grid=(depth, n // tn),
        in_specs=[pl.BlockSpec((rows, d), lambda l, j: (0, 0)),
                  pl.BlockSpec((1, d, tn), lambda l, j: (l, 0, j)),
                  pl.BlockSpec((1, 1, tn), lambda l, j: (l, 0, j))],
        out_specs=pl.BlockSpec((1, rows, tn), lambda l, j: (l, 0, j)),
        out_shape=jax.ShapeDtypeStruct((depth, rows, n), f32),
        compiler_params=_cp(("arbitrary", "arbitrary")),
        name="adaln",
    )(cond, ada_w, ada_b.reshape(depth, 1, n))


PROJ_CHUNK = 512
D1_DT_COL = 2 * D_MODEL
D1_XBC_COL = D1_DT_COL + SSM_GROUPS * LANES
D1_WIDTH = D1_XBC_COL + IN_SPLITS[5]


def _proj_plan():
    rq, rk = RET_HEADS * RET_DK, RET_HEADS * RET_DK
    aq, ak = ATT_HEADS * ATT_HEAD_DIM, ATT_KV_HEADS * ATT_HEAD_DIM
    groups = [(0, rq, "ret", 1.0), (0, rk, "ret", RET_DK ** -0.5),
              (1, aq, "att", ATT_HEAD_DIM ** -0.5), (1, ak, "att", 1.0),
              (2, RET_HEADS * RET_DV + ak, None, 1.0),
              (3, D1_WIDTH, None, 1.0)]
    plan, wcol, ocol = [], 0, {}
    for out, width, kind, scale in groups:
        done = 0
        while done < width:
            step = min(PROJ_CHUNK, width - done)
            plan.append((wcol, step, out, ocol.get(out, 0), kind, scale))
            wcol += step
            ocol[out] = ocol.get(out, 0) + step
            done += step
    return plan, wcol, [ocol[o] for o in range(4)]


def _proj_body(x_ref, nw_ref, mod_ref, w_ref, rcos_ref, rsin_ref, acos_ref, aup_ref, adn_ref,
               o_ret, o_att, o_val, o_d1, *, plan, ret_shift, att_shifts):
    outs = (o_ret, o_att, o_val, o_d1)
    h = _norm_mod(x_ref[...], nw_ref[...], mod_ref[0], 0, 1).astype(bf16)
    for wcol, width, out, ocol, kind, scale in plan:
        acc = _dot(h, w_ref[:, wcol:wcol + width])
        o_ref = outs[out]
        if kind is None:
            o_ref[:, ocol:ocol + width] = acc.astype(o_ref.dtype)
            continue
        for c in range(width // LANES):
            t = acc[:, c * LANES:(c + 1) * LANES]
            if kind == "ret":
                r = t * rcos_ref[...] + pltpu.roll(t, ret_shift, 1) * rsin_ref[...]
            else:
                r = (t * acos_ref[...] + pltpu.roll(t, att_shifts[0], 1) * aup_ref[...]
                     + pltpu.roll(t, att_shifts[1], 1) * adn_ref[...])
            o_ref[:, ocol + c * LANES:ocol + (c + 1) * LANES] = (r * scale).astype(o_ref.dtype)


def in_proj(x, nw, mods, w, rope_ret, rope_att, geom):
    ntok, d = x.shape
    tm = geom["tm_merge"]
    plan, wcols, widths = _proj_plan()
    assert w.shape == (d, wcols)
    nxt = geom["nx"] // tm
    per_b = geom["S"] // tm
    pos_idx = lambda i: (jnp.where(i < nxt, i % per_b, per_b), 0)
    table = pl.BlockSpec((tm, LANES), pos_idx)
    dtypes = (bf16, bf16, bf16, f32)
    return pl.pallas_call(
        functools.partial(_proj_body, plan=plan, ret_shift=rope_ret[0][0], att_shifts=rope_att[0]),
        grid=(ntok // tm,),
        in_specs=[pl.BlockSpec((tm, d), lambda i: (i, 0)),
                  pl.BlockSpec((1, d), lambda i: (0, 0)),
                  pl.BlockSpec((1, 6, d), _row_tile_mod_idx(geom, tm)),
                  _resident((d, wcols)),
                  table, table, table, table, table],
        out_specs=[pl.BlockSpec((tm, n), lambda i: (i, 0)) for n in widths],
        out_shape=[jax.ShapeDtypeStruct((ntok, n), dt) for n, dt in zip(widths, dtypes)],
        compiler_params=_cp(("arbitrary",), VMEM_LIMIT),
        name="in_proj",
    )(x, nw, mods, w, rope_ret[1], rope_ret[2][0], rope_att[1], rope_att[2][0], rope_att[2][1])


def _conv_body(x_ref, prev_ref, next_ref, w_ref, b_ref, o_ref, *, tiles_per_seq, nx_tiles, ctx_len):
    i = pl.program_id(0)
    x = x_ref[...]
    rows = x.shape[0]
    t_in = i % tiles_per_seq
    is_x = i < nx_tiles
    use_prev = jnp.logical_and(is_x, t_in > 0)
    use_next = jnp.logical_and(is_x, t_in < tiles_per_seq - 1)
    prow = jnp.where(use_prev, prev_ref[SUBLANES - 1:SUBLANES, :], 0.0)
    nrow = jnp.where(use_next, next_ref[0:1, :], 0.0)
    rid = lax.broadcasted_iota(jnp.int32, x.shape, 0)
    xm = jnp.where(rid == 0, prow, pltpu.roll(x, 1, 0))
    xp = jnp.where(rid == rows - 1, nrow, pltpu.roll(x, rows - 1, 0))
    seg = lax.rem(rid, ctx_len)
    is_ctx = jnp.logical_not(is_x)
    xm = jnp.where(jnp.logical_and(is_ctx, seg == 0), 0.0, xm)
    xp = jnp.where(jnp.logical_and(is_ctx, seg == ctx_len - 1), 0.0, xp)
    w = w_ref[...]
    o_ref[...] = _silu(xm * w[0:1] + x * w[1:2] + xp * w[2:3] + b_ref[...])


def ssd_conv(d1, conv_w, conv_b, geom, col0):
    ntok = d1.shape[0]
    width = conv_w.shape[1]
    tc = width // 2
    cb0 = col0 // tc
    tr = geom["T"]
    per8 = tr // SUBLANES
    last8 = ntok // SUBLANES - 1
    return pl.pallas_call(
        functools.partial(_conv_body, tiles_per_seq=geom["S"] // tr, nx_tiles=geom["nx"] // tr, ctx_len=geom["L"]),
        grid=(ntok // tr, width // tc),
        in_specs=[pl.BlockSpec((tr, tc), lambda i, j: (i, cb0 + j)),
                  pl.BlockSpec((SUBLANES, tc), lambda i, j: (jnp.maximum(i * per8 - 1, 0), cb0 + j)),
                  pl.BlockSpec((SUBLANES, tc), lambda i, j: (jnp.minimum((i + 1) * per8, last8), cb0 + j)),
                  pl.BlockSpec((3, tc), lambda i, j: (0, j)),
                  pl.BlockSpec((1, tc), lambda i, j: (0, j))],
        out_specs=pl.BlockSpec((tr, tc), lambda i, j: (i, j)),
        out_shape=jax.ShapeDtypeStruct((ntok, width), f32),
        compiler_params=_cp(("arbitrary", "arbitrary")),
        name="ssd_conv",
    )(d1, d1, d1, conv_w, conv_b.reshape(1, width))


def _scan_row_index(geom, bwd):
    tile = geom["T"]
    per_b = geom["S"] // tile
    ctx_tile = geom["nx"] // tile

    def idx(s):
        t = s - 1
        xt = (t // per_b) * per_b + (per_b - 1 - t % per_b) if bwd else t
        return jnp.where(s == 0, ctx_tile, xt)

    return idx, ctx_tile + 1


def _scan_driver(chunk, s_all, *, bwd, geom, group, rows=CHUNK):
    tile, ctx_len, nbatch = geom["T"], geom["L"], geom["B"]
    per_b = geom["S"] // tile
    s = pl.program_id(1)

    @pl.when(s == 0)
    def _():
        for b in range(nbatch):
            state = jnp.zeros(s_all.shape[1:], f32)
            cs = range(ctx_len // rows)
            for ci in (reversed(cs) if bwd else cs):
                state = chunk(b * ctx_len + ci * rows, state)
            s_all[b] = state

    @pl.when(s > 0)
    def _():
        st = s_all.at[(s - 1) // per_b]
        n_groups = tile // (group * rows)
        order = tuple(reversed(range(group))) if bwd else tuple(range(group))

        def body(p, state):
            base = ((n_groups - 1 - p) if bwd else p) * (group * rows)
            if not isinstance(base, int):
                base = pl.multiple_of(base, group * rows)
            for ci in order:
                state = chunk(base + ci * rows, state)
            return state

        st[...] = body(0, st[...]) if n_groups == 1 else lax.fori_loop(0, n_groups, body, st[...])


def _expand_heads(a, lane):
    rows = a.shape[0]
    lane = lane[:rows]
    cols = []
    for v in range(SSM_HG // 2):
        left = jnp.broadcast_to(a[:, 2 * v:2 * v + 1], (rows, LANES))
        right = jnp.broadcast_to(a[:, 2 * v + 1:2 * v + 2], (rows, LANES))
        cols.append(jnp.where(lane < SSM_HEAD_DIM, left, right))
    return jnp.concatenate(cols, axis=1)


def _ssd_body(*refs, bwd, geom):
    if bwd:
        xs_ref, b_ref, c_ref, dt_ref, par_ref, z_ref, skip_ref, nw_ref, prev_ref, o_ref, s_all = refs
    else:
        xs_ref, b_ref, c_ref, dt_ref, par_ref, o_ref, s_all = refs

    par = par_ref[0]
    ii = lax.broadcasted_iota(jnp.int32, (CHUNK, CHUNK), 0)
    jj = lax.broadcasted_iota(jnp.int32, (CHUNK, CHUNK), 1)
    tri_mask = (ii <= jj) if bwd else (ii >= jj)
    tri = jnp.where(tri_mask, 1.0, 0.0).astype(bf16)
    lane = jj
    last = 0 if bwd else CHUNK - 1
    pair_keep = [jnp.where((lane < SSM_HEAD_DIM) == (a == 0), 1.0, 0.0).astype(bf16) for a in range(2)]

    def chunk(r0, state):
        sl = pl.ds(r0, CHUNK)
        pre = dt_ref[sl, :] + par[0:1]
        dt = jnp.maximum(pre, 0.0) + jnp.log(1.0 + jnp.exp(-jnp.abs(pre)))
        hi, mid, lo = _split3(dt * par[1:2])
        cum = _dot(tri, hi) + _dot(tri, mid) + _dot(tri, lo)
        tot = cum[last:last + 1]
        cum_t = cum.T[0:SUBLANES]
        dt_t = dt.T[0:SUBLANES]
        dtws_t = dt_t * jnp.exp(cum_t[:, last:last + 1] - cum_t)
        dec_x = _expand_heads(jnp.broadcast_to(jnp.exp(tot), (SUBLANES, LANES)), lane)[0:1]

        xs = xs_ref[sl, :]
        xs_b = xs.astype(bf16)
        bm = b_ref[sl, :]
        cm = c_ref[sl, :]
        cb = _dot_nt(cm.astype(bf16), bm.astype(bf16))
        bm_t = bm.T
        state_b = state.astype(bf16)
        inter_all = _dot(cm.astype(bf16), state_b)
        y_parts, upd_parts = [], []
        for v in range(SSM_HG // 2):
            ps = slice(v * LANES, (v + 1) * LANES)
            y_p = upd_p = None
            row_scale = []
            for a in range(2):
                h = 2 * v + a
                xs_h = xs_b[:, ps] * pair_keep[a]
                cbc = jnp.broadcast_to(cum[:, h:h + 1], (CHUNK, CHUNK))
                lmat = jnp.exp(jnp.where(tri_mask, cbc - cum_t[h:h + 1, :], NEG_INF))
                row_scale.append(jnp.exp(cbc))
                t = _dot((cb * lmat * dt_t[h:h + 1, :]).astype(bf16), xs_h)
                y_p = t if y_p is None else y_p + t
                t = _dot((bm_t * dtws_t[h:h + 1, :]).astype(bf16), xs_h)
                upd_p = t if upd_p is None else upd_p + t
            y_parts.append(y_p + inter_all[:, ps] * jnp.where(lane < SSM_HEAD_DIM, row_scale[0], row_scale[1]))
            upd_parts.append(upd_p)
        y = jnp.concatenate(y_parts, axis=1)
        if bwd:
            y = y + prev_ref[sl, :] + xs * skip_ref[...]
            gz = y * _silu(z_ref[sl, :])
            o_ref[sl, :] = _rms(gz, nw_ref[...]).astype(o_ref.dtype)
        else:
            o_ref[sl, :] = y
        return dec_x * state + jnp.concatenate(upd_parts, axis=1)

    _scan_driver(chunk, s_all, bwd=bwd, geom=geom, group=SSD_CHUNK_GROUP)


def ssd_scan(u, d1, par, geom, dt_cb0, bwd, z_cb0=None, skip=None, norm_w=None, prev=None):
    ntok = u.shape[0]
    idx, steps = _scan_row_index(geom, bwd)
    gw = SSM_GW
    nb_blk = SSM_HEADS * SSM_HEAD_DIM // SSM_STATE
    tile = lambda w, cb: pl.BlockSpec((geom["T"], w), lambda g, s, cb=cb: (idx(s), cb(g)))
    in_specs = [tile(gw, lambda g: g),
                tile(SSM_STATE, lambda g: nb_blk + g),
                tile(SSM_STATE, lambda g: nb_blk + SSM_GROUPS + g),
                tile(LANES, lambda g: dt_cb0 + g),
                pl.BlockSpec((1, 2, LANES), lambda g, s: (g, 0, 0))]
    args = [u, u, u, d1, par]
    if bwd:
        in_specs += [tile(gw, lambda g: z_cb0 + g),
                     pl.BlockSpec((1, gw), lambda g, s: (0, g)),
                     pl.BlockSpec((1, gw), lambda g, s: (0, g)),
                     tile(gw, lambda g: g)]
        args += [d1, skip, norm_w, prev]
    return pl.pallas_call(
        functools.partial(_ssd_body, bwd=bwd, geom=geom),
        grid=(SSM_GROUPS, steps),
        in_specs=in_specs,
        out_specs=tile(gw, lambda g: g),
        out_shape=jax.ShapeDtypeStruct((ntok, SSM_GROUPS * gw), bf16 if bwd else f32),
        scratch_shapes=[pltpu.VMEM((geom["B"], SSM_STATE, gw), f32)],
        compiler_params=_cp(("arbitrary", "arbitrary"), VMEM_LIMIT),
        name="ssd_bwd" if bwd else "ssd_fwd",
    )(*args)


def _ret_body(*refs, bwd, geom):
    if bwd:
        lg_ref, q_ref, k_ref, v_ref, g_ref, prev_ref, o_ref, s_all = refs
    else:
        lg_ref, q_ref, k_ref, v_ref, o_ref, s_all = refs

    ii = lax.broadcasted_iota(jnp.int32, (RET_CHUNK, RET_CHUNK), 0).astype(f32)
    jj = lax.broadcasted_iota(jnp.int32, (RET_CHUNK, RET_CHUNK), 1).astype(f32)
    diff = (jj - ii) if bwd else (ii - jj)
    icol = ii[:, 0:1]
    jrow = jj[0:1, :]
    consts = []
    for h in range(RET_HEADS):
        lg = lg_ref[1 if bwd else 0, h]
        dmat = jnp.where(diff >= 0, jnp.exp(jnp.maximum(diff, 0.0) * lg), 0.0)
        if bwd:
            wq = jnp.exp((RET_CHUNK - icol) * lg)
            ws = jnp.exp(jrow * lg)
        else:
            wq = jnp.exp((icol + 1.0) * lg)
            ws = jnp.exp((RET_CHUNK - 1.0 - jrow) * lg)
        consts.append((dmat, wq, ws, jnp.exp(jnp.full((1, 1), float(RET_CHUNK), f32) * lg)))

    def chunk(r0, state):
        sl = pl.ds(r0, RET_CHUNK)
        new_state = []
        for h, (dmat, wq, ws, decay) in enumerate(consts):
            ks = slice(h * RET_DK, (h + 1) * RET_DK)
            vs = slice(h * RET_DV, (h + 1) * RET_DV)
            q = q_ref[sl, ks]
            k = k_ref[sl, ks]
            v = v_ref[sl, vs]
            sc = _dot_nt(q, k) * dmat
            o = _dot(sc.astype(bf16), v) + _dot(q, state[h].astype(bf16)) * wq
            kt = (k.astype(f32).T * ws).astype(bf16)
            if bwd:
                o = o + prev_ref[sl, vs]
                o = o * lax.rsqrt(jnp.mean(o * o, axis=-1, keepdims=True) + NORM_EPS)
                o_ref[sl, vs] = (_silu(g_ref[sl, vs]) * o).astype(o_ref.dtype)
            else:
                o_ref[sl, vs] = o
            new_state.append(decay * state[h] + _dot(kt, v))
        return jnp.stack(new_state)

    _scan_driver(chunk, s_all, bwd=bwd, geom=geom, group=RET_CHUNK_GROUP, rows=RET_CHUNK)


def ret_scan(lg, qk, vb, geom, bwd, d1=None, prev=None):
    ntok = qk.shape[0]
    idx, steps = _scan_row_index(geom, bwd)
    kw, vw = RET_HEADS * RET_DK, RET_HEADS * RET_DV
    tile = lambda w, cb: pl.BlockSpec((geom["T"], w), lambda z, s: (idx(s), cb))
    in_specs = [pl.BlockSpec(memory_space=pltpu.SMEM), tile(kw, 0), tile(kw, 1), tile(vw, 0)]
    args = [lg, qk, qk, vb]
    if bwd:
        in_specs += [tile(vw, 0), tile(vw, 0)]
        args += [d1, prev]
    return pl.pallas_call(
        functools.partial(_ret_body, bwd=bwd, geom=geom),
        grid=(1, steps),
        in_specs=in_specs,
        out_specs=tile(vw, 0),
        out_shape=jax.ShapeDtypeStruct((ntok, vw), bf16 if bwd else f32),
        scratch_shapes=[pltpu.VMEM((geom["B"], RET_HEADS, RET_DK, RET_DV), f32)],
        compiler_params=_cp(("arbitrary", "arbitrary"), VMEM_LIMIT),
        name="ret_bwd" if bwd else "ret_fwd",
    )(*args)


def _att_body(sink_ref, q_ref, kc_ref, vc_ref, kp_ref, ko_ref, kn_ref, vp_ref, vo_ref, vn_ref, o_ref, *, nblk, ctx_blk):
    n = pl.program_id(1) - ctx_blk
    own_ok = n >= 0
    prev_ok = n >= 1
    next_ok = jnp.logical_and(n >= 0, n <= nblk - 2)
    rows = ATT_STACK * CHUNK
    qi = lax.rem(lax.broadcasted_iota(jnp.int32, (rows, CHUNK), 0), CHUNK)
    kj = lax.broadcasted_iota(jnp.int32, (rows, CHUNK), 1)
    m_prev = jnp.logical_and(kj >= qi, prev_ok)
    m_next = jnp.logical_and(kj <= qi, next_ok)
    hd = ATT_HEAD_DIM
    for h0 in range(0, ATT_HEADS, ATT_STACK):
        kv = h0 // ATT_G
        ks = slice(kv * hd, (kv + 1) * hd)
        kc, vc = kc_ref[:, ks], vc_ref[:, ks]
        heads = list(range(h0, h0 + ATT_STACK))
        q = jnp.concatenate([q_ref[:, h * hd:(h + 1) * hd] for h in heads], axis=0)
        sk = jnp.concatenate([jnp.full((CHUNK, 1), sink_ref[h], f32) for h in heads], axis=0)
        s_c = _dot_nt(q, kc)
        s_p = jnp.where(m_prev, _dot_nt(q, kp_ref[:, ks]), NEG_INF)
        s_o = jnp.where(own_ok, _dot_nt(q, ko_ref[:, ks]), NEG_INF)
        s_n = jnp.where(m_next, _dot_nt(q, kn_ref[:, ks]), NEG_INF)
        parts = [s_c[:, c * CHUNK:(c + 1) * CHUNK] for c in range(s_c.shape[1] // CHUNK)] + [s_p, s_o, s_n]
        mx = jnp.maximum(jnp.max(functools.reduce(jnp.maximum, parts), axis=-1, keepdims=True), sk)
        probs = [jnp.exp(t - mx) for t in parts]
        den = jnp.exp(sk - mx) + jnp.sum(functools.reduce(jnp.add, probs), axis=-1, keepdims=True)
        n_c = len(parts) - 3
        p_c = jnp.concatenate(probs[:n_c], axis=1)
        o = (_dot(p_c.astype(bf16), vc) + _dot(probs[n_c].astype(bf16), vp_ref[:, ks])
             + _dot(probs[n_c + 1].astype(bf16), vo_ref[:, ks]) + _dot(probs[n_c + 2].astype(bf16), vn_ref[:, ks]))
        o = o / den
        for g, h in enumerate(heads):
            o_ref[:, h * hd:(h + 1) * hd] = o[g * CHUNK:(g + 1) * CHUNK].astype(o_ref.dtype)


def window_attention(sink, qk, vb, geom):
    ntok = qk.shape[0]
    blk = CHUNK
    nblk = geom["S"] // blk
    ctx_blk = geom["L"] // blk
    ctx0 = geom["nx"] // blk
    kvw = ATT_KV_HEADS * ATT_HEAD_DIM
    qw = ATT_HEADS * ATT_HEAD_DIM
    kcol = qw // kvw

    def q_idx(b, i):
        return jnp.where(i < ctx_blk, ctx0 + b * ctx_blk + i, b * nblk + i - ctx_blk)

    def win(off):
        def f(b, i):
            n = jnp.clip(i - ctx_blk + off, 0, nblk - 1)
            return (b * nblk + n, kcol)
        return f

    ctx_spec = pl.BlockSpec((geom["L"], kvw), lambda b, i: (geom["nx"] // geom["L"] + b, kcol))
    wspec = lambda off: pl.BlockSpec((blk, kvw), win(off))
    return pl.pallas_call(
        functools.partial(_att_body, nblk=nblk, ctx_blk=ctx_blk),
        grid=(geom["B"], ctx_blk + nblk),
        in_specs=[pl.BlockSpec(memory_space=pltpu.SMEM),
                  pl.BlockSpec((blk, qw), lambda b, i: (q_idx(b, i), 0)),
                  ctx_spec, ctx_spec,
                  wspec(-1), wspec(0), wspec(1), wspec(-1), wspec(0), wspec(1)],
        out_specs=pl.BlockSpec((blk, qw), lambda b, i: (q_idx(b, i), 0)),
        out_shape=jax.ShapeDtypeStruct((ntok, qw), bf16),
        compiler_params=_cp(("arbitrary", "arbitrary")),
        name="window_attention",
    )(sink, qk, qk, vb, qk, qk, qk, vb, vb, vb)


def _merge_body(r_ref, s_ref, a_ref, x_ref, npre_ref, wg_ref, wb_ref, wo_ref, nw_ref, mod_ref, o_ref):
    x = x_ref[...]
    m = mod_ref[0]
    h = _norm_mod(x, npre_ref[...], m, 0, 1).astype(bf16)
    acc = None
    for n, ref in enumerate((r_ref, s_ref, a_ref)):
        gate = _dot(h, wg_ref[:, n * D_MODEL:(n + 1) * D_MODEL])
        t = _sigmoid(gate) * _dot(ref[...], wb_ref[n])
        acc = t if acc is None else acc + t
    y = _dot(acc.astype(bf16), wo_ref[...])
    o_ref[...] = x + m[2:3] * _rms(y, nw_ref[...])


def merge(ret_o, ssd_o, att_o, x, npre, wg, wb, wo, nw, mods, geom, rows):
    d = D_MODEL
    tm = geom["tm_merge"]
    row = lambda w: pl.BlockSpec((tm, w), lambda i: (i, 0))
    return pl.pallas_call(
        _merge_body,
        grid=(rows // tm,),
        in_specs=[row(d), row(d), row(d), row(d),
                  pl.BlockSpec((1, d), lambda i: (0, 0)),
                  _resident((d, 3 * d)), _resident((3, d, d)), _resident((d, d)),
                  pl.BlockSpec((1, d), lambda i: (0, 0)),
                  pl.BlockSpec((1, 6, d), _row_tile_mod_idx(geom, tm))],
        out_specs=row(d),
        out_shape=jax.ShapeDtypeStruct((rows, d), f32),
        compiler_params=_cp(("arbitrary",), VMEM_LIMIT),
        name="merge",
    )(ret_o, ssd_o, att_o, x, npre, wg, wb, wo, nw, mods)


def _swiglu(h, w1, w3, w2, f_split):
    step = w1.shape[1] // f_split
    acc = None
    for f in range(f_split):
        fs = slice(f * step, (f + 1) * step)
        act = _silu(_dot(h, w1[:, fs])) * _dot(h, w3[:, fs])
        t = _dot(act.astype(bf16), w2[fs, :])
        acc = t if acc is None else acc + t
    return acc


def _ffn_body(x_ref, npre_ref, npost_ref, mod_ref, w1_ref, w3_ref, w2_ref, o_ref):
    x = x_ref[...]
    m = mod_ref[0]
    h = _norm_mod(x, npre_ref[...], m, 3, 4).astype(bf16)
    y = _swiglu(h, w1_ref, w3_ref, w2_ref, FFN_SPLIT)
    o_ref[...] = x + m[5:6] * _rms(y, npost_ref[...])


def dense_ffn(x, npre, npost, mods, w1, w3, w2, geom, rows):
    d = D_MODEL
    ff = w1.shape[1]
    tm = geom["tm_merge"]
    return pl.pallas_call(
        _ffn_body,
        grid=(rows // tm,),
        in_specs=[pl.BlockSpec((tm, d), lambda i: (i, 0)),
                  pl.BlockSpec((1, d), lambda i: (0, 0)),
                  pl.BlockSpec((1, d), lambda i: (0, 0)),
                  pl.BlockSpec((1, 6, d), _row_tile_mod_idx(geom, tm)),
                  _resident((d, ff)), _resident((d, ff)), _resident((ff, d))],
        out_specs=pl.BlockSpec((tm, d), lambda i: (i, 0)),
        out_shape=jax.ShapeDtypeStruct((rows, d), f32),
        compiler_params=_cp(("arbitrary",), VMEM_LIMIT),
        name="dense_ffn",
    )(x, npre, npost, mods, w1, w3, w2)


def _router_body(x_ref, npre_ref, mod_ref, wr_ref, h_ref, route_ref):
    h = _norm_mod(x_ref[...], npre_ref[...], mod_ref[0], 3, 4)
    h_ref[...] = h
    h_hi = h.astype(bf16)
    h_lo = (h - h_hi.astype(f32)).astype(bf16)
    wr = wr_ref[...]
    w_hi = wr.astype(bf16)
    w_lo = (wr - w_hi.astype(f32)).astype(bf16)
    logits = _dot(h_hi, w_hi) + _dot(h_hi, w_lo) + _dot(h_lo, w_hi) + _dot(h_lo, w_lo)
    lane = lax.broadcasted_iota(jnp.int32, logits.shape, 1).astype(f32)
    l1 = jnp.where(lane < N_EXPERTS, logits, -jnp.inf)
    m1 = jnp.max(l1, axis=-1, keepdims=True)
    i1 = jnp.min(jnp.where(l1 == m1, lane, float(LANES)), axis=-1, keepdims=True)
    l2 = jnp.where(lane == i1, -jnp.inf, l1)
    m2 = jnp.max(l2, axis=-1, keepdims=True)
    i2 = jnp.min(jnp.where(l2 == m2, lane, float(LANES)), axis=-1, keepdims=True)
    e = jnp.exp(m2 - m1)
    w1 = 1.0 / (1.0 + e)
    w2 = e / (1.0 + e)
    route_ref[...] = jnp.where(lane == 0.0, i1, jnp.where(lane == 1.0, i2,
                                                          jnp.where(lane == 2.0, w1, jnp.where(lane == 3.0, w2, 0.0))))


def moe_router(x, npre, mods, w_router, geom, rows):
    d = D_MODEL
    tm = geom["tm_merge"]
    wr = jnp.pad(w_router, ((0, 0), (0, LANES - w_router.shape[1])))
    return pl.pallas_call(
        _router_body,
        grid=(rows // tm,),
        in_specs=[pl.BlockSpec((tm, d), lambda i: (i, 0)),
                  pl.BlockSpec((1, d), lambda i: (0, 0)),
                  pl.BlockSpec((1, 6, d), _row_tile_mod_idx(geom, tm)),
                  pl.BlockSpec((d, LANES), lambda i: (0, 0))],
        out_specs=[pl.BlockSpec((tm, d), lambda i: (i, 0)), pl.BlockSpec((tm, LANES), lambda i: (i, 0))],
        out_shape=[jax.ShapeDtypeStruct((rows, d), f32), jax.ShapeDtypeStruct((rows, LANES), f32)],
        compiler_params=_cp(("arbitrary",)),
        name="moe_router",
    )(x, npre, mods, wr)


def _row_copy(src, src_row, dst, dst_row, sem):
    return pltpu.make_async_copy(src.at[pl.ds(src_row, 1)], dst.at[pl.ds(dst_row, 1)], sem)


def _dispatch_body(slot_ref, h_ref, xb_in, xb_hbm, sem, *, rows):
    del xb_in

    def start(r, c):
        _row_copy(h_ref, r, xb_hbm, slot_ref[0, 0, 2 * r], sem).start(priority=0)
        _row_copy(h_ref, r, xb_hbm, slot_ref[0, 0, 2 * r + 1], sem).start(priority=1)
        return c

    lax.fori_loop(0, rows, start, 0, unroll=DMA_ISSUE_UNROLL)
    for _ in range(2):
        pltpu.make_async_copy(h_ref, xb_hbm.at[pl.ds(0, rows)], sem).wait()


def moe_dispatch(h, slots, n_slots):
    n_tok, d = h.shape
    rows = MOE_DISPATCH_ROWS
    steps = n_tok // rows
    return pl.pallas_call(
        functools.partial(_dispatch_body, rows=rows),
        grid=(steps,),
        in_specs=[pl.BlockSpec((1, 1, 2 * rows), lambda i: (i, 0, 0), memory_space=pltpu.SMEM),
                  pl.BlockSpec((rows, d), lambda i: (i, 0)),
                  pl.BlockSpec(memory_space=pl.ANY)],
        out_specs=pl.BlockSpec(memory_space=pl.ANY),
        out_shape=jax.ShapeDtypeStruct((n_slots, d), f32),
        scratch_shapes=[pltpu.SemaphoreType.DMA(())],
        input_output_aliases={2: 0},
        compiler_params=_cp(("arbitrary",), VMEM_LIMIT),
        name="moe_dispatch",
    )(slots.reshape(steps, 1, 2 * rows), h, jnp.zeros((n_slots, d), f32))


def _expert_body(be_ref, cnt_ref, used_ref, x_ref, w1_ref, w3_ref, w2_ref, o_ref):
    del be_ref, used_ref
    i = pl.program_id(0)

    @pl.when(cnt_ref[i] > 0)
    def _():
        o_ref[...] = _swiglu(x_ref[...].astype(bf16), w1_ref.at[0], w3_ref.at[0], w2_ref.at[0], FFN_SPLIT)

    @pl.when(cnt_ref[i] == 0)
    def _():
        o_ref[...] = jnp.zeros_like(o_ref)


def moe_experts(xb, blk_e, blk_cnt, n_used, w1, w3, w2):
    n_slots, d = xb.shape
    ff = w1.shape[2]
    tm = MOE_BLOCK
    grid_spec = pltpu.PrefetchScalarGridSpec(
        num_scalar_prefetch=3,
        grid=(n_slots // tm,),
        in_specs=[pl.BlockSpec((tm, d), lambda i, be, cnt, used: (jnp.minimum(i, used[0] - 1), 0)),
                  pl.BlockSpec((1, d, ff), lambda i, be, cnt, used: (be[i], 0, 0)),
                  pl.BlockSpec((1, d, ff), lambda i, be, cnt, used: (be[i], 0, 0)),
                  pl.BlockSpec((1, ff, d), lambda i, be, cnt, used: (be[i], 0, 0))],
        out_specs=pl.BlockSpec((tm, d), lambda i, be, cnt, used: (i, 0)),
    )
    return pl.pallas_call(
        _expert_body,
        grid_spec=grid_spec,
        out_shape=jax.ShapeDtypeStruct((n_slots, d), f32),
        compiler_params=_cp(("arbitrary",), VMEM_LIMIT),
        name="moe_experts",
    )(blk_e, blk_cnt, n_used, xb, w1, w3, w2)


def _combine_body(slot_ref, route_ref, x_ref, npost_ref, mod_ref, yb_hbm, o_ref, y0_scr, y1_scr, sem, *, rows):
    def start(r, c):
        _row_copy(yb_hbm, slot_ref[0, 0, 2 * r], y0_scr, r, sem).start(priority=0)
        _row_copy(yb_hbm, slot_ref[0, 0, 2 * r + 1], y1_scr, r, sem).start(priority=1)
        return c

    lax.fori_loop(0, rows, start, 0, unroll=DMA_ISSUE_UNROLL)
    for buf in (y0_scr, y1_scr):
        pltpu.make_async_copy(yb_hbm.at[pl.ds(0, rows)], buf, sem).wait()
    y = y0_scr[...] * route_ref[:, 2:3] + y1_scr[...] * route_ref[:, 3:4]
    o_ref[...] = x_ref[...] + mod_ref[0][5:6] * _rms(y, npost_ref[...])


def moe_combine(slots, route, x, npost, mods, yb, geom, rows_total):
    d = D_MODEL
    rows = MOE_COMBINE_ROWS
    steps = rows_total // rows
    return pl.pallas_call(
        functools.partial(_combine_body, rows=rows),
        grid=(steps,),
        in_specs=[pl.BlockSpec((1, 1, 2 * rows), lambda i: (i, 0, 0), memory_space=pltpu.SMEM),
                  pl.BlockSpec((rows, LANES), lambda i: (i, 0)),
                  pl.BlockSpec((rows, d), lambda i: (i, 0)),
                  pl.BlockSpec((1, d), lambda i: (0, 0)),
                  pl.BlockSpec((1, 6, d), _row_tile_mod_idx(geom, rows)),
                  pl.BlockSpec(memory_space=pl.ANY)],
        out_specs=pl.BlockSpec((rows, d), lambda i: (i, 0)),
        out_shape=jax.ShapeDtypeStruct((rows_total, d), f32),
        scratch_shapes=[pltpu.VMEM((rows, d), f32), pltpu.VMEM((rows, d), f32), pltpu.SemaphoreType.DMA(())],
        compiler_params=_cp(("arbitrary",), VMEM_LIMIT),
        name="moe_combine",
    )(slots.reshape(steps, 1, 2 * rows), route, x, npost, mods, yb)


def _moe_slots(route, n_tok):
    tm = MOE_BLOCK
    flat_e = route[:, 0:2].astype(jnp.int32).reshape(2 * n_tok)
    onehot = (flat_e[:, None] == jnp.arange(N_EXPERTS, dtype=jnp.int32)[None, :]).astype(jnp.int32)
    csum = jnp.cumsum(onehot, axis=0)
    counts = csum[-1]
    rank = jnp.sum(csum * onehot, axis=1) - 1
    padded = (counts + tm - 1) // tm * tm
    pend = jnp.cumsum(padded)
    pstart = pend - padded
    slots = jnp.sum(onehot * pstart[None, :], axis=1) + rank
    n_blocks = 2 * n_tok // tm + N_EXPERTS
    blk = jnp.arange(n_blocks, dtype=jnp.int32)
    blk_e = jnp.minimum(jnp.sum((blk[:, None] >= (pend // tm)[None, :]).astype(jnp.int32), axis=1), N_EXPERTS - 1)
    n_used = (pend[-1] // tm).astype(jnp.int32)
    cnt = jnp.clip(counts[blk_e] - (blk - pstart[blk_e] // tm) * tm, 0, tm)
    blk_cnt = jnp.where(blk < n_used, cnt, 0).astype(jnp.int32)
    return slots.astype(jnp.int32), blk_e, blk_cnt, n_used.reshape(1), n_blocks * tm


def moe_ffn(x, npre, npost, mods, w_router, w1, w3, w2, geom, rows):
    h, route = moe_router(x, npre, mods, w_router, geom, rows)
    slots, blk_e, blk_cnt, n_used, n_slots = _moe_slots(route, rows)
    xb = moe_dispatch(h, slots, n_slots)
    yb = moe_experts(xb, blk_e, blk_cnt, n_used, w1, w3, w2)
    return moe_combine(slots, route, x, npost, mods, yb, geom, rows)


def _rope_tables(geom):
    s, tm = geom["S"], geom["tm"]
    pos = jnp.arange(s)

    def angles(p, dim):
        inv = ROPE_BASE ** (-jnp.arange(0, dim, 2, dtype=f32) / dim)
        return p.astype(f32)[:, None] * inv[None, :]

    def with_identity(t, one):
        return jnp.concatenate([t, jnp.full((tm, LANES), one, f32)], axis=0)

    a = angles(pos, RET_DK)
    ret_cos = jnp.concatenate([jnp.cos(a), jnp.cos(a)], axis=1)
    ret_sin = jnp.concatenate([-jnp.sin(a), jnp.sin(a)], axis=1)
    half = ATT_HEAD_DIM // 2
    ar = angles(pos // GRID_W, half)
    ac = angles(pos % GRID_W, half)
    z = jnp.zeros_like(ar)
    att_cos = jnp.concatenate([jnp.cos(ar), jnp.cos(ar), jnp.cos(ac), jnp.cos(ac)], axis=1)
    att_s_up = jnp.concatenate([-jnp.sin(ar), z, -jnp.sin(ac), z], axis=1)
    att_s_dn = jnp.concatenate([z, jnp.sin(ar), z, jnp.sin(ac)], axis=1)
    ret = ((RET_DK // 2,), with_identity(ret_cos, 1.0), (with_identity(ret_sin, 0.0),))
    att = ((LANES - half // 2, half // 2), with_identity(att_cos, 1.0),
           (with_identity(att_s_up, 0.0), with_identity(att_s_dn, 0.0)))
    return ret, att


def _pack_w_in(w):
    rq, rk, rv, rg, sz, sxbc, sdt, aq, ak, av, bg = jnp.split(w, np.cumsum(IN_SPLITS)[:-1].tolist(), axis=1)
    pad = lambda t, n: jnp.pad(t, ((0, 0), (0, n - t.shape[1])))
    packed = jnp.concatenate([rq, rk, aq, ak, rv, av, rg, sz,
                              pad(sdt[:, :SSM_HG], LANES), pad(sdt[:, SSM_HG:], LANES), sxbc], axis=1)
    return packed.astype(bf16), bg.astype(bf16)


def _pad_heads(v):
    return jnp.pad(v.astype(f32).reshape(2, SSM_GROUPS, SSM_HG), ((0, 0), (0, 0), (0, LANES - SSM_HG)))


def _geometry(batch, seq, ctx_len):
    tm = 1024
    while seq % tm or (batch * ctx_len) % tm:
        tm //= 2
    tm_merge = min(tm, 512)
    scan_tile = batch * ctx_len
    assert seq % scan_tile == 0 and scan_tile % (SSD_CHUNK_GROUP * CHUNK) == 0 and scan_tile % (RET_CHUNK_GROUP * RET_CHUNK) == 0 and ctx_len % RET_CHUNK == 0 and seq % GRID_W == 0
    return {"B": batch, "S": seq, "L": ctx_len, "nx": batch * seq, "tm": tm, "tm_merge": tm_merge, "T": scan_tile}


def kernel(x, c, ctx, c_ctx, ada_w, ada_b, mix_norm_pre, mix_norm_post, ffn_norm_pre, ffn_norm_post, w_in, ret_decay, ssm_conv_w, ssm_conv_b, ssm_dt_bias, ssm_a_log, ssm_d, ssm_norm_w, att_sink, w_branch, w_out, ffn_w1, ffn_w3, ffn_w2, moe_router, moe_w1, moe_w3, moe_w2):
    batch, seq, d = x.shape
    ctx_len = ctx.shape[1]
    depth = ada_w.shape[0]
    geom = _geometry(batch, seq, ctx_len)
    nx = geom["nx"]
    ntok = nx + batch * ctx_len

    stream = jnp.concatenate([x.reshape(nx, d), ctx.reshape(batch * ctx_len, d)], axis=0)
    cond = jnp.concatenate([c, c_ctx[None, :], jnp.zeros((SUBLANES - 1 - batch % SUBLANES, d), f32)], axis=0)
    mods_all = adaln_all(cond, ada_w, ada_b).reshape(depth, cond.shape[0], 6, d)
    rope_ret, rope_att = _rope_tables(geom)
    row = lambda v: v.reshape(1, -1)
    dt_cb0 = D1_DT_COL // LANES
    z_cb0 = D_MODEL // SSM_GW
    conv_col0 = D1_XBC_COL

    for i in range(depth):
        last = i == depth - 1
        rows = nx if last else ntok
        mods = mods_all[i]
        w_packed, w_gate = _pack_w_in(w_in[i])
        npre = row(mix_norm_pre[i])
        ret_qk, att_qk, val, d1 = in_proj(stream, npre, mods, w_packed, rope_ret, rope_att, geom)

        lg = jax.nn.log_sigmoid(ret_decay[i].astype(f32))
        ret_f = ret_scan(lg, ret_qk, val, geom, bwd=False)
        ret_o = ret_scan(lg, ret_qk, val, geom, bwd=True, d1=d1, prev=ret_f)

        u = ssd_conv(d1, ssm_conv_w[i], ssm_conv_b[i], geom, conv_col0)
        par = jnp.stack([_pad_heads(ssm_dt_bias[i]), _pad_heads(-jnp.exp(ssm_a_log[i].astype(f32)))], axis=2)
        skip = row(jnp.repeat(ssm_d[i, 0].astype(f32) + ssm_d[i, 1].astype(f32), SSM_HEAD_DIM))
        ssd_f = ssd_scan(u, d1, par[0], geom, dt_cb0, bwd=False)
        ssd_o = ssd_scan(u, d1, par[1], geom, dt_cb0, bwd=True, z_cb0=z_cb0, skip=skip,
                         norm_w=row(ssm_norm_w[i]), prev=ssd_f)

        att_o = window_attention(att_sink[i].astype(f32), att_qk, val, geom)

        stream = merge(ret_o, ssd_o, att_o, stream, npre, w_gate, w_branch[i].astype(bf16), w_out[i].astype(bf16),
                       row(mix_norm_post[i]), mods, geom, rows)

        j = i // 2
        fpre, fpost = row(ffn_norm_pre[i]), row(ffn_norm_post[i])
        if i % 2 == 0:
            stream = dense_ffn(stream, fpre, fpost, mods, ffn_w1[j].astype(bf16), ffn_w3[j].astype(bf16),
                               ffn_w2[j].astype(bf16), geom, rows)
        else:
            stream = moe_ffn(stream, fpre, fpost, mods, moe_router[j], moe_w1[j].astype(bf16),
                             moe_w3[j].astype(bf16), moe_w2[j].astype(bf16), geom, rows)
    return stream[:nx].reshape(batch, seq, d)
```
